```python
import jax, jax.numpy as jnp
from jax import lax
import numpy as np

D_MODEL = 4096
BATCH = 2
SEQ = 4096
DEPTH = 2

GRID_W = 64
CTX_LEN = 256
ATT_HEAD_DIM = 64
ATT_HEADS = D_MODEL // 2 // ATT_HEAD_DIM
ATT_KV_HEADS = ATT_HEADS // 4
ATT_GROUP = ATT_HEADS // ATT_KV_HEADS
ATT_WIDTH = ATT_HEADS * ATT_HEAD_DIM
ATT_KV_WIDTH = ATT_KV_HEADS * ATT_HEAD_DIM
ATT_WINDOW = 128
ATT_BLOCK = 128
RET_HEADS = 8
RET_HEAD_DIM = D_MODEL // 2 // RET_HEADS
RET_WIDTH = RET_HEADS * RET_HEAD_DIM
RET_CHUNK = 128
N_EXPERTS = 32
TOP_K = 4
EXPERT_FF = D_MODEL // 8
SWIGLU_ALPHA = 1.702
SWIGLU_LIMIT = 7.0
ROPE_BASE = 10000.0
NORM_EPS = 1e-6
GN_EPS = 1e-5
NEG_INF = -1e30
COL_SIZES = (ATT_KV_WIDTH, ATT_KV_WIDTH, RET_WIDTH, RET_WIDTH,
             ATT_WIDTH, RET_WIDTH, RET_WIDTH, D_MODEL, D_MODEL)
SPLIT_POINTS = tuple(sum(COL_SIZES[:i + 1]) for i in range(len(COL_SIZES) - 1))
KV_COLS = sum(COL_SIZES[:4])
IN_COLS = sum(COL_SIZES)

kernel_name = 'hybrid_dit_window_gqa_retention_moe'


def rmsnorm(x, g):
    xf = x.astype(jnp.float32)
    y = xf * lax.rsqrt(jnp.mean(xf * xf, axis=-1, keepdims=True) + NORM_EPS)
    return y.astype(x.dtype) * g


def modulate(h, shift, scale):
    return h * (1 + scale) + shift


def axial_rope(seq, head_dim):
    rows = seq // GRID_W
    row = jnp.repeat(jnp.arange(rows, dtype=jnp.float32), GRID_W)
    col = jnp.tile(jnp.arange(GRID_W, dtype=jnp.float32), rows)
    n_freq = head_dim // 4
    inv_freq = ROPE_BASE ** (-jnp.arange(n_freq, dtype=jnp.float32) / n_freq)
    ang = jnp.concatenate([row[:, None] * inv_freq, col[:, None] * inv_freq], axis=-1)
    return jnp.cos(ang), jnp.sin(ang)


def apply_rope(x, cos, sin):
    x1, x2 = jnp.split(x, 2, axis=-1)
    c = cos[None, :, None, :]
    sn = sin[None, :, None, :]
    return jnp.concatenate([x1 * c - x2 * sn, x2 * c + x1 * sn], axis=-1).astype(x.dtype)


def sink_attention(q, k, v, valid, sink):
    s = jnp.einsum('bqhgd,bkhd->bhgqk', q, k).astype(jnp.float32)
    if valid is not None:
        s = jnp.where(valid, s, NEG_INF)
    sink_col = jnp.broadcast_to(sink.astype(jnp.float32)[None, :, :, None, None], s.shape[:-1] + (1,))
    p = jax.nn.softmax(jnp.concatenate([s, sink_col], axis=-1), axis=-1)[..., :-1]
    return jnp.einsum('bhgqk,bkhd->bqhgd', p.astype(v.dtype), v)


def window_attention(q, k, v, k_ctx, v_ctx, sink):
    b, s = q.shape[:2]
    nb = s // ATT_BLOCK
    n_ctx = k_ctx.shape[1]
    pad = ((0, 0), (ATT_BLOCK, ATT_BLOCK), (0, 0), (0, 0))
    kp = jnp.pad(k, pad)
    vp = jnp.pad(v, pad)
    offs_q = jnp.arange(ATT_BLOCK)
    offs_k = jnp.arange(3 * ATT_BLOCK)
    ctx_valid = jnp.ones((ATT_BLOCK, n_ctx), dtype=bool)

    def one_block(n):
        start = n * ATT_BLOCK
        q_n = lax.dynamic_slice_in_dim(q, start, ATT_BLOCK, axis=1)
        k_n = jnp.concatenate([lax.dynamic_slice_in_dim(kp, start, 3 * ATT_BLOCK, axis=1), k_ctx], axis=1)
        v_n = jnp.concatenate([lax.dynamic_slice_in_dim(vp, start, 3 * ATT_BLOCK, axis=1), v_ctx], axis=1)
        q_pos = start + offs_q
        k_pos = start - ATT_BLOCK + offs_k
        local = ((k_pos[None, :] >= 0) & (k_pos[None, :] < s)
                 & (jnp.abs(k_pos[None, :] - q_pos[:, None]) <= ATT_WINDOW))
        valid = jnp.concatenate([local, ctx_valid], axis=1)
        return sink_attention(q_n, k_n, v_n, valid, sink)

    out = lax.map(one_block, jnp.arange(nb))
    return jnp.moveaxis(out, 0, 1).reshape(b, s, ATT_WIDTH)


def retention_final_state(k, v, log_gamma):
    t = k.shape[1]
    w = jnp.exp(log_gamma[:, None] * (t - 1 - jnp.arange(t, dtype=jnp.float32))[None, :])
    return jnp.einsum('bthd,ht,bthv->bhdv', k, w, v)


def retention_chunkwise(q, k, v, log_gamma, s0):
    b, t, h, dk = q.shape
    dv = v.shape[-1]
    nc = t // RET_CHUNK
    qc = q.reshape(b, nc, RET_CHUNK, h, dk)
    kc = k.reshape(b, nc, RET_CHUNK, h, dk)
    vc = v.reshape(b, nc, RET_CHUNK, h, dv)
    idx = jnp.arange(RET_CHUNK, dtype=jnp.float32)
    lg = log_gamma[:, None]
    rel = idx[:, None] - idx[None, :]
    decay_mat = jnp.where(rel >= 0, jnp.exp(lg[:, :, None] * jnp.maximum(rel, 0.0)), 0.0)
    q_decay = jnp.exp(lg * (idx + 1.0))
    k_decay = jnp.exp(lg * (RET_CHUNK - 1.0 - idx))
    chunk_decay = jnp.exp(log_gamma * RET_CHUNK)[None, :, None, None]
    scores = jnp.einsum('bnihd,bnjhd->bnhij', qc, kc) * decay_mat
    intra = jnp.einsum('bnhij,bnjhv->bnihv', scores, vc)
    kv_chunk = jnp.einsum('bnjhd,hj,bnjhv->bnhdv', kc, k_decay, vc)

    def step(state, xs):
        q_n, kv_n = xs
        cross = jnp.einsum('bihd,hi,bhdv->bihv', q_n, q_decay, state)
        return chunk_decay * state + kv_n, cross

    _, cross = lax.scan(step, s0, (jnp.swapaxes(qc, 0, 1), jnp.swapaxes(kv_chunk, 0, 1)))
    return (intra + jnp.swapaxes(cross, 0, 1)).reshape(b, t, h, dv)


def bidirectional_retention(q, k, v, lg_f, lg_b, s0_f, s0_b):
    b, t = q.shape[:2]
    fwd = retention_chunkwise(q, k, v, lg_f, s0_f)
    bwd = retention_chunkwise(q[:, ::-1], k[:, ::-1], v[:, ::-1], lg_b, s0_b)[:, ::-1]
    o = (fwd + bwd).astype(jnp.float32)
    mu = jnp.mean(o, axis=-1, keepdims=True)
    var = jnp.mean(jnp.square(o - mu), axis=-1, keepdims=True)
    o = (o - mu) * lax.rsqrt(var + GN_EPS)
    return o.astype(q.dtype).reshape(b, t, RET_WIDTH)


def merge_branches(y_att, y_ret, g_ret, gate_att, gate_ret, w_branch_att, w_branch_ret, branch_gate_b, w_out):
    ya = y_att @ w_branch_att
    yr = (jax.nn.silu(g_ret) * y_ret) @ w_branch_ret
    ga = jax.nn.sigmoid(gate_att + branch_gate_b[:D_MODEL])
    gr = jax.nn.sigmoid(gate_ret + branch_gate_b[D_MODEL:])
    return (ga * ya + gr * yr) @ w_out


def token_mixer(h, hc, w_in, att_sink, ret_decay_fwd, ret_decay_bwd, w_branch_att, w_branch_ret,
                branch_gate_b, w_out, rope_att, rope_ret, ctx_out):
    b, s, _ = h.shape
    n_ctx = hc.shape[1]
    k_a, v_a, k_b, v_b, q_a, q_b, g_b, gate_att, gate_ret = jnp.split(h @ w_in, SPLIT_POINTS, axis=-1)
    if ctx_out:
        kc_a, vc_a, kc_b, vc_b, qc_a, qc_b, gc_b, gatec_att, gatec_ret = jnp.split(hc @ w_in, SPLIT_POINTS, axis=-1)
    else:
        kc_a, vc_a, kc_b, vc_b = jnp.split(hc @ w_in[:, :KV_COLS], SPLIT_POINTS[:3], axis=-1)

    sink = att_sink.reshape(ATT_KV_HEADS, ATT_GROUP)
    q_scale = ATT_HEAD_DIM ** -0.5
    kc_att = kc_a.reshape(b, n_ctx, ATT_KV_HEADS, ATT_HEAD_DIM)
    vc_att = vc_a.reshape(b, n_ctx, ATT_KV_HEADS, ATT_HEAD_DIM)
    qa = (apply_rope(q_a.reshape(b, s, ATT_HEADS, ATT_HEAD_DIM), *rope_att) * q_scale).reshape(
        b, s, ATT_KV_HEADS, ATT_GROUP, ATT_HEAD_DIM)
    ka = apply_rope(k_a.reshape(b, s, ATT_KV_HEADS, ATT_HEAD_DIM), *rope_att)
    va = v_a.reshape(b, s, ATT_KV_HEADS, ATT_HEAD_DIM)
    y_att = window_attention(qa, ka, va, kc_att, vc_att, sink)

    k_scale = RET_HEAD_DIM ** -0.5
    lg_f = jax.nn.log_sigmoid(ret_decay_fwd.astype(jnp.float32))
    lg_b = jax.nn.log_sigmoid(ret_decay_bwd.astype(jnp.float32))
    kc_ret = kc_b.reshape(b, n_ctx, RET_HEADS, RET_HEAD_DIM) * k_scale
    vc_ret = vc_b.reshape(b, n_ctx, RET_HEADS, RET_HEAD_DIM)
    s0_f = retention_final_state(kc_ret, vc_ret, lg_f)
    s0_b = retention_final_state(kc_ret[:, ::-1], vc_ret[:, ::-1], lg_b)
    qr = apply_rope(q_b.reshape(b, s, RET_HEADS, RET_HEAD_DIM), *rope_ret)
    kr = apply_rope(k_b.reshape(b, s, RET_HEADS, RET_HEAD_DIM), *rope_ret) * k_scale
    vr = v_b.reshape(b, s, RET_HEADS, RET_HEAD_DIM)
    y_ret = bidirectional_retention(qr, kr, vr, lg_f, lg_b, s0_f, s0_b)

    y = merge_branches(y_att, y_ret, g_b, gate_att, gate_ret, w_branch_att, w_branch_ret, branch_gate_b, w_out)
    if not ctx_out:
        return y, None

    qc_att = (qc_a.reshape(b, n_ctx, ATT_HEADS, ATT_HEAD_DIM) * q_scale).reshape(
        b, n_ctx, ATT_KV_HEADS, ATT_GROUP, ATT_HEAD_DIM)
    yc_att = sink_attention(qc_att, kc_att, vc_att, None, sink).reshape(b, n_ctx, ATT_WIDTH)
    zero_state = jnp.zeros((b, RET_HEADS, RET_HEAD_DIM, RET_HEAD_DIM), jnp.float32)
    yc_ret = bidirectional_retention(qc_b.reshape(b, n_ctx, RET_HEADS, RET_HEAD_DIM), kc_ret, vc_ret,
                                     lg_f, lg_b, zero_state, zero_state)
    yc = merge_branches(yc_att, yc_ret, gc_b, gatec_att, gatec_ret, w_branch_att, w_branch_ret, branch_gate_b, w_out)
    return y, yc


def moe_ffn(t, router_w, router_b, w_gu, b_gu, w_dn, b_dn):
    logits = (t @ router_w).astype(jnp.float32) + router_b.astype(jnp.float32)
    top_val, top_idx = lax.top_k(logits, TOP_K)
    weights = jax.nn.softmax(top_val, axis=-1)
    combine = jnp.einsum('nk,nke->en', weights, jax.nn.one_hot(top_idx, N_EXPERTS, dtype=jnp.float32)).astype(t.dtype)

    def expert(acc, xs):
        wg, bg, wd, bd, cw = xs
        gu = t @ wg + bg
        gate = jnp.minimum(gu[:, 0::2], SWIGLU_LIMIT)
        up = jnp.clip(gu[:, 1::2], -SWIGLU_LIMIT, SWIGLU_LIMIT)
        glu = gate * jax.nn.sigmoid(SWIGLU_ALPHA * gate)
        y = ((up + 1) * glu) @ wd + bd
        return acc + cw[:, None] * y, None

    out, _ = lax.scan(expert, jnp.zeros_like(t), (w_gu, b_gu, w_dn, b_dn, combine))
    return out


def setup_inputs(seed: int = 0) -> dict:
    key = jax.random.key(seed)
    ks = jax.random.split(key, 26)
    f32 = jnp.float32
    D = D_MODEL

    def nrm(k, shape, scale):
        return jax.random.normal(k, shape, f32) * scale

    heads = jnp.arange(RET_HEADS, dtype=f32)
    gamma0 = 1.0 - jnp.exp2(-(5.0 + heads))
    decay_logit0 = jnp.log(gamma0) - jnp.log1p(-gamma0)
    return {
        'x': nrm(ks[0], (BATCH, SEQ, D), 1.0),
        'c': nrm(ks[1], (BATCH, D), 1.0),
        'ctx': nrm(ks[2], (BATCH, CTX_LEN, D), 1.0),
        'c_ctx': nrm(ks[3], (D,), 1.0),
        'ada_w': nrm(ks[4], (DEPTH, D, 6 * D), 0.5 * D ** -0.5),
        'ada_b': nrm(ks[5], (DEPTH, 6 * D), 0.01),
        'norm_mix_g': 1.0 + nrm(ks[6], (DEPTH, D), 0.02),
        'norm_ffn_g': 1.0 + nrm(ks[7], (DEPTH, D), 0.02),
        'w_in': nrm(ks[8], (DEPTH, D, IN_COLS), D ** -0.5),
        'att_sink': nrm(ks[9], (DEPTH, ATT_HEADS), 0.5),
        'ret_decay_fwd': decay_logit0[None, :] + nrm(ks[10], (DEPTH, RET_HEADS), 0.1),
        'ret_decay_bwd': decay_logit0[None, :] + nrm(ks[11], (DEPTH, RET_HEADS), 0.1),
        'w_branch_att': nrm(ks[12], (DEPTH, ATT_WIDTH, D), ATT_WIDTH ** -0.5),
        'w_branch_ret': nrm(ks[13], (DEPTH, RET_WIDTH, D), RET_WIDTH ** -0.5),
        'branch_gate_b': nrm(ks[14], (DEPTH, 2 * D), 0.01),
        'w_out': nrm(ks[15], (DEPTH, D, D), D ** -0.5),
        'router_w': nrm(ks[16], (DEPTH, D, N_EXPERTS), D ** -0.5),
        'router_b': nrm(ks[17], (DEPTH, N_EXPERTS), 0.01),
        'exp_w_gu': nrm(ks[18], (DEPTH, N_EXPERTS, D, 2 * EXPERT_FF), D ** -0.5),
        'exp_b_gu': nrm(ks[19], (DEPTH, N_EXPERTS, 2 * EXPERT_FF), 0.01),
        'exp_w_down': nrm(ks[20], (DEPTH, N_EXPERTS, EXPERT_FF, D), EXPERT_FF ** -0.5),
        'exp_b_down': nrm(ks[21], (DEPTH, N_EXPERTS, D), 0.01),
        'final_norm_g': 1.0 + nrm(ks[22], (D,), 0.02),
    }


def reference(x, c, ctx, c_ctx, ada_w, ada_b, norm_mix_g, norm_ffn_g, w_in, att_sink, ret_decay_fwd,
              ret_decay_bwd, w_branch_att, w_branch_ret, branch_gate_b, w_out, router_w, router_b,
              exp_w_gu, exp_b_gu, exp_w_down, exp_b_down, final_norm_g):
    b, s, d = x.shape
    n_ctx = ctx.shape[1]
    rope_att = axial_rope(s, ATT_HEAD_DIM)
    rope_ret = axial_rope(s, RET_HEAD_DIM)
    xc = ctx
    for layer in range(DEPTH):
        last = layer == DEPTH - 1
        mod = jax.nn.silu(c) @ ada_w[layer] + ada_b[layer]
        sh1, sc1, g1, sh2, sc2, g2 = jnp.split(mod[:, None, :], 6, axis=-1)
        n_mod_c = 2 if last else 6
        mod_c = jax.nn.silu(c_ctx) @ ada_w[layer][:, :n_mod_c * d] + ada_b[layer][:n_mod_c * d]
        mod_c = jnp.split(mod_c, n_mod_c)

        h = modulate(rmsnorm(x, norm_mix_g[layer]), sh1, sc1)
        hc = modulate(rmsnorm(xc, norm_mix_g[layer]), mod_c[0], mod_c[1])
        y, yc = token_mixer(h, hc, w_in[layer], att_sink[layer], ret_decay_fwd[layer], ret_decay_bwd[layer],
                            w_branch_att[layer], w_branch_ret[layer], branch_gate_b[layer], w_out[layer],
                            rope_att, rope_ret, not last)
        x = x + g1 * y
        h2 = modulate(rmsnorm(x, norm_ffn_g[layer]), sh2, sc2).reshape(b * s, d)
        if last:
            f = moe_ffn(h2, router_w[layer], router_b[layer], exp_w_gu[layer], exp_b_gu[layer],
                        exp_w_down[layer], exp_b_down[layer])
            x = x + g2 * f.reshape(b, s, d)
        else:
            xc = xc + mod_c[2] * yc
            hc2 = modulate(rmsnorm(xc, norm_ffn_g[layer]), mod_c[3], mod_c[4]).reshape(b * n_ctx, d)
            f = moe_ffn(jnp.concatenate([h2, hc2], axis=0), router_w[layer], router_b[layer], exp_w_gu[layer],
                        exp_b_gu[layer], exp_w_down[layer], exp_b_down[layer])
            x = x + g2 * f[:b * s].reshape(b, s, d)
            xc = xc + mod_c[5] * f[b * s:].reshape(b, n_ctx, d)
    return rmsnorm(x, final_norm_g)
```

```python
import functools

import jax
import jax.numpy as jnp
from jax import lax
from jax.experimental import pallas as pl
from jax.experimental.pallas import tpu as pltpu

F32 = jnp.float32
BF16 = jnp.bfloat16

GRID_W = 64
ATT_HEAD_DIM = 64
ATT_GROUP = 4
ATT_WINDOW = 128
ATT_BLOCK = 128
RET_HEADS = 8
RET_CHUNK = 128
TOP_K = 4
SWIGLU_ALPHA = 1.702
SWIGLU_LIMIT = 7.0
ROPE_BASE = 10000.0
NORM_EPS = 1e-6
GN_EPS = 1e-5
NEG_INF = -1e30

LANES = 128
VMEM_LIMIT = 56 * 1024 * 1024


def _params(sem, vmem=VMEM_LIMIT):
    return pltpu.CompilerParams(dimension_semantics=sem, vmem_limit_bytes=vmem)


def _tile(n, cap, mult):
    for t in range(min(cap, n), 0, -1):
        if n % t == 0 and t % mult == 0:
            return t
    raise ValueError(f"no tile for {n} (cap {cap}, multiple of {mult})")


def _silu(v):
    return v * jax.nn.sigmoid(v)


def _mod_body(c_ref, w_ref, b_ref, o_ref):
    s = _silu(c_ref[...]).astype(BF16)
    o_ref[...] = jnp.dot(s, w_ref[...].astype(BF16), preferred_element_type=F32) + b_ref[...]


def _mod_call(cond, ada_w, ada_b):
    depth, d, n = ada_w.shape
    rows = cond.shape[0]
    tn = _tile(n, 512, LANES)
    return pl.pallas_call(
        _mod_body,
        grid=(depth, n // tn),
        in_specs=[
            pl.BlockSpec((rows, d), lambda l, j: (0, 0)),
            pl.BlockSpec((None, d, tn), lambda l, j: (l, 0, j)),
            pl.BlockSpec((None, 1, tn), lambda l, j: (l, 0, j)),
        ],
        out_specs=pl.BlockSpec((None, rows, tn), lambda l, j: (l, 0, j)),
        out_shape=jax.ShapeDtypeStruct((depth, rows, n), F32),
        compiler_params=_params(("arbitrary", "arbitrary")),
        name="adaln_mod",
    )(cond, ada_w, ada_b.reshape(depth, 1, n))


def _rms(x, g):
    ms = jnp.mean(x * x, axis=-1, keepdims=True)
    return x * lax.rsqrt(ms + NORM_EPS) * g


def _norm_mod_body(x_ref, g_ref, sh_ref, sc_ref, h_ref):
    h = _rms(x_ref[...], g_ref[...]) * (1.0 + sc_ref[...]) + sh_ref[...]
    h_ref[...] = h.astype(h_ref.dtype)


def _split_bf16(v):
    hi = v.astype(BF16)
    lo = (v - hi.astype(F32)).astype(BF16)
    return hi, lo


def _norm_router_body(x_ref, g_ref, sh_ref, sc_ref, rw_ref, rb_ref, h_ref, cw_ref):
    h = _rms(x_ref[...], g_ref[...]) * (1.0 + sc_ref[...]) + sh_ref[...]
    h_ref[...] = h.astype(h_ref.dtype)
    h_hi, h_lo = _split_bf16(h)
    w_hi, w_lo = _split_bf16(rw_ref[...])
    logits = (jnp.dot(h_hi, w_hi, preferred_element_type=F32)
              + jnp.dot(h_hi, w_lo, preferred_element_type=F32)
              + jnp.dot(h_lo, w_hi, preferred_element_type=F32)) + rb_ref[...]
    n_exp = logits.shape[-1]
    lane = lax.broadcasted_iota(jnp.int32, logits.shape, 1).astype(F32)
    work = logits
    chosen = jnp.zeros(logits.shape, jnp.bool_)
    top = None
    for k in range(TOP_K):
        m = jnp.max(work, axis=-1, keepdims=True)
        if k == 0:
            top = m
        first = jnp.min(jnp.where(work == m, lane, float(n_exp)), axis=-1, keepdims=True)
        pick = lane == first
        chosen = jnp.logical_or(chosen, pick)
        work = jnp.where(pick, -jnp.inf, work)
    e = jnp.where(chosen, jnp.exp(logits - top), 0.0)
    cw_ref[...] = e / jnp.sum(e, axis=-1, keepdims=True)


def _norm_plain_body(x_ref, g_ref, o_ref):
    o_ref[...] = _rms(x_ref[...], g_ref[...]).astype(o_ref.dtype)


def _seg_map(tiles_per_batch, n_batch, which):
    def index(i):
        return (jnp.minimum(i // tiles_per_batch, n_batch) * 6 + which, 0, 0)
    return index


def _norm_call(xa, g, modl, rows, seq, n_batch, shift_idx, scale_idx, router=None):
    d = xa.shape[1]
    tr = 256
    assert rows % tr == 0 and seq % tr == 0
    tpb = seq // tr
    row_spec = pl.BlockSpec((tr, d), lambda i: (i, 0))
    in_specs = [
        row_spec,
        pl.BlockSpec((1, d), lambda i: (0, 0)),
        pl.BlockSpec((None, 1, d), _seg_map(tpb, n_batch, shift_idx)),
        pl.BlockSpec((None, 1, d), _seg_map(tpb, n_batch, scale_idx)),
    ]
    args = [xa, g.reshape(1, d), modl, modl]
    if router is None:
        return pl.pallas_call(
            _norm_mod_body, grid=(rows // tr,), in_specs=in_specs, out_specs=row_spec,
            out_shape=jax.ShapeDtypeStruct((rows, d), BF16),
            compiler_params=_params(("arbitrary",)), name="norm_mod",
        )(*args)
    rw, rb = router
    n_exp = rw.shape[1]
    in_specs += [pl.BlockSpec((d, n_exp), lambda i: (0, 0)), pl.BlockSpec((1, n_exp), lambda i: (0, 0))]
    args += [rw, rb.reshape(1, n_exp)]
    return pl.pallas_call(
        _norm_router_body, grid=(rows // tr,), in_specs=in_specs,
        out_specs=[row_spec, pl.BlockSpec((tr, n_exp), lambda i: (i, 0))],
        out_shape=[jax.ShapeDtypeStruct((rows, d), BF16), jax.ShapeDtypeStruct((rows, n_exp), F32)],
        compiler_params=_params(("arbitrary",)), name="norm_router",
    )(*args)


def _final_norm_call(xa, g, rows):
    d = xa.shape[1]
    tr = 256
    row_spec = pl.BlockSpec((tr, d), lambda i: (i, 0))
    return pl.pallas_call(
        _norm_plain_body, grid=(rows // tr,),
        in_specs=[row_spec, pl.BlockSpec((1, d), lambda i: (0, 0))], out_specs=row_spec,
        out_shape=jax.ShapeDtypeStruct((rows, d), F32),
        compiler_params=_params(("arbitrary",)), name="final_norm",
    )(xa, g.reshape(1, d))


def _inproj_body(a_ref, w_ref, o_ref, wb_ref):
    @pl.when(pl.program_id(1) == 0)
    def _():
        wb_ref[...] = w_ref[...].astype(BF16)
    o_ref[...] = jnp.dot(a_ref[...], wb_ref[...], preferred_element_type=F32).astype(o_ref.dtype)


def _inproj_call(h, w_in, layer, rows, col_lo, col_hi):
    d = h.shape[1]
    tn = 512
    assert col_lo % tn == 0 and col_hi % tn == 0
    tm = _tile(rows, 1152, 16)
    j0 = col_lo // tn
    return pl.pallas_call(
        _inproj_body,
        grid=((col_hi - col_lo) // tn, rows // tm),
        in_specs=[
            pl.BlockSpec((tm, d), lambda j, i: (i, 0)),
            pl.BlockSpec((None, d, tn), lambda j, i: (layer, 0, j0 + j)),
        ],
        out_specs=pl.BlockSpec((tm, tn), lambda j, i: (i, j)),
        out_shape=jax.ShapeDtypeStruct((rows, col_hi - col_lo), BF16),
        scratch_shapes=[pltpu.VMEM((d, tn), BF16)],
        compiler_params=_params(("arbitrary", "arbitrary")),
        name="in_proj",
    )(h, w_in)


def _attn_body(*refs, local, seq, scale):
    if local:
        (sink_ref, q_ref, kp_ref, kc_ref, kn_ref, vp_ref, vc_ref, vn_ref, kx_ref, vx_ref,
         cp_ref, cc_ref, cn_ref, sp_ref, sc_ref, sn_ref, o_ref) = refs
    else:
        sink_ref, q_ref, kx_ref, vx_ref, o_ref = refs
    pair = pl.program_id(1)
    n = pl.program_id(2)
    blk = ATT_BLOCK
    half = ATT_HEAD_DIM // 2

    lane_q = lax.broadcasted_iota(jnp.int32, (blk, LANES), 1)
    first_half = (lane_q % ATT_HEAD_DIM) < half

    def rope(v, c_ref, s_ref):
        rot = jnp.where(first_half, pltpu.roll(v, LANES - half, 1), pltpu.roll(v, half, 1))
        return v * c_ref[...] + rot * s_ref[...]

    if local:
        keys = jnp.concatenate([
            rope(kp_ref[...].astype(F32), cp_ref, sp_ref),
            rope(kc_ref[...].astype(F32), cc_ref, sc_ref),
            rope(kn_ref[...].astype(F32), cn_ref, sn_ref),
            kx_ref[...].astype(F32)], axis=0)
        vals = jnp.concatenate([vp_ref[...], vc_ref[...], vn_ref[...], vx_ref[...]], axis=0).astype(F32)
    else:
        keys = kx_ref[...].astype(F32)
        vals = vx_ref[...].astype(F32)
    nk = keys.shape[0]

    lane_k = lax.broadcasted_iota(jnp.int32, (nk, LANES), 1)
    low = lane_k < ATT_HEAD_DIM

    def placed(v):
        swapped = pltpu.roll(v, ATT_HEAD_DIM, 1)
        return [[jnp.where(low, v, 0.0).astype(BF16), jnp.where(low, 0.0, swapped).astype(BF16)],
                [jnp.where(low, swapped, 0.0).astype(BF16), jnp.where(low, 0.0, v).astype(BF16)]]

    k_var = placed(keys)
    v_var = placed(vals)

    if local:
        row = lax.broadcasted_iota(jnp.int32, (2 * blk, nk), 0) % blk
        col = lax.broadcasted_iota(jnp.int32, (2 * blk, nk), 1)
        q_pos = n * blk + row
        k_pos = (n - 1) * blk + col
        valid = ((col >= 3 * blk)
                 | ((k_pos >= 0) & (k_pos < seq) & (jnp.abs(k_pos - q_pos) <= ATT_WINDOW)))
    top_rows = lax.broadcasted_iota(jnp.int32, (2 * blk, 1), 0) < blk

    n_slab = q_ref.shape[1] // LANES
    q_slabs = []
    for c in range(n_slab):
        qc = q_ref[:, c * LANES:(c + 1) * LANES].astype(F32)
        if local:
            qc = rope(qc, cc_ref, sc_ref)
        q_slabs.append((qc * scale).astype(BF16))

    out = [jnp.zeros((blk, LANES), F32) for _ in range(n_slab)]
    for j in range(2):
        lhs = jnp.concatenate([q_slabs[2 * j], q_slabs[2 * j + 1]], axis=0)
        for hh in range(2):
            s = lax.dot_general(lhs, k_var[j][hh], (((1,), (1,)), ((), ())),
                                preferred_element_type=F32)
            if local:
                s = jnp.where(valid, s, NEG_INF)
            head0 = pair * (2 * ATT_GROUP) + ATT_GROUP * j + hh
            sink = jnp.where(top_rows, sink_ref[head0], sink_ref[head0 + 2])
            m = jnp.maximum(jnp.max(s, axis=-1, keepdims=True), sink)
            e = jnp.exp(s - m)
            denom = jnp.sum(e, axis=-1, keepdims=True) + jnp.exp(sink - m)
            p = (e / denom).astype(BF16)
            o = jnp.dot(p, v_var[j][hh], preferred_element_type=F32)
            out[2 * j] = out[2 * j] + o[:blk]
            out[2 * j + 1] = out[2 * j + 1] + o[blk:]
    for c in range(n_slab):
        o_ref[:, c * LANES:(c + 1) * LANES] = out[c].astype(o_ref.dtype)


def _attn_call(proj, sink, rope_tabs, cols, n_batch, seq, n_ctx, att_width, local):
    blk = ATT_BLOCK
    nb = seq // blk
    pair_w = 2 * ATT_HEAD_DIM
    q_w = pair_w * ATT_GROUP
    n_pair = att_width // q_w
    k0 = cols["k_att"] // pair_w
    v0 = cols["v_att"] // pair_w
    q0 = cols["q_att"] // q_w
    ctx_row0 = (n_batch * seq) // n_ctx
    kx_spec = pl.BlockSpec((n_ctx, pair_w), lambda b, p, n: (ctx_row0 + b, k0 + p))
    vx_spec = pl.BlockSpec((n_ctx, pair_w), lambda b, p, n: (ctx_row0 + b, v0 + p))
    sink_spec = pl.BlockSpec(memory_space=pltpu.SMEM)
    scale = ATT_HEAD_DIM ** -0.5
    if local:
        cos_t, sin_t = rope_tabs

        def prev(n):
            return jnp.maximum(n - 1, 0)

        def nxt(n):
            return jnp.minimum(n + 1, nb - 1)

        def kv(c0, f):
            return pl.BlockSpec((blk, pair_w), lambda b, p, n: (b * nb + f(n), c0 + p))

        def tab(f):
            return pl.BlockSpec((blk, LANES), lambda b, p, n: (f(n), 0))

        ident = lambda n: n
        in_specs = [sink_spec,
                    pl.BlockSpec((blk, q_w), lambda b, p, n: (b * nb + n, q0 + p)),
                    kv(k0, prev), kv(k0, ident), kv(k0, nxt),
                    kv(v0, prev), kv(v0, ident), kv(v0, nxt),
                    kx_spec, vx_spec,
                    tab(prev), tab(ident), tab(nxt), tab(prev), tab(ident), tab(nxt)]
        args = [sink, proj] + [proj] * 6 + [proj, proj] + [cos_t] * 3 + [sin_t] * 3
        grid = (n_batch, n_pair, nb)
        rows = n_batch * seq
        out_spec = pl.BlockSpec((blk, q_w), lambda b, p, n: (b * nb + n, p))
    else:
        nqb = n_ctx // blk
        qrow0 = (n_batch * seq) // blk
        in_specs = [sink_spec,
                    pl.BlockSpec((blk, q_w), lambda b, p, n: (qrow0 + b * nqb + n, q0 + p)),
                    kx_spec, vx_spec]
        args = [sink, proj, proj, proj]
        grid = (n_batch, n_pair, nqb)
        rows = n_batch * n_ctx
        out_spec = pl.BlockSpec((blk, q_w), lambda b, p, n: (b * nqb + n, p))
    return pl.pallas_call(
        functools.partial(_attn_body, local=local, seq=seq, scale=scale),
        grid=grid, in_specs=in_specs, out_specs=out_spec,
        out_shape=jax.ShapeDtypeStruct((rows, att_width), BF16),
        compiler_params=_params(("arbitrary",) * 3),
        name="window_attn" if local else "ctx_attn",
    )(*args)


def _dot_tn(a, b):
    return lax.dot_general(a, b, (((0,), (0,)), ((), ())), preferred_element_type=F32)


def _ret_body(*refs, latent, n_chunks, n_ctx, k_scale):
    if latent:
        (lg_ref, q_ref, k_ref, v_ref, g_ref, kx_ref, vx_ref, cos_ref, sin_ref,
         o_ref, qs_ref, ks_ref, cb_ref, st_ref, sf_ref) = refs
    else:
        lg_ref, q_ref, k_ref, v_ref, g_ref, o_ref, qs_ref, ks_ref, cb_ref, st_ref, sf_ref = refs
    head = pl.program_id(1)
    lg_f = lg_ref[0, head]
    lg_b = lg_ref[1, head]
    c = RET_CHUNK
    hd = q_ref.shape[1]
    hh = hd // 2
    idx = lax.broadcasted_iota(jnp.int32, (c, 1), 0).astype(F32)
    qdec_f = jnp.exp(lg_f * (idx + 1.0))
    kdec_f = jnp.exp(lg_f * (c - 1.0 - idx))
    qdec_b = jnp.exp(lg_b * (c - idx))
    kdec_b = jnp.exp(lg_b * idx)
    one = jnp.ones((1, 1), F32)
    cdec_f = jnp.exp(lg_f * c * one)
    cdec_b = jnp.exp(lg_b * c * one)
    rel = (lax.broadcasted_iota(jnp.int32, (c, c), 0) - lax.broadcasted_iota(jnp.int32, (c, c), 1)).astype(F32)
    dmat = (jnp.where(rel >= 0, jnp.exp(lg_f * jnp.maximum(rel, 0.0)), 0.0)
            + jnp.where(rel <= 0, jnp.exp(lg_b * jnp.maximum(-rel, 0.0)), 0.0))

    def rope(v, rows):
        v1 = v[:, :hh]
        v2 = v[:, hh:]
        cs = cos_ref[rows, :]
        sn = sin_ref[rows, :]
        return jnp.concatenate([v1 * cs - v2 * sn, v2 * cs + v1 * sn], axis=1)

    if latent:
        t = lax.broadcasted_iota(jnp.int32, (n_ctx, 1), 0).astype(F32)
        kx = kx_ref[...].astype(F32) * k_scale
        sf_ref[...] = _dot_tn((kx * jnp.exp(lg_f * (n_ctx - 1.0 - t))).astype(BF16), vx_ref[...])
        st_ref[...] = _dot_tn((kx * jnp.exp(lg_b * t)).astype(BF16), vx_ref[...])
    else:
        sf_ref[...] = jnp.zeros((hd, hd), F32)
        st_ref[...] = jnp.zeros((hd, hd), F32)


    def bwd(step, carry):
        rows = pl.ds(pl.multiple_of((n_chunks - 1 - step) * c, c), c)
        q = q_ref[rows, :].astype(F32)
        k = k_ref[rows, :].astype(F32)
        if latent:
            q = rope(q, rows)
            k = rope(k, rows)
        k = k * k_scale
        qs_ref[rows, :] = q.astype(BF16)
        ks_ref[rows, :] = k.astype(BF16)
        state = st_ref[...]
        cb_ref[rows, :] = jnp.dot((q * qdec_b).astype(BF16), state.astype(BF16), preferred_element_type=F32)
        st_ref[...] = cdec_b * state + _dot_tn((k * kdec_b).astype(BF16), v_ref[rows, :])
        return carry

    lax.fori_loop(0, n_chunks, bwd, 0)

    st_ref[...] = sf_ref[...]

    def fwd(step, carry):
        rows = pl.ds(pl.multiple_of(step * c, c), c)
        qb = qs_ref[rows, :]
        kb = ks_ref[rows, :]
        v = v_ref[rows, :]
        scores = lax.dot_general(qb, kb, (((1,), (1,)), ((), ())), preferred_element_type=F32) * dmat
        o = jnp.dot(scores.astype(BF16), v, preferred_element_type=F32)
        state = st_ref[...]
        o = o + jnp.dot((qb.astype(F32) * qdec_f).astype(BF16), state.astype(BF16), preferred_element_type=F32)
        o = o + cb_ref[rows, :]
        st_ref[...] = cdec_f * state + _dot_tn((kb.astype(F32) * kdec_f).astype(BF16), v)
        mu = jnp.mean(o, axis=-1, keepdims=True)
        dev = o - mu
        var = jnp.mean(dev * dev, axis=-1, keepdims=True)
        y = dev * lax.rsqrt(var + GN_EPS)
        o_ref[rows, :] = (_silu(g_ref[rows, :].astype(F32)) * y).astype(o_ref.dtype)
        return carry

    lax.fori_loop(0, n_chunks, fwd, 0)


def _ret_call(proj, log_gamma, rope_tabs, cols, n_batch, seq, n_ctx, ret_width, latent):
    hd = ret_width // RET_HEADS
    t_len = seq if latent else n_ctx
    assert t_len % RET_CHUNK == 0
    k0, v0, q0, g0 = (cols[name] // hd for name in ("k_ret", "v_ret", "q_ret", "g_ret"))
    ctx_row0 = (n_batch * seq) // n_ctx
    row0 = 0 if latent else ctx_row0

    def seq_spec(c0):
        return pl.BlockSpec((t_len, hd), lambda b, h: (row0 + b, c0 + h))

    in_specs = [pl.BlockSpec(memory_space=pltpu.SMEM),
                seq_spec(q0), seq_spec(k0), seq_spec(v0), seq_spec(g0)]
    args = [log_gamma, proj, proj, proj, proj]
    if latent:
        cos_t, sin_t = rope_tabs
        in_specs += [pl.BlockSpec((n_ctx, hd), lambda b, h: (ctx_row0 + b, k0 + h)),
                     pl.BlockSpec((n_ctx, hd), lambda b, h: (ctx_row0 + b, v0 + h)),
                     pl.BlockSpec((t_len, hd // 2), lambda b, h: (0, 0)),
                     pl.BlockSpec((t_len, hd // 2), lambda b, h: (0, 0))]
        args += [proj, proj, cos_t, sin_t]
    return pl.pallas_call(
        functools.partial(_ret_body, latent=latent, n_chunks=t_len // RET_CHUNK, n_ctx=n_ctx,
                          k_scale=hd ** -0.5),
        grid=(n_batch, RET_HEADS), in_specs=in_specs,
        out_specs=pl.BlockSpec((t_len, hd), lambda b, h: (b, h)),
        out_shape=jax.ShapeDtypeStruct((n_batch * t_len, ret_width), BF16),
        scratch_shapes=[pltpu.VMEM((t_len, hd), BF16), pltpu.VMEM((t_len, hd), BF16),
                        pltpu.VMEM((t_len, hd), F32), pltpu.VMEM((hd, hd), F32), pltpu.VMEM((hd, hd), F32)],
        compiler_params=_params(("arbitrary", "arbitrary")),
        name="retention" if latent else "ctx_retention",
    )(*args)


def _merge_gate_body(ya_ref, yr_ref, ga_ref, gr_ref, wa_ref, wr_ref, ba_ref, br_ref, z_ref, wab_ref, wrb_ref):
    @pl.when(pl.program_id(1) == 0)
    def _():
        wab_ref[...] = wa_ref[...].astype(BF16)
        wrb_ref[...] = wr_ref[...].astype(BF16)
    a = jnp.dot(ya_ref[...], wab_ref[...], preferred_element_type=F32)
    r = jnp.dot(yr_ref[...], wrb_ref[...], preferred_element_type=F32)
    ga = jax.nn.sigmoid(ga_ref[...].astype(F32) + ba_ref[...])
    gr = jax.nn.sigmoid(gr_ref[...].astype(F32) + br_ref[...])
    z_ref[...] = (ga * a + gr * r).astype(z_ref.dtype)


def _merge_gate_call(ya, yr, proj, w_att, w_ret, gate_b, layer, rows, cols):
    d = w_att.shape[2]
    wa_rows, wr_rows = w_att.shape[1], w_ret.shape[1]
    tn = 512
    tm = _tile(rows, 1024, 16)
    ga0 = cols["gate_att"] // tn
    gr0 = cols["gate_ret"] // tn
    gb = gate_b.reshape(gate_b.shape[0], 1, 2 * d)
    return pl.pallas_call(
        _merge_gate_body,
        grid=(d // tn, rows // tm),
        in_specs=[
            pl.BlockSpec((tm, wa_rows), lambda j, i: (i, 0)),
            pl.BlockSpec((tm, wr_rows), lambda j, i: (i, 0)),
            pl.BlockSpec((tm, tn), lambda j, i: (i, ga0 + j)),
            pl.BlockSpec((tm, tn), lambda j, i: (i, gr0 + j)),
            pl.BlockSpec((None, wa_rows, tn), lambda j, i: (layer, 0, j)),
            pl.BlockSpec((None, wr_rows, tn), lambda j, i: (layer, 0, j)),
            pl.BlockSpec((None, 1, tn), lambda j, i: (layer, 0, j)),
            pl.BlockSpec((None, 1, tn), lambda j, i: (layer, 0, d // tn + j)),
        ],
        out_specs=pl.BlockSpec((tm, tn), lambda j, i: (i, j)),
        out_shape=jax.ShapeDtypeStruct((rows, d), BF16),
        scratch_shapes=[pltpu.VMEM((wa_rows, tn), BF16), pltpu.VMEM((wr_rows, tn), BF16)],
        compiler_params=_params(("arbitrary", "arbitrary")),
        name="merge_gate",
    )(ya, yr, proj, proj, w_att, w_ret, gb, gb)


def _resid_matmul_body(z_ref, w_ref, x_ref, g_ref, o_ref, wb_ref):
    @pl.when(pl.program_id(1) == 0)
    def _():
        wb_ref[...] = w_ref[...].astype(BF16)
    y = jnp.dot(z_ref[...], wb_ref[...], preferred_element_type=F32)
    o_ref[...] = x_ref[...] + g_ref[...] * y


def _resid_matmul_call(z, w, layer, xa, modl, gate_idx, rows, seq, n_batch):
    k, d = w.shape[1], w.shape[2]
    tn = 512
    tm = 512
    assert rows % tm == 0 and seq % tm == 0
    tpb = seq // tm

    def gate_map(j, i):
        return (jnp.minimum(i // tpb, n_batch) * 6 + gate_idx, 0, j)

    return pl.pallas_call(
        _resid_matmul_body,
        grid=(d // tn, rows // tm),
        in_specs=[
            pl.BlockSpec((tm, k), lambda j, i: (i, 0)),
            pl.BlockSpec((None, k, tn), lambda j, i: (layer, 0, j)),
            pl.BlockSpec((tm, tn), lambda j, i: (i, j)),
            pl.BlockSpec((None, 1, tn), gate_map),
        ],
        out_specs=pl.BlockSpec((tm, tn), lambda j, i: (i, j)),
        out_shape=jax.ShapeDtypeStruct((rows, d), F32),
        scratch_shapes=[pltpu.VMEM((k, tn), BF16)],
        compiler_params=_params(("arbitrary", "arbitrary")),
        name="resid_matmul",
    )(z, w, xa, modl)


def _expert_up_body(h_ref, wg_ref, wu_ref, bg_ref, bu_ref, cw_ref, a_ref):
    h = h_ref[...]
    gate = jnp.minimum(jnp.dot(h, wg_ref[...], preferred_element_type=F32) + bg_ref[...], SWIGLU_LIMIT)
    up = jnp.clip(jnp.dot(h, wu_ref[...], preferred_element_type=F32) + bu_ref[...], -SWIGLU_LIMIT, SWIGLU_LIMIT)
    glu = gate * jax.nn.sigmoid(SWIGLU_ALPHA * gate)
    a_ref[...] = (cw_ref[...] * ((up + 1.0) * glu)).astype(a_ref.dtype)


def _expert_up_call(h2, wg, wu, bg, bu, cw_t, rows):
    n_exp, d, ff = wg.shape
    tm = _tile(rows, 1152, 16)
    return pl.pallas_call(
        _expert_up_body,
        grid=(n_exp, rows // tm),
        in_specs=[
            pl.BlockSpec((tm, d), lambda e, i: (i, 0)),
            pl.BlockSpec((None, d, ff), lambda e, i: (e, 0, 0)),
            pl.BlockSpec((None, d, ff), lambda e, i: (e, 0, 0)),
            pl.BlockSpec((None, 1, ff), lambda e, i: (e, 0, 0)),
            pl.BlockSpec((None, 1, ff), lambda e, i: (e, 0, 0)),
            pl.BlockSpec((None, tm, 1), lambda e, i: (e, i, 0)),
        ],
        out_specs=pl.BlockSpec((tm, ff), lambda e, i: (i, e)),
        out_shape=jax.ShapeDtypeStruct((rows, n_exp * ff), BF16),
        compiler_params=_params(("arbitrary", "arbitrary")),
        name="expert_up",
    )(h2, wg, wu, bg, bu, cw_t)


def _expert_down_body(a_ref, w_ref, cw_ref, bd_ref, x_ref, g_ref, o_ref, acc_ref):
    kk = pl.program_id(2)

    @pl.when(kk == 0)
    def _():
        acc_ref[...] = jnp.dot(cw_ref[...].astype(BF16), bd_ref[...].astype(BF16), preferred_element_type=F32)

    acc_ref[...] += jnp.dot(a_ref[...], w_ref[...], preferred_element_type=F32)

    @pl.when(kk == pl.num_programs(2) - 1)
    def _():
        o_ref[...] = x_ref[...] + g_ref[...] * acc_ref[...]


def _expert_down_call(act, wd, cw, bd, xa, modl, gate_idx, rows, seq, n_batch):
    k, d = wd.shape
    n_exp = cw.shape[1]
    tm, tn, tk = 512, 1024, 2048
    assert rows % tm == 0 and seq % tm == 0 and d % tn == 0 and k % tk == 0
    tpb = seq // tm

    def gate_map(j, i, kk):
        return (jnp.minimum(i // tpb, n_batch) * 6 + gate_idx, 0, j)

    return pl.pallas_call(
        _expert_down_body,
        grid=(d // tn, rows // tm, k // tk),
        in_specs=[
            pl.BlockSpec((tm, tk), lambda j, i, kk: (i, kk)),
            pl.BlockSpec((tk, tn), lambda j, i, kk: (kk, j)),
            pl.BlockSpec((tm, n_exp), lambda j, i, kk: (i, 0)),
            pl.BlockSpec((n_exp, tn), lambda j, i, kk: (0, j)),
            pl.BlockSpec((tm, tn), lambda j, i, kk: (i, j)),
            pl.BlockSpec((None, 1, tn), gate_map),
        ],
        out_specs=pl.BlockSpec((tm, tn), lambda j, i, kk: (i, j)),
        out_shape=jax.ShapeDtypeStruct((rows, d), F32),
        scratch_shapes=[pltpu.VMEM((tm, tn), F32)],
        compiler_params=_params(("arbitrary", "arbitrary", "arbitrary")),
        name="expert_down",
    )(act, wd, cw, bd, xa, modl)


def _axial_angles(seq, head_dim):
    rows = seq // GRID_W
    row = jnp.repeat(jnp.arange(rows, dtype=F32), GRID_W)
    col = jnp.tile(jnp.arange(GRID_W, dtype=F32), rows)
    n_freq = head_dim // 4
    inv_freq = ROPE_BASE ** (-jnp.arange(n_freq, dtype=F32) / n_freq)
    return jnp.concatenate([row[:, None] * inv_freq, col[:, None] * inv_freq], axis=-1)


def _rope_tables(seq, ret_head_dim):
    ang_a = _axial_angles(seq, ATT_HEAD_DIM)
    reps = LANES // ATT_HEAD_DIM
    cos_a = jnp.tile(jnp.concatenate([jnp.cos(ang_a), jnp.cos(ang_a)], axis=-1), (1, reps))
    sin_a = jnp.tile(jnp.concatenate([-jnp.sin(ang_a), jnp.sin(ang_a)], axis=-1), (1, reps))
    ang_r = _axial_angles(seq, ret_head_dim)
    return (cos_a, sin_a), (jnp.cos(ang_r), jnp.sin(ang_r))


def kernel(x, c, ctx, c_ctx, ada_w, ada_b, norm_mix_g, norm_ffn_g, w_in, att_sink, ret_decay_fwd,
           ret_decay_bwd, w_branch_att, w_branch_ret, branch_gate_b, w_out, router_w, router_b,
           exp_w_gu, exp_b_gu, exp_w_down, exp_b_down, final_norm_g):
    n_batch, seq, d = x.shape
    n_ctx = ctx.shape[1]
    depth = ada_w.shape[0]
    n_exp, ff = exp_w_down.shape[1], exp_w_down.shape[2]
    att_width = w_branch_att.shape[1]
    ret_width = w_branch_ret.shape[1]
    kv_width = att_width // ATT_GROUP
    sizes = (("k_att", kv_width), ("v_att", kv_width), ("k_ret", ret_width), ("v_ret", ret_width),
             ("q_att", att_width), ("q_ret", ret_width), ("g_ret", ret_width),
             ("gate_att", d), ("gate_ret", d))
    cols, off = {}, 0
    for name, width in sizes:
        cols[name] = off
        off += width
    in_cols = off
    assert in_cols == w_in.shape[2]

    lat_rows = n_batch * seq
    all_rows = lat_rows + n_batch * n_ctx
    xa = jnp.concatenate([x.reshape(lat_rows, d), ctx.reshape(n_batch * n_ctx, d)], axis=0)

    cond = jnp.zeros((16, d), F32).at[:n_batch].set(c).at[n_batch].set(c_ctx)
    mod = _mod_call(cond, ada_w, ada_b)
    rope_att, rope_ret = _rope_tables(seq, ret_width // RET_HEADS)

    for layer in range(depth):
        last = layer == depth - 1
        rows = lat_rows if last else all_rows
        modl = mod[layer, :n_batch + 1].reshape((n_batch + 1) * 6, 1, d)
        log_gamma = jnp.stack([jax.nn.log_sigmoid(ret_decay_fwd[layer].astype(F32)),
                               jax.nn.log_sigmoid(ret_decay_bwd[layer].astype(F32))])

        h = _norm_call(xa, norm_mix_g[layer], modl, all_rows, seq, n_batch, 0, 1)
        proj = _inproj_call(h, w_in, layer, all_rows, 0, in_cols)

        ya = _attn_call(proj, att_sink[layer], rope_att, cols, n_batch, seq, n_ctx, att_width, True)
        yr = _ret_call(proj, log_gamma, rope_ret, cols, n_batch, seq, n_ctx, ret_width, True)
        if not last:
            ya_c = _attn_call(proj, att_sink[layer], None, cols, n_batch, seq, n_ctx, att_width, False)
            yr_c = _ret_call(proj, log_gamma, None, cols, n_batch, seq, n_ctx, ret_width, False)
            ya = jnp.concatenate([ya, ya_c], axis=0)
            yr = jnp.concatenate([yr, yr_c], axis=0)

        z = _merge_gate_call(ya, yr, proj, w_branch_att, w_branch_ret, branch_gate_b, layer, rows, cols)
        xa = _resid_matmul_call(z, w_out, layer, xa, modl, 2, rows, seq, n_batch)

        h2, cw = _norm_call(xa, norm_ffn_g[layer], modl, rows, seq, n_batch, 3, 4,
                            router=(router_w[layer], router_b[layer]))
        wgu = exp_w_gu[layer].reshape(n_exp, d, ff, 2)
        wg = wgu[..., 0].astype(BF16)
        wu = wgu[..., 1].astype(BF16)
        bgu = exp_b_gu[layer].reshape(n_exp, 1, ff, 2)
        wd = exp_w_down[layer].reshape(n_exp * ff, d).astype(BF16)
        cw_t = cw.T.reshape(n_exp, rows, 1)
        act = _expert_up_call(h2, wg, wu, bgu[..., 0], bgu[..., 1], cw_t, rows)
        xa = _expert_down_call(act, wd, cw, exp_b_down[layer], xa, modl, 5, rows, seq, n_batch)

    out = _final_norm_call(xa, final_norm_g, lat_rows)
    return out.reshape(n_batch, seq, d)
```

```python
import functools

import jax
import jax.numpy as jnp
from jax import lax
from jax.experimental import pallas as pl
from jax.experimental.pallas import tpu as pltpu

F32 = jnp.float32
BF16 = jnp.bfloat16

GRID_W = 64
ATT_HEAD_DIM = 64
ATT_GROUP = 4
ATT_WINDOW = 128
ATT_BLOCK = 128
RET_HEADS = 8
RET_CHUNK = 128
TOP_K = 4
SWIGLU_ALPHA = 1.702
SWIGLU_LIMIT = 7.0
ROPE_BASE = 10000.0
NORM_EPS = 1e-6
GN_EPS = 1e-5
NEG_INF = -1e30

LANES = 128
VMEM_LIMIT = 56 * 1024 * 1024


def _params(sem, vmem=VMEM_LIMIT):
    return pltpu.CompilerParams(dimension_semantics=sem, vmem_limit_bytes=vmem)


def _tile(n, cap, mult):
    for t in range(min(cap, n), 0, -1):
        if n % t == 0 and t % mult == 0:
            return t
    raise ValueError(f"no tile for {n} (cap {cap}, multiple of {mult})")


def _silu(v):
    return v * jax.nn.sigmoid(v)


def _mod_body(c_ref, w_ref, b_ref, o_ref):
    s = _silu(c_ref[...]).astype(BF16)
    o_ref[...] = jnp.dot(s, w_ref[...].astype(BF16), preferred_element_type=F32) + b_ref[...]


def _mod_call(cond, ada_w, ada_b):
    depth, d, n = ada_w.shape
    rows = cond.shape[0]
    tn = _tile(n, 512, LANES)
    return pl.pallas_call(
        _mod_body,
        grid=(depth, n // tn),
        in_specs=[
            pl.BlockSpec((rows, d), lambda l, j: (0, 0)),
            pl.BlockSpec((None, d, tn), lambda l, j: (l, 0, j)),
            pl.BlockSpec((None, 1, tn), lambda l, j: (l, 0, j)),
        ],
        out_specs=pl.BlockSpec((None, rows, tn), lambda l, j: (l, 0, j)),
        out_shape=jax.ShapeDtypeStruct((depth, rows, n), F32),
        compiler_params=_params(("arbitrary", "arbitrary")),
        name="adaln_mod",
    )(cond, ada_w, ada_b.reshape(depth, 1, n))


def _rms(x, g):
    ms = jnp.mean(x * x, axis=-1, keepdims=True)
    return x * lax.rsqrt(ms + NORM_EPS) * g


def _norm_mod_body(x_ref, g_ref, sh_ref, sc_ref, h_ref):
    h = _rms(x_ref[...], g_ref[...]) * (1.0 + sc_ref[...]) + sh_ref[...]
    h_ref[...] = h.astype(h_ref.dtype)


def _split_bf16(v):
    hi = v.astype(BF16)
    lo = (v - hi.astype(F32)).astype(BF16)
    return hi, lo


def _norm_router_body(x_ref, g_ref, sh_ref, sc_ref, rw_ref, rb_ref, h_ref, idx_ref, wt_ref):
    h = _rms(x_ref[...], g_ref[...]) * (1.0 + sc_ref[...]) + sh_ref[...]
    h_ref[...] = h
    h_hi, h_lo = _split_bf16(h)
    w_hi, w_lo = _split_bf16(rw_ref[...])
    logits = (jnp.dot(h_hi, w_hi, preferred_element_type=F32)
              + jnp.dot(h_hi, w_lo, preferred_element_type=F32)
              + jnp.dot(h_lo, w_hi, preferred_element_type=F32)) + rb_ref[...]
    n_exp = logits.shape[-1]
    lane = lax.broadcasted_iota(jnp.int32, logits.shape, 1).astype(F32)
    work = logits
    out_lane = lax.broadcasted_iota(jnp.int32, idx_ref.shape, 1)
    idx = jnp.zeros(idx_ref.shape, F32)
    ex = jnp.zeros(wt_ref.shape, F32)
    top = None
    total = None
    for k in range(TOP_K):
        m = jnp.max(work, axis=-1, keepdims=True)
        first = jnp.min(jnp.where(work == m, lane, float(n_exp)), axis=-1, keepdims=True)
        work = jnp.where(lane == first, -jnp.inf, work)
        if k == 0:
            top = m
        e = jnp.exp(m - top)
        total = e if k == 0 else total + e
        idx = jnp.where(out_lane == k, first, idx)
        ex = jnp.where(out_lane == k, e, ex)
    idx_ref[...] = idx.astype(jnp.int32)
    wt_ref[...] = ex / total


def _norm_plain_body(x_ref, g_ref, o_ref):
    o_ref[...] = _rms(x_ref[...], g_ref[...]).astype(o_ref.dtype)


def _seg_map(tiles_per_batch, n_batch, which):
    def index(i):
        return (jnp.minimum(i // tiles_per_batch, n_batch) * 6 + which, 0, 0)
    return index


def _norm_call(xa, g, modl, rows, seq, n_batch, shift_idx, scale_idx, router=None):
    d = xa.shape[1]
    tr = 256
    assert rows % tr == 0 and seq % tr == 0
    tpb = seq // tr
    row_spec = pl.BlockSpec((tr, d), lambda i: (i, 0))
    in_specs = [
        row_spec,
        pl.BlockSpec((1, d), lambda i: (0, 0)),
        pl.BlockSpec((None, 1, d), _seg_map(tpb, n_batch, shift_idx)),
        pl.BlockSpec((None, 1, d), _seg_map(tpb, n_batch, scale_idx)),
    ]
    args = [xa, g.reshape(1, d), modl, modl]
    if router is None:
        return pl.pallas_call(
            _norm_mod_body, grid=(rows // tr,), in_specs=in_specs, out_specs=row_spec,
            out_shape=jax.ShapeDtypeStruct((rows, d), BF16),
            compiler_params=_params(("arbitrary",)), name="norm_mod",
        )(*args)
    rw, rb = router
    n_exp = rw.shape[1]
    in_specs += [pl.BlockSpec((d, n_exp), lambda i: (0, 0)), pl.BlockSpec((1, n_exp), lambda i: (0, 0))]
    args += [rw, rb.reshape(1, n_exp)]
    return pl.pallas_call(
        _norm_router_body, grid=(rows // tr,), in_specs=in_specs,
        out_specs=[row_spec, pl.BlockSpec((tr, LANES), lambda i: (i, 0)), pl.BlockSpec((tr, LANES), lambda i: (i, 0))],
        out_shape=[jax.ShapeDtypeStruct((rows, d), F32), jax.ShapeDtypeStruct((rows, LANES), jnp.int32),
                   jax.ShapeDtypeStruct((rows, LANES), F32)],
        compiler_params=_params(("arbitrary",)), name="norm_router",
    )(*args)


def _final_norm_call(xa, g, rows):
    d = xa.shape[1]
    tr = 256
    row_spec = pl.BlockSpec((tr, d), lambda i: (i, 0))
    return pl.pallas_call(
        _norm_plain_body, grid=(rows // tr,),
        in_specs=[row_spec, pl.BlockSpec((1, d), lambda i: (0, 0))], out_specs=row_spec,
        out_shape=jax.ShapeDtypeStruct((rows, d), F32),
        compiler_params=_params(("arbitrary",)), name="final_norm",
    )(xa, g.reshape(1, d))


def _inproj_body(a_ref, w_ref, o_ref, wb_ref):
    @pl.when(pl.program_id(1) == 0)
    def _():
        wb_ref[...] = w_ref[...].astype(BF16)
    o_ref[...] = jnp.dot(a_ref[...], wb_ref[...], preferred_element_type=F32).astype(o_ref.dtype)


def _inproj_call(h, w_in, layer, rows, col_lo, col_hi):
    d = h.shape[1]
    tn = 512
    assert col_lo % tn == 0 and col_hi % tn == 0
    tm = _tile(rows, 1152, 16)
    j0 = col_lo // tn
    return pl.pallas_call(
        _inproj_body,
        grid=((col_hi - col_lo) // tn, rows // tm),
        in_specs=[
            pl.BlockSpec((tm, d), lambda j, i: (i, 0)),
            pl.BlockSpec((None, d, tn), lambda j, i: (layer, 0, j0 + j)),
        ],
        out_specs=pl.BlockSpec((tm, tn), lambda j, i: (i, j)),
        out_shape=jax.ShapeDtypeStruct((rows, col_hi - col_lo), BF16),
        scratch_shapes=[pltpu.VMEM((d, tn), BF16)],
        compiler_params=_params(("arbitrary", "arbitrary")),
        name="in_proj",
    )(h, w_in)


def _attn_body(*refs, local, seq, scale):
    if local:
        (sink_ref, q_ref, kp_ref, kc_ref, kn_ref, vp_ref, vc_ref, vn_ref, kx_ref, vx_ref,
         cp_ref, cc_ref, cn_ref, sp_ref, sc_ref, sn_ref, o_ref) = refs
    else:
        sink_ref, q_ref, kx_ref, vx_ref, o_ref = refs
    pair = pl.program_id(1)
    n = pl.program_id(2)
    blk = ATT_BLOCK
    half = ATT_HEAD_DIM // 2

    lane_q = lax.broadcasted_iota(jnp.int32, (blk, LANES), 1)
    first_half = (lane_q % ATT_HEAD_DIM) < half

    def rope(v, c_ref, s_ref):
        rot = jnp.where(first_half, pltpu.roll(v, LANES - half, 1), pltpu.roll(v, half, 1))
        return v * c_ref[...] + rot * s_ref[...]

    if local:
        keys = jnp.concatenate([
            rope(kp_ref[...].astype(F32), cp_ref, sp_ref),
            rope(kc_ref[...].astype(F32), cc_ref, sc_ref),
            rope(kn_ref[...].astype(F32), cn_ref, sn_ref),
            kx_ref[...].astype(F32)], axis=0)
        vals = jnp.concatenate([vp_ref[...], vc_ref[...], vn_ref[...], vx_ref[...]], axis=0).astype(F32)
    else:
        keys = kx_ref[...].astype(F32)
        vals = vx_ref[...].astype(F32)
    nk = keys.shape[0]

    lane_k = lax.broadcasted_iota(jnp.int32, (nk, LANES), 1)
    low = lane_k < ATT_HEAD_DIM

    def placed(v):
        swapped = pltpu.roll(v, ATT_HEAD_DIM, 1)
        return [[jnp.where(low, v, 0.0).astype(BF16), jnp.where(low, 0.0, swapped).astype(BF16)],
                [jnp.where(low, swapped, 0.0).astype(BF16), jnp.where(low, 0.0, v).astype(BF16)]]

    k_var = placed(keys)
    v_var = placed(vals)

    if local:
        row = lax.broadcasted_iota(jnp.int32, (2 * blk, nk), 0) % blk
        col = lax.broadcasted_iota(jnp.int32, (2 * blk, nk), 1)
        q_pos = n * blk + row
        k_pos = (n - 1) * blk + col
        valid = ((col >= 3 * blk)
                 | ((k_pos >= 0) & (k_pos < seq) & (jnp.abs(k_pos - q_pos) <= ATT_WINDOW)))
    top_rows = lax.broadcasted_iota(jnp.int32, (2 * blk, 1), 0) < blk

    n_slab = q_ref.shape[1] // LANES
    q_slabs = []
    for c in range(n_slab):
        qc = q_ref[:, c * LANES:(c + 1) * LANES].astype(F32)
        if local:
            qc = rope(qc, cc_ref, sc_ref)
        q_slabs.append((qc * scale).astype(BF16))

    out = [jnp.zeros((blk, LANES), F32) for _ in range(n_slab)]
    for j in range(2):
        lhs = jnp.concatenate([q_slabs[2 * j], q_slabs[2 * j + 1]], axis=0)
        for hh in range(2):
            s = lax.dot_general(lhs, k_var[j][hh], (((1,), (1,)), ((), ())),
                                preferred_element_type=F32)
            if local:
                s = jnp.where(valid, s, NEG_INF)
            head0 = pair * (2 * ATT_GROUP) + ATT_GROUP * j + hh
            sink = jnp.where(top_rows, sink_ref[head0], sink_ref[head0 + 2])
            m = jnp.maximum(jnp.max(s, axis=-1, keepdims=True), sink)
            e = jnp.exp(s - m)
            denom = jnp.sum(e, axis=-1, keepdims=True) + jnp.exp(sink - m)
            p = (e / denom).astype(BF16)
            o = jnp.dot(p, v_var[j][hh], preferred_element_type=F32)
            out[2 * j] = out[2 * j] + o[:blk]
            out[2 * j + 1] = out[2 * j + 1] + o[blk:]
    for c in range(n_slab):
        o_ref[:, c * LANES:(c + 1) * LANES] = out[c].astype(o_ref.dtype)


def _attn_call(proj, sink, rope_tabs, cols, n_batch, seq, n_ctx, att_width, local):
    blk = ATT_BLOCK
    nb = seq // blk
    pair_w = 2 * ATT_HEAD_DIM
    q_w = pair_w * ATT_GROUP
    n_pair = att_width // q_w
    k0 = cols["k_att"] // pair_w
    v0 = cols["v_att"] // pair_w
    q0 = cols["q_att"] // q_w
    ctx_row0 = (n_batch * seq) // n_ctx
    kx_spec = pl.BlockSpec((n_ctx, pair_w), lambda b, p, n: (ctx_row0 + b, k0 + p))
    vx_spec = pl.BlockSpec((n_ctx, pair_w), lambda b, p, n: (ctx_row0 + b, v0 + p))
    sink_spec = pl.BlockSpec(memory_space=pltpu.SMEM)
    scale = ATT_HEAD_DIM ** -0.5
    if local:
        cos_t, sin_t = rope_tabs

        def prev(n):
            return jnp.maximum(n - 1, 0)

        def nxt(n):
            return jnp.minimum(n + 1, nb - 1)

        def kv(c0, f):
            return pl.BlockSpec((blk, pair_w), lambda b, p, n: (b * nb + f(n), c0 + p))

        def tab(f):
            return pl.BlockSpec((blk, LANES), lambda b, p, n: (f(n), 0))

        ident = lambda n: n
        in_specs = [sink_spec,
                    pl.BlockSpec((blk, q_w), lambda b, p, n: (b * nb + n, q0 + p)),
                    kv(k0, prev), kv(k0, ident), kv(k0, nxt),
                    kv(v0, prev), kv(v0, ident), kv(v0, nxt),
                    kx_spec, vx_spec,
                    tab(prev), tab(ident), tab(nxt), tab(prev), tab(ident), tab(nxt)]
        args = [sink, proj] + [proj] * 6 + [proj, proj] + [cos_t] * 3 + [sin_t] * 3
        grid = (n_batch, n_pair, nb)
        rows = n_batch * seq
        out_spec = pl.BlockSpec((blk, q_w), lambda b, p, n: (b * nb + n, p))
    else:
        nqb = n_ctx // blk
        qrow0 = (n_batch * seq) // blk
        in_specs = [sink_spec,
                    pl.BlockSpec((blk, q_w), lambda b, p, n: (qrow0 + b * nqb + n, q0 + p)),
                    kx_spec, vx_spec]
        args = [sink, proj, proj, proj]
        grid = (n_batch, n_pair, nqb)
        rows = n_batch * n_ctx
        out_spec = pl.BlockSpec((blk, q_w), lambda b, p, n: (b * nqb + n, p))
    return pl.pallas_call(
        functools.partial(_attn_body, local=local, seq=seq, scale=scale),
        grid=grid, in_specs=in_specs, out_specs=out_spec,
        out_shape=jax.ShapeDtypeStruct((rows, att_width), BF16),
        compiler_params=_params(("arbitrary",) * 3),
        name="window_attn" if local else "ctx_attn",
    )(*args)


def _dot_tn(a, b):
    return lax.dot_general(a, b, (((0,), (0,)), ((), ())), preferred_element_type=F32)


def _ret_body(*refs, latent, n_chunks, n_ctx, k_scale):
    if latent:
        (lg_ref, q_ref, k_ref, v_ref, g_ref, kx_ref, vx_ref, cos_ref, sin_ref,
         o_ref, qs_ref, ks_ref, cb_ref, st_ref, sf_ref) = refs
    else:
        lg_ref, q_ref, k_ref, v_ref, g_ref, o_ref, qs_ref, ks_ref, cb_ref, st_ref, sf_ref = refs
    head = pl.program_id(1)
    lg_f = lg_ref[0, head]
    lg_b = lg_ref[1, head]
    c = RET_CHUNK
    hd = q_ref.shape[1]
    hh = hd // 2
    idx = lax.broadcasted_iota(jnp.int32, (c, 1), 0).astype(F32)
    qdec_f = jnp.exp(lg_f * (idx + 1.0))
    kdec_f = jnp.exp(lg_f * (c - 1.0 - idx))
    qdec_b = jnp.exp(lg_b * (c - idx))
    kdec_b = jnp.exp(lg_b * idx)
    one = jnp.ones((1, 1), F32)
    cdec_f = jnp.exp(lg_f * c * one)
    cdec_b = jnp.exp(lg_b * c * one)
    rel = (lax.broadcasted_iota(jnp.int32, (c, c), 0) - lax.broadcasted_iota(jnp.int32, (c, c), 1)).astype(F32)
    dmat = (jnp.where(rel >= 0, jnp.exp(lg_f * jnp.maximum(rel, 0.0)), 0.0)
            + jnp.where(rel <= 0, jnp.exp(lg_b * jnp.maximum(-rel, 0.0)), 0.0))

    def rope(v, rows):
        v1 = v[:, :hh]
        v2 = v[:, hh:]
        cs = cos_ref[rows, :]
        sn = sin_ref[rows, :]
        return jnp.concatenate([v1 * cs - v2 * sn, v2 * cs + v1 * sn], axis=1)

    if latent:
        t = lax.broadcasted_iota(jnp.int32, (n_ctx, 1), 0).astype(F32)
        kx = kx_ref[...].astype(F32) * k_scale
        sf_ref[...] = _dot_tn((kx * jnp.exp(lg_f * (n_ctx - 1.0 - t))).astype(BF16), vx_ref[...])
        st_ref[...] = _dot_tn((kx * jnp.exp(lg_b * t)).astype(BF16), vx_ref[...])
    else:
        sf_ref[...] = jnp.zeros((hd, hd), F32)
        st_ref[...] = jnp.zeros((hd, hd), F32)


    def bwd(step, carry):
        rows = pl.ds(pl.multiple_of((n_chunks - 1 - step) * c, c), c)
        q = q_ref[rows, :].astype(F32)
        k = k_ref[rows, :].astype(F32)
        if latent:
            q = rope(q, rows)
            k = rope(k, rows)
        k = k * k_scale
        qs_ref[rows, :] = q.astype(BF16)
        ks_ref[rows, :] = k.astype(BF16)
        state = st_ref[...]
        cb_ref[rows, :] = jnp.dot((q * qdec_b).astype(BF16), state.astype(BF16), preferred_element_type=F32)
        st_ref[...] = cdec_b * state + _dot_tn((k * kdec_b).astype(BF16), v_ref[rows, :])
        return carry

    lax.fori_loop(0, n_chunks, bwd, 0)

    st_ref[...] = sf_ref[...]

    def fwd(step, carry):
        rows = pl.ds(pl.multiple_of(step * c, c), c)
        qb = qs_ref[rows, :]
        kb = ks_ref[rows, :]
        v = v_ref[rows, :]
        scores = lax.dot_general(qb, kb, (((1,), (1,)), ((), ())), preferred_element_type=F32) * dmat
        o = jnp.dot(scores.astype(BF16), v, preferred_element_type=F32)
        state = st_ref[...]
        o = o + jnp.dot((qb.astype(F32) * qdec_f).astype(BF16), state.astype(BF16), preferred_element_type=F32)
        o = o + cb_ref[rows, :]
        st_ref[...] = cdec_f * state + _dot_tn((kb.astype(F32) * kdec_f).astype(BF16), v)
        mu = jnp.mean(o, axis=-1, keepdims=True)
        dev = o - mu
        var = jnp.mean(dev * dev, axis=-1, keepdims=True)
        y = dev * lax.rsqrt(var + GN_EPS)
        o_ref[rows, :] = (_silu(g_ref[rows, :].astype(F32)) * y).astype(o_ref.dtype)
        return carry

    lax.fori_loop(0, n_chunks, fwd, 0)


def _ret_call(proj, log_gamma, rope_tabs, cols, n_batch, seq, n_ctx, ret_width, latent):
    hd = ret_width // RET_HEADS
    t_len = seq if latent else n_ctx
    assert t_len % RET_CHUNK == 0
    k0, v0, q0, g0 = (cols[name] // hd for name in ("k_ret", "v_ret", "q_ret", "g_ret"))
    ctx_row0 = (n_batch * seq) // n_ctx
    row0 = 0 if latent else ctx_row0

    def seq_spec(c0):
        return pl.BlockSpec((t_len, hd), lambda b, h: (row0 + b, c0 + h))

    in_specs = [pl.BlockSpec(memory_space=pltpu.SMEM),
                seq_spec(q0), seq_spec(k0), seq_spec(v0), seq_spec(g0)]
    args = [log_gamma, proj, proj, proj, proj]
    if latent:
        cos_t, sin_t = rope_tabs
        in_specs += [pl.BlockSpec((n_ctx, hd), lambda b, h: (ctx_row0 + b, k0 + h)),
                     pl.BlockSpec((n_ctx, hd), lambda b, h: (ctx_row0 + b, v0 + h)),
                     pl.BlockSpec((t_len, hd // 2), lambda b, h: (0, 0)),
                     pl.BlockSpec((t_len, hd // 2), lambda b, h: (0, 0))]
        args += [proj, proj, cos_t, sin_t]
    return pl.pallas_call(
        functools.partial(_ret_body, latent=latent, n_chunks=t_len // RET_CHUNK, n_ctx=n_ctx,
                          k_scale=hd ** -0.5),
        grid=(n_batch, RET_HEADS), in_specs=in_specs,
        out_specs=pl.BlockSpec((t_len, hd), lambda b, h: (b, h)),
        out_shape=jax.ShapeDtypeStruct((n_batch * t_len, ret_width), BF16),
        scratch_shapes=[pltpu.VMEM((t_len, hd), BF16), pltpu.VMEM((t_len, hd), BF16),
                        pltpu.VMEM((t_len, hd), F32), pltpu.VMEM((hd, hd), F32), pltpu.VMEM((hd, hd), F32)],
        compiler_params=_params(("arbitrary", "arbitrary")),
        name="retention" if latent else "ctx_retention",
    )(*args)


def _merge_gate_body(ya_ref, yr_ref, ga_ref, gr_ref, wa_ref, wr_ref, ba_ref, br_ref, z_ref, wab_ref, wrb_ref):
    @pl.when(pl.program_id(1) == 0)
    def _():
        wab_ref[...] = wa_ref[...].astype(BF16)
        wrb_ref[...] = wr_ref[...].astype(BF16)
    a = jnp.dot(ya_ref[...], wab_ref[...], preferred_element_type=F32)
    r = jnp.dot(yr_ref[...], wrb_ref[...], preferred_element_type=F32)
    ga = jax.nn.sigmoid(ga_ref[...].astype(F32) + ba_ref[...])
    gr = jax.nn.sigmoid(gr_ref[...].astype(F32) + br_ref[...])
    z_ref[...] = (ga * a + gr * r).astype(z_ref.dtype)


def _merge_gate_call(ya, yr, proj, w_att, w_ret, gate_b, layer, rows, cols):
    d = w_att.shape[2]
    wa_rows, wr_rows = w_att.shape[1], w_ret.shape[1]
    tn = 512
    tm = _tile(rows, 1024, 16)
    ga0 = cols["gate_att"] // tn
    gr0 = cols["gate_ret"] // tn
    gb = gate_b.reshape(gate_b.shape[0], 1, 2 * d)
    return pl.pallas_call(
        _merge_gate_body,
        grid=(d // tn, rows // tm),
        in_specs=[
            pl.BlockSpec((tm, wa_rows), lambda j, i: (i, 0)),
            pl.BlockSpec((tm, wr_rows), lambda j, i: (i, 0)),
            pl.BlockSpec((tm, tn), lambda j, i: (i, ga0 + j)),
            pl.BlockSpec((tm, tn), lambda j, i: (i, gr0 + j)),
            pl.BlockSpec((None, wa_rows, tn), lambda j, i: (layer, 0, j)),
            pl.BlockSpec((None, wr_rows, tn), lambda j, i: (layer, 0, j)),
            pl.BlockSpec((None, 1, tn), lambda j, i: (layer, 0, j)),
            pl.BlockSpec((None, 1, tn), lambda j, i: (layer, 0, d // tn + j)),
        ],
        out_specs=pl.BlockSpec((tm, tn), lambda j, i: (i, j)),
        out_shape=jax.ShapeDtypeStruct((rows, d), BF16),
        scratch_shapes=[pltpu.VMEM((wa_rows, tn), BF16), pltpu.VMEM((wr_rows, tn), BF16)],
        compiler_params=_params(("arbitrary", "arbitrary")),
        name="merge_gate",
    )(ya, yr, proj, proj, w_att, w_ret, gb, gb)


def _resid_matmul_body(z_ref, w_ref, x_ref, g_ref, o_ref, wb_ref):
    @pl.when(pl.program_id(1) == 0)
    def _():
        wb_ref[...] = w_ref[...].astype(BF16)
    y = jnp.dot(z_ref[...], wb_ref[...], preferred_element_type=F32)
    o_ref[...] = x_ref[...] + g_ref[...] * y


def _resid_matmul_call(z, w, layer, xa, modl, gate_idx, rows, seq, n_batch):
    k, d = w.shape[1], w.shape[2]
    tn = 512
    tm = 512
    assert rows % tm == 0 and seq % tm == 0
    tpb = seq // tm

    def gate_map(j, i):
        return (jnp.minimum(i // tpb, n_batch) * 6 + gate_idx, 0, j)

    return pl.pallas_call(
        _resid_matmul_body,
        grid=(d // tn, rows // tm),
        in_specs=[
            pl.BlockSpec((tm, k), lambda j, i: (i, 0)),
            pl.BlockSpec((None, k, tn), lambda j, i: (layer, 0, j)),
            pl.BlockSpec((tm, tn), lambda j, i: (i, j)),
            pl.BlockSpec((None, 1, tn), gate_map),
        ],
        out_specs=pl.BlockSpec((tm, tn), lambda j, i: (i, j)),
        out_shape=jax.ShapeDtypeStruct((rows, d), F32),
        scratch_shapes=[pltpu.VMEM((k, tn), BF16)],
        compiler_params=_params(("arbitrary", "arbitrary")),
        name="resid_matmul",
    )(z, w, xa, modl)


EXPERT_TILE = 256
COMBINE_TILE = 128


def _routing(top_idx, top_w, n_exp, tm, n_tiles):
    n_assign = top_idx.size
    e_flat = top_idx.reshape(n_assign)
    onehot = (e_flat[:, None] == jnp.arange(n_exp, dtype=jnp.int32)[None, :]).astype(jnp.int32)
    running = jnp.cumsum(onehot, axis=0)
    counts = running[-1]
    rank = jnp.take_along_axis(running, e_flat[:, None], axis=1)[:, 0] - 1
    group = (counts + tm - 1) // tm * tm
    group_end = jnp.cumsum(group)
    slot = (group_end - group)[e_flat] + rank
    n_slots = n_tiles * tm
    token = jnp.arange(n_assign, dtype=jnp.int32) // TOP_K
    src = jnp.zeros((n_slots,), jnp.int32).at[slot].set(token)
    w_sorted = jnp.zeros((n_slots,), F32).at[slot].set(top_w.reshape(n_assign))
    n_used = (group_end[-1] // tm).astype(jnp.int32).reshape(1)
    tile_expert = jnp.searchsorted(group_end, jnp.arange(n_tiles, dtype=jnp.int32) * tm, side="right")
    tile_expert = jnp.minimum(tile_expert, n_exp - 1).astype(jnp.int32)
    return tile_expert, n_used, src, w_sorted.reshape(n_slots, 1), slot.astype(jnp.int32)


def _expert_changed(te_ref, t):
    return jnp.logical_or(t == 0, te_ref[t] != te_ref[jnp.maximum(t - 1, 0)])


def _expert_up_body(te_ref, nu_ref, src_ref, h_hbm, wgu_ref, bgu_ref, sel_ref, a_ref, xbuf, wb_ref, sem, *, tm):
    t = pl.program_id(0)
    n_used = nu_ref[0]
    slot = lax.rem(t, 2)

    def row_copy(tok, s, r):
        return pltpu.make_async_copy(h_hbm.at[pl.ds(tok, 1), :], xbuf.at[s, pl.ds(r, 1), :], sem.at[s])

    def gather(tile, s):
        base = tile * tm

        def body(r, carry):
            row_copy(src_ref[base + r], s, r).start()
            return carry
        lax.fori_loop(0, tm, body, 0, unroll=8)

    @pl.when(t == 0)
    def _():
        gather(0, 0)

    @pl.when(t + 1 < n_used)
    def _():
        gather(t + 1, 1 - slot)

    @pl.when(t < n_used)
    def _():
        @pl.when(_expert_changed(te_ref, t))
        def _():
            wb_ref[...] = wgu_ref[...].astype(BF16)

        pltpu.make_async_copy(h_hbm.at[pl.ds(0, tm), :], xbuf.at[slot], sem.at[slot]).wait()
        x = xbuf[slot].astype(BF16)
        gu = jnp.dot(x, wb_ref[...], preferred_element_type=F32) + bgu_ref[...]
        n2 = gu.shape[1]
        even = lax.broadcasted_iota(jnp.int32, gu.shape, 1) % 2 == 0
        gate = jnp.minimum(gu, SWIGLU_LIMIT)
        up = jnp.clip(pltpu.roll(gu, n2 - 1, 1), -SWIGLU_LIMIT, SWIGLU_LIMIT)
        act = jnp.where(even, (up + 1.0) * (gate * jax.nn.sigmoid(SWIGLU_ALPHA * gate)), 0.0)
        a_ref[...] = jnp.dot(act.astype(BF16), sel_ref[...], preferred_element_type=F32).astype(a_ref.dtype)

    @pl.when(t >= n_used)
    def _():
        a_ref[...] = jnp.zeros(a_ref.shape, a_ref.dtype)


def _expert_up_call(h2, w_gu, b_gu, layer, tile_expert, n_used, src, n_tiles):
    n_exp, d, ff2 = w_gu.shape[1:]
    ff = ff2 // 2
    tm = EXPERT_TILE
    sel = (jnp.arange(ff2, dtype=jnp.int32)[:, None] == 2 * jnp.arange(ff, dtype=jnp.int32)[None, :]).astype(BF16)
    grid_spec = pltpu.PrefetchScalarGridSpec(
        num_scalar_prefetch=3,
        grid=(n_tiles,),
        in_specs=[
            pl.BlockSpec(memory_space=pl.ANY),
            pl.BlockSpec((None, None, d, ff2), lambda t, te, nu, sr: (layer, te[t], 0, 0)),
            pl.BlockSpec((None, None, 1, ff2), lambda t, te, nu, sr: (layer, te[t], 0, 0)),
            pl.BlockSpec((ff2, ff), lambda t, te, nu, sr: (0, 0)),
        ],
        out_specs=pl.BlockSpec((tm, ff), lambda t, te, nu, sr: (t, 0)),
        scratch_shapes=[pltpu.VMEM((2, tm, d), F32), pltpu.VMEM((d, ff2), BF16), pltpu.SemaphoreType.DMA((2,))],
    )
    return pl.pallas_call(
        functools.partial(_expert_up_body, tm=tm),
        grid_spec=grid_spec,
        out_shape=jax.ShapeDtypeStruct((n_tiles * tm, ff), BF16),
        compiler_params=_params(("arbitrary",)),
        name="expert_up",
    )(tile_expert, n_used, src, h2, w_gu, b_gu.reshape(b_gu.shape[0], n_exp, 1, ff2), sel)


def _expert_down_body(te_ref, nu_ref, a_ref, wd_ref, bd_ref, ws_ref, y_ref, wb_ref):
    t = pl.program_id(0)

    @pl.when(t < nu_ref[0])
    def _():
        @pl.when(_expert_changed(te_ref, t))
        def _():
            wb_ref[...] = wd_ref[...].astype(BF16)
        y = jnp.dot(a_ref[...], wb_ref[...], preferred_element_type=F32) + bd_ref[...]
        y_ref[...] = ws_ref[...] * y

    @pl.when(t >= nu_ref[0])
    def _():
        y_ref[...] = jnp.zeros(y_ref.shape, y_ref.dtype)


def _expert_down_call(act, w_down, b_down, layer, tile_expert, n_used, w_sorted, n_tiles):
    n_exp, ff, d = w_down.shape[1:]
    tm = EXPERT_TILE
    grid_spec = pltpu.PrefetchScalarGridSpec(
        num_scalar_prefetch=2,
        grid=(n_tiles,),
        in_specs=[
            pl.BlockSpec((tm, ff), lambda t, te, nu: (t, 0)),
            pl.BlockSpec((None, None, ff, d), lambda t, te, nu: (layer, te[t], 0, 0)),
            pl.BlockSpec((None, None, 1, d), lambda t, te, nu: (layer, te[t], 0, 0)),
            pl.BlockSpec((tm, 1), lambda t, te, nu: (t, 0)),
        ],
        out_specs=pl.BlockSpec((tm, d), lambda t, te, nu: (t, 0)),
        scratch_shapes=[pltpu.VMEM((ff, d), BF16)],
    )
    return pl.pallas_call(
        _expert_down_body,
        grid_spec=grid_spec,
        out_shape=jax.ShapeDtypeStruct((n_tiles * tm, d), F32),
        compiler_params=_params(("arbitrary",)),
        name="expert_down",
    )(tile_expert, n_used, act, w_down, b_down.reshape(b_down.shape[0], n_exp, 1, d), w_sorted)


def _combine_body(pos_ref, y_hbm, x_ref, g_ref, o_ref, ybuf, sem, *, tc):
    t = pl.program_id(0)
    slot = lax.rem(t, 2)

    def gather(tile, s):
        base = tile * (tc * TOP_K)

        def body(r, carry):
            for k in range(TOP_K):
                row = pos_ref[base + r * TOP_K + k]
                pltpu.make_async_copy(y_hbm.at[pl.ds(row, 1), :], ybuf.at[s, k, pl.ds(r, 1), :], sem.at[s]).start()
            return carry
        lax.fori_loop(0, tc, body, 0, unroll=2)

    @pl.when(t == 0)
    def _():
        gather(0, 0)

    @pl.when(t + 1 < pl.num_programs(0))
    def _():
        gather(t + 1, 1 - slot)

    for k in range(TOP_K):
        pltpu.make_async_copy(y_hbm.at[pl.ds(0, tc), :], ybuf.at[slot, k], sem.at[slot]).wait()
    total = ybuf[slot, 0]
    for k in range(1, TOP_K):
        total = total + ybuf[slot, k]
    o_ref[...] = x_ref[...] + g_ref[...] * total


def _combine_call(y_sorted, slot, xa, modl, gate_idx, rows, seq, n_batch):
    d = xa.shape[1]
    tc = COMBINE_TILE
    assert rows % tc == 0 and seq % tc == 0
    tpb = seq // tc
    grid_spec = pltpu.PrefetchScalarGridSpec(
        num_scalar_prefetch=1,
        grid=(rows // tc,),
        in_specs=[
            pl.BlockSpec(memory_space=pl.ANY),
            pl.BlockSpec((tc, d), lambda t, pos: (t, 0)),
            pl.BlockSpec((None, 1, d), lambda t, pos: (jnp.minimum(t // tpb, n_batch) * 6 + gate_idx, 0, 0)),
        ],
        out_specs=pl.BlockSpec((tc, d), lambda t, pos: (t, 0)),
        scratch_shapes=[pltpu.VMEM((2, TOP_K, tc, d), F32), pltpu.SemaphoreType.DMA((2,))],
    )
    return pl.pallas_call(
        functools.partial(_combine_body, tc=tc),
        grid_spec=grid_spec,
        out_shape=jax.ShapeDtypeStruct((rows, d), F32),
        compiler_params=_params(("arbitrary",)),
        name="expert_combine",
    )(slot, y_sorted, xa, modl)


def _axial_angles(seq, head_dim):
    rows = seq // GRID_W
    row = jnp.repeat(jnp.arange(rows, dtype=F32), GRID_W)
    col = jnp.tile(jnp.arange(GRID_W, dtype=F32), rows)
    n_freq = head_dim // 4
    inv_freq = ROPE_BASE ** (-jnp.arange(n_freq, dtype=F32) / n_freq)
    return jnp.concatenate([row[:, None] * inv_freq, col[:, None] * inv_freq], axis=-1)


def _rope_tables(seq, ret_head_dim):
    ang_a = _axial_angles(seq, ATT_HEAD_DIM)
    reps = LANES // ATT_HEAD_DIM
    cos_a = jnp.tile(jnp.concatenate([jnp.cos(ang_a), jnp.cos(ang_a)], axis=-1), (1, reps))
    sin_a = jnp.tile(jnp.concatenate([-jnp.sin(ang_a), jnp.sin(ang_a)], axis=-1), (1, reps))
    ang_r = _axial_angles(seq, ret_head_dim)
    return (cos_a, sin_a), (jnp.cos(ang_r), jnp.sin(ang_r))


def kernel(x, c, ctx, c_ctx, ada_w, ada_b, norm_mix_g, norm_ffn_g, w_in, att_sink, ret_decay_fwd,
           ret_decay_bwd, w_branch_att, w_branch_ret, branch_gate_b, w_out, router_w, router_b,
           exp_w_gu, exp_b_gu, exp_w_down, exp_b_down, final_norm_g):
    n_batch, seq, d = x.shape
    n_ctx = ctx.shape[1]
    depth = ada_w.shape[0]
    n_exp, ff = exp_w_down.shape[1], exp_w_down.shape[2]
    att_width = w_branch_att.shape[1]
    ret_width = w_branch_ret.shape[1]
    kv_width = att_width // ATT_GROUP
    sizes = (("k_att", kv_width), ("v_att", kv_width), ("k_ret", ret_width), ("v_ret", ret_width),
             ("q_att", att_width), ("q_ret", ret_width), ("g_ret", ret_width),
             ("gate_att", d), ("gate_ret", d))
    cols, off = {}, 0
    for name, width in sizes:
        cols[name] = off
        off += width
    in_cols = off
    assert in_cols == w_in.shape[2]

    lat_rows = n_batch * seq
    all_rows = lat_rows + n_batch * n_ctx
    xa = jnp.concatenate([x.reshape(lat_rows, d), ctx.reshape(n_batch * n_ctx, d)], axis=0)

    cond = jnp.zeros((16, d), F32).at[:n_batch].set(c).at[n_batch].set(c_ctx)
    mod = _mod_call(cond, ada_w, ada_b)
    rope_att, rope_ret = _rope_tables(seq, ret_width // RET_HEADS)

    for layer in range(depth):
        last = layer == depth - 1
        rows = lat_rows if last else all_rows
        modl = mod[layer, :n_batch + 1].reshape((n_batch + 1) * 6, 1, d)
        log_gamma = jnp.stack([jax.nn.log_sigmoid(ret_decay_fwd[layer].astype(F32)),
                               jax.nn.log_sigmoid(ret_decay_bwd[layer].astype(F32))])

        h = _norm_call(xa, norm_mix_g[layer], modl, all_rows, seq, n_batch, 0, 1)
        proj = _inproj_call(h, w_in, layer, all_rows, 0, in_cols)

        ya = _attn_call(proj, att_sink[layer], rope_att, cols, n_batch, seq, n_ctx, att_width, True)
        yr = _ret_call(proj, log_gamma, rope_ret, cols, n_batch, seq, n_ctx, ret_width, True)
        if not last:
            ya_c = _attn_call(proj, att_sink[layer], None, cols, n_batch, seq, n_ctx, att_width, False)
            yr_c = _ret_call(proj, log_gamma, None, cols, n_batch, seq, n_ctx, ret_width, False)
            ya = jnp.concatenate([ya, ya_c], axis=0)
            yr = jnp.concatenate([yr, yr_c], axis=0)

        z = _merge_gate_call(ya, yr, proj, w_branch_att, w_branch_ret, branch_gate_b, layer, rows, cols)
        xa = _resid_matmul_call(z, w_out, layer, xa, modl, 2, rows, seq, n_batch)

        h2, top_idx, top_w = _norm_call(xa, norm_ffn_g[layer], modl, rows, seq, n_batch, 3, 4,
                                        router=(router_w[layer], router_b[layer]))
        n_tiles = -(-rows * TOP_K // EXPERT_TILE) + n_exp
        tile_expert, n_used, src, w_sorted, slot = _routing(
            top_idx[:, :TOP_K], top_w[:, :TOP_K], n_exp, EXPERT_TILE, n_tiles)
        act = _expert_up_call(h2, exp_w_gu, exp_b_gu, layer, tile_expert, n_used, src, n_tiles)
        y_sorted = _expert_down_call(act, exp_w_down, exp_b_down, layer, tile_expert, n_used, w_sorted, n_tiles)
        xa = _combine_call(y_sorted, slot, xa, modl, 5, rows, seq, n_batch)

    out = _final_norm_call(xa, final_norm_g, lat_rows)
    return out.reshape(n_batch, seq, d)
```

```python
import functools

import jax
import jax.numpy as jnp
from jax import lax
from jax.experimental import pallas as pl
from jax.experimental.pallas import tpu as pltpu

F32 = jnp.float32
BF16 = jnp.bfloat16

GRID_W = 64
ATT_HEAD_DIM = 64
ATT_GROUP = 4
ATT_WINDOW = 128
ATT_BLOCK = 128
RET_HEADS = 8
RET_CHUNK = 128
TOP_K = 4
SWIGLU_ALPHA = 1.702
SWIGLU_LIMIT = 7.0
ROPE_BASE = 10000.0
NORM_EPS = 1e-6
GN_EPS = 1e-5
NEG_INF = -1e30

LANES = 128
VMEM_LIMIT = 56 * 1024 * 1024


def _params(sem, vmem=VMEM_LIMIT):
    return pltpu.CompilerParams(dimension_semantics=sem, vmem_limit_bytes=vmem)


def _tile(n, cap, mult):
    for t in range(min(cap, n), 0, -1):
        if n % t == 0 and t % mult == 0:
            return t
    raise ValueError(f"no tile for {n} (cap {cap}, multiple of {mult})")


def _silu(v):
    return v * jax.nn.sigmoid(v)


def _mod_body(c_ref, w_ref, b_ref, o_ref):
    s = _silu(c_ref[...]).astype(BF16)
    o_ref[...] = jnp.dot(s, w_ref[...].astype(BF16), preferred_element_type=F32) + b_ref[...]


def _mod_call(cond, ada_w, ada_b):
    depth, d, n = ada_w.shape
    rows = cond.shape[0]
    tn = _tile(n, 512, LANES)
    return pl.pallas_call(
        _mod_body,
        grid=(depth, n // tn),
        in_specs=[
            pl.BlockSpec((rows, d), lambda l, j: (0, 0)),
            pl.BlockSpec((None, d, tn), lambda l, j: (l, 0, j)),
            pl.BlockSpec((None, 1, tn), lambda l, j: (l, 0, j)),
        ],
        out_specs=pl.BlockSpec((None, rows, tn), lambda l, j: (l, 0, j)),
        out_shape=jax.ShapeDtypeStruct((depth, rows, n), F32),
        compiler_params=_params(("arbitrary", "arbitrary")),
        name="adaln_mod",
    )(cond, ada_w, ada_b.reshape(depth, 1, n))


def _rms(x, g):
    ms = jnp.mean(x * x, axis=-1, keepdims=True)
    return x * lax.rsqrt(ms + NORM_EPS) * g


def _norm_mod_body(x_ref, g_ref, sh_ref, sc_ref, h_ref):
    h = _rms(x_ref[...], g_ref[...]) * (1.0 + sc_ref[...]) + sh_ref[...]
    h_ref[...] = h.astype(h_ref.dtype)


def _pack_halves(v):
    w = v.shape[1] // 2
    bits = lax.bitcast_convert_type(v.astype(BF16).astype(F32), jnp.int32)
    return lax.shift_right_logical(bits[:, :w], 16) | (bits[:, w:] & jnp.int32(-65536))


def _unpack_halves(p):
    lo = lax.bitcast_convert_type(lax.shift_left(p, 16), F32)
    hi = lax.bitcast_convert_type(p & jnp.int32(-65536), F32)
    return lo.astype(BF16), hi.astype(BF16)


def _split_bf16(v):
    hi = v.astype(BF16)
    lo = (v - hi.astype(F32)).astype(BF16)
    return hi, lo


def _norm_router_body(x_ref, g_ref, sh_ref, sc_ref, rw_ref, rb_ref, h_ref, idx_ref, wt_ref):
    h = _rms(x_ref[...], g_ref[...]) * (1.0 + sc_ref[...]) + sh_ref[...]
    h_ref[...] = _pack_halves(h)
    h_hi, h_lo = _split_bf16(h)
    w_hi, w_lo = _split_bf16(rw_ref[...])
    logits = (jnp.dot(h_hi, w_hi, preferred_element_type=F32)
              + jnp.dot(h_hi, w_lo, preferred_element_type=F32)
              + jnp.dot(h_lo, w_hi, preferred_element_type=F32)) + rb_ref[...]
    n_exp = logits.shape[-1]
    lane = lax.broadcasted_iota(jnp.int32, logits.shape, 1).astype(F32)
    work = logits
    out_lane = lax.broadcasted_iota(jnp.int32, idx_ref.shape, 1)
    idx = jnp.zeros(idx_ref.shape, F32)
    ex = jnp.zeros(wt_ref.shape, F32)
    top = None
    total = None
    for k in range(TOP_K):
        m = jnp.max(work, axis=-1, keepdims=True)
        first = jnp.min(jnp.where(work == m, lane, float(n_exp)), axis=-1, keepdims=True)
        work = jnp.where(lane == first, -jnp.inf, work)
        if k == 0:
            top = m
        e = jnp.exp(m - top)
        total = e if k == 0 else total + e
        idx = jnp.where(out_lane == k, first, idx)
        ex = jnp.where(out_lane == k, e, ex)
    idx_ref[...] = idx.astype(jnp.int32)
    wt_ref[...] = ex / total


def _norm_plain_body(x_ref, g_ref, o_ref):
    o_ref[...] = _rms(x_ref[...], g_ref[...]).astype(o_ref.dtype)


def _seg_map(tiles_per_batch, n_batch, which):
    def index(i):
        return (jnp.minimum(i // tiles_per_batch, n_batch) * 6 + which, 0, 0)
    return index


def _norm_call(xa, g, modl, rows, seq, n_batch, shift_idx, scale_idx, router=None):
    d = xa.shape[1]
    tr = 256
    assert rows % tr == 0 and seq % tr == 0
    tpb = seq // tr
    row_spec = pl.BlockSpec((tr, d), lambda i: (i, 0))
    in_specs = [
        row_spec,
        pl.BlockSpec((1, d), lambda i: (0, 0)),
        pl.BlockSpec((None, 1, d), _seg_map(tpb, n_batch, shift_idx)),
        pl.BlockSpec((None, 1, d), _seg_map(tpb, n_batch, scale_idx)),
    ]
    args = [xa, g.reshape(1, d), modl, modl]
    if router is None:
        return pl.pallas_call(
            _norm_mod_body, grid=(rows // tr,), in_specs=in_specs, out_specs=row_spec,
            out_shape=jax.ShapeDtypeStruct((rows, d), BF16),
            compiler_params=_params(("arbitrary",)), name="norm_mod",
        )(*args)
    rw, rb = router
    n_exp = rw.shape[1]
    in_specs += [pl.BlockSpec((d, n_exp), lambda i: (0, 0)), pl.BlockSpec((1, n_exp), lambda i: (0, 0))]
    args += [rw, rb.reshape(1, n_exp)]
    return pl.pallas_call(
        _norm_router_body, grid=(rows // tr,), in_specs=in_specs,
        out_specs=[pl.BlockSpec((tr, d // 2), lambda i: (i, 0)),
                   pl.BlockSpec((tr, LANES), lambda i: (i, 0)), pl.BlockSpec((tr, LANES), lambda i: (i, 0))],
        out_shape=[jax.ShapeDtypeStruct((rows, d // 2), jnp.int32), jax.ShapeDtypeStruct((rows, LANES), jnp.int32),
                   jax.ShapeDtypeStruct((rows, LANES), F32)],
        compiler_params=_params(("arbitrary",)), name="norm_router",
    )(*args)


def _final_norm_call(xa, g, rows):
    d = xa.shape[1]
    tr = 256
    row_spec = pl.BlockSpec((tr, d), lambda i: (i, 0))
    return pl.pallas_call(
        _norm_plain_body, grid=(rows // tr,),
        in_specs=[row_spec, pl.BlockSpec((1, d), lambda i: (0, 0))], out_specs=row_spec,
        out_shape=jax.ShapeDtypeStruct((rows, d), F32),
        compiler_params=_params(("arbitrary",)), name="final_norm",
    )(xa, g.reshape(1, d))


def _inproj_body(a_ref, w_ref, o_ref, wb_ref):
    @pl.when(pl.program_id(1) == 0)
    def _():
        wb_ref[...] = w_ref[...].astype(BF16)
    o_ref[...] = jnp.dot(a_ref[...], wb_ref[...], preferred_element_type=F32).astype(o_ref.dtype)


def _inproj_call(h, w_in, layer, rows, col_lo, col_hi):
    d = h.shape[1]
    tn = 512
    assert col_lo % tn == 0 and col_hi % tn == 0
    tm = _tile(rows, 1152, 16)
    j0 = col_lo // tn
    return pl.pallas_call(
        _inproj_body,
        grid=((col_hi - col_lo) // tn, rows // tm),
        in_specs=[
            pl.BlockSpec((tm, d), lambda j, i: (i, 0)),
            pl.BlockSpec((None, d, tn), lambda j, i: (layer, 0, j0 + j)),
        ],
        out_specs=pl.BlockSpec((tm, tn), lambda j, i: (i, j)),
        out_shape=jax.ShapeDtypeStruct((rows, col_hi - col_lo), BF16),
        scratch_shapes=[pltpu.VMEM((d, tn), BF16)],
        compiler_params=_params(("arbitrary", "arbitrary")),
        name="in_proj",
    )(h, w_in)


def _attn_body(*refs, local, seq, scale):
    if local:
        (sink_ref, q_ref, kp_ref, kc_ref, kn_ref, vp_ref, vc_ref, vn_ref, kx_ref, vx_ref,
         cp_ref, cc_ref, cn_ref, sp_ref, sc_ref, sn_ref, o_ref) = refs
    else:
        sink_ref, q_ref, kx_ref, vx_ref, o_ref = refs
    pair = pl.program_id(1)
    n = pl.program_id(2)
    blk = ATT_BLOCK
    half = ATT_HEAD_DIM // 2

    lane_q = lax.broadcasted_iota(jnp.int32, (blk, LANES), 1)
    first_half = (lane_q % ATT_HEAD_DIM) < half

    def rope(v, c_ref, s_ref):
        rot = jnp.where(first_half, pltpu.roll(v, LANES - half, 1), pltpu.roll(v, half, 1))
        return v * c_ref[...] + rot * s_ref[...]

    if local:
        keys = jnp.concatenate([
            rope(kp_ref[...].astype(F32), cp_ref, sp_ref),
            rope(kc_ref[...].astype(F32), cc_ref, sc_ref),
            rope(kn_ref[...].astype(F32), cn_ref, sn_ref),
            kx_ref[...].astype(F32)], axis=0)
        vals = jnp.concatenate([vp_ref[...], vc_ref[...], vn_ref[...], vx_ref[...]], axis=0).astype(F32)
    else:
        keys = kx_ref[...].astype(F32)
        vals = vx_ref[...].astype(F32)
    nk = keys.shape[0]

    lane_k = lax.broadcasted_iota(jnp.int32, (nk, LANES), 1)
    low = lane_k < ATT_HEAD_DIM

    def placed(v):
        swapped = pltpu.roll(v, ATT_HEAD_DIM, 1)
        return [[jnp.where(low, v, 0.0).astype(BF16), jnp.where(low, 0.0, swapped).astype(BF16)],
                [jnp.where(low, swapped, 0.0).astype(BF16), jnp.where(low, 0.0, v).astype(BF16)]]

    k_var = placed(keys)
    v_var = placed(vals)

    if local:
        row = lax.broadcasted_iota(jnp.int32, (2 * blk, nk), 0) % blk
        col = lax.broadcasted_iota(jnp.int32, (2 * blk, nk), 1)
        q_pos = n * blk + row
        k_pos = (n - 1) * blk + col
        valid = ((col >= 3 * blk)
                 | ((k_pos >= 0) & (k_pos < seq) & (jnp.abs(k_pos - q_pos) <= ATT_WINDOW)))
    top_rows = lax.broadcasted_iota(jnp.int32, (2 * blk, 1), 0) < blk

    n_slab = q_ref.shape[1] // LANES
    q_slabs = []
    for c in range(n_slab):
        qc = q_ref[:, c * LANES:(c + 1) * LANES].astype(F32)
        if local:
            qc = rope(qc, cc_ref, sc_ref)
        q_slabs.append((qc * scale).astype(BF16))

    out = [jnp.zeros((blk, LANES), F32) for _ in range(n_slab)]
    for j in range(2):
        lhs = jnp.concatenate([q_slabs[2 * j], q_slabs[2 * j + 1]], axis=0)
        for hh in range(2):
            s = lax.dot_general(lhs, k_var[j][hh], (((1,), (1,)), ((), ())),
                                preferred_element_type=F32)
            if local:
                s = jnp.where(valid, s, NEG_INF)
            head0 = pair * (2 * ATT_GROUP) + ATT_GROUP * j + hh
            sink = jnp.where(top_rows, sink_ref[head0], sink_ref[head0 + 2])
            m = jnp.maximum(jnp.max(s, axis=-1, keepdims=True), sink)
            e = jnp.exp(s - m)
            denom = jnp.sum(e, axis=-1, keepdims=True) + jnp.exp(sink - m)
            p = (e / denom).astype(BF16)
            o = jnp.dot(p, v_var[j][hh], preferred_element_type=F32)
            out[2 * j] = out[2 * j] + o[:blk]
            out[2 * j + 1] = out[2 * j + 1] + o[blk:]
    for c in range(n_slab):
        o_ref[:, c * LANES:(c + 1) * LANES] = out[c].astype(o_ref.dtype)


def _attn_call(proj, sink, rope_tabs, cols, n_batch, seq, n_ctx, att_width, local):
    blk = ATT_BLOCK
    nb = seq // blk
    pair_w = 2 * ATT_HEAD_DIM
    q_w = pair_w * ATT_GROUP
    n_pair = att_width // q_w
    k0 = cols["k_att"] // pair_w
    v0 = cols["v_att"] // pair_w
    q0 = cols["q_att"] // q_w
    ctx_row0 = (n_batch * seq) // n_ctx
    kx_spec = pl.BlockSpec((n_ctx, pair_w), lambda b, p, n: (ctx_row0 + b, k0 + p))
    vx_spec = pl.BlockSpec((n_ctx, pair_w), lambda b, p, n: (ctx_row0 + b, v0 + p))
    sink_spec = pl.BlockSpec(memory_space=pltpu.SMEM)
    scale = ATT_HEAD_DIM ** -0.5
    if local:
        cos_t, sin_t = rope_tabs

        def prev(n):
            return jnp.maximum(n - 1, 0)

        def nxt(n):
            return jnp.minimum(n + 1, nb - 1)

        def kv(c0, f):
            return pl.BlockSpec((blk, pair_w), lambda b, p, n: (b * nb + f(n), c0 + p))

        def tab(f):
            return pl.BlockSpec((blk, LANES), lambda b, p, n: (f(n), 0))

        ident = lambda n: n
        in_specs = [sink_spec,
                    pl.BlockSpec((blk, q_w), lambda b, p, n: (b * nb + n, q0 + p)),
                    kv(k0, prev), kv(k0, ident), kv(k0, nxt),
                    kv(v0, prev), kv(v0, ident), kv(v0, nxt),
                    kx_spec, vx_spec,
                    tab(prev), tab(ident), tab(nxt), tab(prev), tab(ident), tab(nxt)]
        args = [sink, proj] + [proj] * 6 + [proj, proj] + [cos_t] * 3 + [sin_t] * 3
        grid = (n_batch, n_pair, nb)
        rows = n_batch * seq
        out_spec = pl.BlockSpec((blk, q_w), lambda b, p, n: (b * nb + n, p))
    else:
        nqb = n_ctx // blk
        qrow0 = (n_batch * seq) // blk
        in_specs = [sink_spec,
                    pl.BlockSpec((blk, q_w), lambda b, p, n: (qrow0 + b * nqb + n, q0 + p)),
                    kx_spec, vx_spec]
        args = [sink, proj, proj, proj]
        grid = (n_batch, n_pair, nqb)
        rows = n_batch * n_ctx
        out_spec = pl.BlockSpec((blk, q_w), lambda b, p, n: (b * nqb + n, p))
    return pl.pallas_call(
        functools.partial(_attn_body, local=local, seq=seq, scale=scale),
        grid=grid, in_specs=in_specs, out_specs=out_spec,
        out_shape=jax.ShapeDtypeStruct((rows, att_width), BF16),
        compiler_params=_params(("arbitrary",) * 3),
        name="window_attn" if local else "ctx_attn",
    )(*args)


def _dot_tn(a, b):
    return lax.dot_general(a, b, (((0,), (0,)), ((), ())), preferred_element_type=F32)


def _ret_body(*refs, latent, n_chunks, n_ctx, k_scale):
    if latent:
        (lg_ref, q_ref, k_ref, v_ref, g_ref, kx_ref, vx_ref, cos_ref, sin_ref,
         o_ref, qs_ref, ks_ref, cb_ref, st_ref, sf_ref) = refs
    else:
        lg_ref, q_ref, k_ref, v_ref, g_ref, o_ref, qs_ref, ks_ref, cb_ref, st_ref, sf_ref = refs
    head = pl.program_id(1)
    lg_f = lg_ref[0, head]
    lg_b = lg_ref[1, head]
    c = RET_CHUNK
    hd = q_ref.shape[1]
    hh = hd // 2
    idx = lax.broadcasted_iota(jnp.int32, (c, 1), 0).astype(F32)
    qdec_f = jnp.exp(lg_f * (idx + 1.0))
    kdec_f = jnp.exp(lg_f * (c - 1.0 - idx))
    qdec_b = jnp.exp(lg_b * (c - idx))
    kdec_b = jnp.exp(lg_b * idx)
    one = jnp.ones((1, 1), F32)
    cdec_f = jnp.exp(lg_f * c * one)
    cdec_b = jnp.exp(lg_b * c * one)
    rel = (lax.broadcasted_iota(jnp.int32, (c, c), 0) - lax.broadcasted_iota(jnp.int32, (c, c), 1)).astype(F32)
    dmat = (jnp.where(rel >= 0, jnp.exp(lg_f * jnp.maximum(rel, 0.0)), 0.0)
            + jnp.where(rel <= 0, jnp.exp(lg_b * jnp.maximum(-rel, 0.0)), 0.0))

    def rope(v, rows):
        v1 = v[:, :hh]
        v2 = v[:, hh:]
        cs = cos_ref[rows, :]
        sn = sin_ref[rows, :]
        return jnp.concatenate([v1 * cs - v2 * sn, v2 * cs + v1 * sn], axis=1)

    if latent:
        t = lax.broadcasted_iota(jnp.int32, (n_ctx, 1), 0).astype(F32)
        kx = kx_ref[...].astype(F32) * k_scale
        sf_ref[...] = _dot_tn((kx * jnp.exp(lg_f * (n_ctx - 1.0 - t))).astype(BF16), vx_ref[...])
        st_ref[...] = _dot_tn((kx * jnp.exp(lg_b * t)).astype(BF16), vx_ref[...])
    else:
        sf_ref[...] = jnp.zeros((hd, hd), F32)
        st_ref[...] = jnp.zeros((hd, hd), F32)


    def bwd(step, carry):
        rows = pl.ds(pl.multiple_of((n_chunks - 1 - step) * c, c), c)
        q = q_ref[rows, :].astype(F32)
        k = k_ref[rows, :].astype(F32)
        if latent:
            q = rope(q, rows)
            k = rope(k, rows)
        k = k * k_scale
        qs_ref[rows, :] = q.astype(BF16)
        ks_ref[rows, :] = k.astype(BF16)
        state = st_ref[...]
        cb_ref[rows, :] = jnp.dot((q * qdec_b).astype(BF16), state.astype(BF16), preferred_element_type=F32)
        st_ref[...] = cdec_b * state + _dot_tn((k * kdec_b).astype(BF16), v_ref[rows, :])
        return carry

    lax.fori_loop(0, n_chunks, bwd, 0)

    st_ref[...] = sf_ref[...]

    def fwd(step, carry):
        rows = pl.ds(pl.multiple_of(step * c, c), c)
        qb = qs_ref[rows, :]
        kb = ks_ref[rows, :]
        v = v_ref[rows, :]
        scores = lax.dot_general(qb, kb, (((1,), (1,)), ((), ())), preferred_element_type=F32) * dmat
        o = jnp.dot(scores.astype(BF16), v, preferred_element_type=F32)
        state = st_ref[...]
        o = o + jnp.dot((qb.astype(F32) * qdec_f).astype(BF16), state.astype(BF16), preferred_element_type=F32)
        o = o + cb_ref[rows, :]
        st_ref[...] = cdec_f * state + _dot_tn((kb.astype(F32) * kdec_f).astype(BF16), v)
        mu = jnp.mean(o, axis=-1, keepdims=True)
        dev = o - mu
        var = jnp.mean(dev * dev, axis=-1, keepdims=True)
        y = dev * lax.rsqrt(var + GN_EPS)
        o_ref[rows, :] = (_silu(g_ref[rows, :].astype(F32)) * y).astype(o_ref.dtype)
        return carry

    lax.fori_loop(0, n_chunks, fwd, 0)


def _ret_call(proj, log_gamma, rope_tabs, cols, n_batch, seq, n_ctx, ret_width, latent):
    hd = ret_width // RET_HEADS
    t_len = seq if latent else n_ctx
    assert t_len % RET_CHUNK == 0
    k0, v0, q0, g0 = (cols[name] // hd for name in ("k_ret", "v_ret", "q_ret", "g_ret"))
    ctx_row0 = (n_batch * seq) // n_ctx
    row0 = 0 if latent else ctx_row0

    def seq_spec(c0):
        return pl.BlockSpec((t_len, hd), lambda b, h: (row0 + b, c0 + h))

    in_specs = [pl.BlockSpec(memory_space=pltpu.SMEM),
                seq_spec(q0), seq_spec(k0), seq_spec(v0), seq_spec(g0)]
    args = [log_gamma, proj, proj, proj, proj]
    if latent:
        cos_t, sin_t = rope_tabs
        in_specs += [pl.BlockSpec((n_ctx, hd), lambda b, h: (ctx_row0 + b, k0 + h)),
                     pl.BlockSpec((n_ctx, hd), lambda b, h: (ctx_row0 + b, v0 + h)),
                     pl.BlockSpec((t_len, hd // 2), lambda b, h: (0, 0)),
                     pl.BlockSpec((t_len, hd // 2), lambda b, h: (0, 0))]
        args += [proj, proj, cos_t, sin_t]
    return pl.pallas_call(
        functools.partial(_ret_body, latent=latent, n_chunks=t_len // RET_CHUNK, n_ctx=n_ctx,
                          k_scale=hd ** -0.5),
        grid=(n_batch, RET_HEADS), in_specs=in_specs,
        out_specs=pl.BlockSpec((t_len, hd), lambda b, h: (b, h)),
        out_shape=jax.ShapeDtypeStruct((n_batch * t_len, ret_width), BF16),
        scratch_shapes=[pltpu.VMEM((t_len, hd), BF16), pltpu.VMEM((t_len, hd), BF16),
                        pltpu.VMEM((t_len, hd), F32), pltpu.VMEM((hd, hd), F32), pltpu.VMEM((hd, hd), F32)],
        compiler_params=_params(("arbitrary", "arbitrary")),
        name="retention" if latent else "ctx_retention",
    )(*args)


def _merge_gate_body(ya_ref, yr_ref, ga_ref, gr_ref, wa_ref, wr_ref, ba_ref, br_ref, z_ref, wab_ref, wrb_ref):
    @pl.when(pl.program_id(1) == 0)
    def _():
        wab_ref[...] = wa_ref[...].astype(BF16)
        wrb_ref[...] = wr_ref[...].astype(BF16)
    a = jnp.dot(ya_ref[...], wab_ref[...], preferred_element_type=F32)
    r = jnp.dot(yr_ref[...], wrb_ref[...], preferred_element_type=F32)
    ga = jax.nn.sigmoid(ga_ref[...].astype(F32) + ba_ref[...])
    gr = jax.nn.sigmoid(gr_ref[...].astype(F32) + br_ref[...])
    z_ref[...] = (ga * a + gr * r).astype(z_ref.dtype)


def _merge_gate_call(ya, yr, proj, w_att, w_ret, gate_b, layer, rows, cols):
    d = w_att.shape[2]
    wa_rows, wr_rows = w_att.shape[1], w_ret.shape[1]
    tn = 512
    tm = _tile(rows, 1024, 16)
    ga0 = cols["gate_att"] // tn
    gr0 = cols["gate_ret"] // tn
    gb = gate_b.reshape(gate_b.shape[0], 1, 2 * d)
    return pl.pallas_call(
        _merge_gate_body,
        grid=(d // tn, rows // tm),
        in_specs=[
            pl.BlockSpec((tm, wa_rows), lambda j, i: (i, 0)),
            pl.BlockSpec((tm, wr_rows), lambda j, i: (i, 0)),
            pl.BlockSpec((tm, tn), lambda j, i: (i, ga0 + j)),
            pl.BlockSpec((tm, tn), lambda j, i: (i, gr0 + j)),
            pl.BlockSpec((None, wa_rows, tn), lambda j, i: (layer, 0, j)),
            pl.BlockSpec((None, wr_rows, tn), lambda j, i: (layer, 0, j)),
            pl.BlockSpec((None, 1, tn), lambda j, i: (layer, 0, j)),
            pl.BlockSpec((None, 1, tn), lambda j, i: (layer, 0, d // tn + j)),
        ],
        out_specs=pl.BlockSpec((tm, tn), lambda j, i: (i, j)),
        out_shape=jax.ShapeDtypeStruct((rows, d), BF16),
        scratch_shapes=[pltpu.VMEM((wa_rows, tn), BF16), pltpu.VMEM((wr_rows, tn), BF16)],
        compiler_params=_params(("arbitrary", "arbitrary")),
        name="merge_gate",
    )(ya, yr, proj, proj, w_att, w_ret, gb, gb)


def _resid_matmul_body(z_ref, w_ref, x_ref, g_ref, o_ref, wb_ref):
    @pl.when(pl.program_id(1) == 0)
    def _():
        wb_ref[...] = w_ref[...].astype(BF16)
    y = jnp.dot(z_ref[...], wb_ref[...], preferred_element_type=F32)
    o_ref[...] = x_ref[...] + g_ref[...] * y


def _resid_matmul_call(z, w, layer, xa, modl, gate_idx, rows, seq, n_batch):
    k, d = w.shape[1], w.shape[2]
    tn = 512
    tm = 512
    assert rows % tm == 0 and seq % tm == 0
    tpb = seq // tm

    def gate_map(j, i):
        return (jnp.minimum(i // tpb, n_batch) * 6 + gate_idx, 0, j)

    return pl.pallas_call(
        _resid_matmul_body,
        grid=(d // tn, rows // tm),
        in_specs=[
            pl.BlockSpec((tm, k), lambda j, i: (i, 0)),
            pl.BlockSpec((None, k, tn), lambda j, i: (layer, 0, j)),
            pl.BlockSpec((tm, tn), lambda j, i: (i, j)),
            pl.BlockSpec((None, 1, tn), gate_map),
        ],
        out_specs=pl.BlockSpec((tm, tn), lambda j, i: (i, j)),
        out_shape=jax.ShapeDtypeStruct((rows, d), F32),
        scratch_shapes=[pltpu.VMEM((k, tn), BF16)],
        compiler_params=_params(("arbitrary", "arbitrary")),
        name="resid_matmul",
    )(z, w, xa, modl)


EXPERT_TILE = 256
COMBINE_TILE = 128
DISPATCH_TILE = 256


def _routing(top_idx, n_exp, tm, n_tiles):
    n_assign = top_idx.size
    e_flat = top_idx.reshape(n_assign)
    onehot = (e_flat[:, None] == jnp.arange(n_exp, dtype=jnp.int32)[None, :]).astype(jnp.int32)
    running = jnp.cumsum(onehot, axis=0)
    counts = running[-1]
    group = (counts + tm - 1) // tm * tm
    group_end = jnp.cumsum(group)
    slot = jnp.sum(onehot * (running - 1 + (group_end - group)[None, :]), axis=1).astype(jnp.int32)
    n_used = (group_end[-1] // tm).astype(jnp.int32).reshape(1)
    tile_start = jnp.arange(n_tiles, dtype=jnp.int32) * tm
    tile_expert = jnp.sum((group_end[None, :] <= tile_start[:, None]).astype(jnp.int32), axis=1)
    tile_expert = jnp.minimum(tile_expert, n_exp - 1).astype(jnp.int32)
    last_tile = jnp.stack([jnp.maximum(group_end - tm, 0), (group > 0).astype(jnp.int32)], axis=1)
    return tile_expert, n_used, slot, last_tile.reshape(2 * n_exp).astype(jnp.int32)


def _expert_changed(te_ref, t):
    return jnp.logical_or(t == 0, te_ref[t] != te_ref[jnp.maximum(t - 1, 0)])


def _dispatch_body(slot_ref, last_ref, nu_ref, h_ref, xs_hbm, zero_ref, sem, zsem, *, tr, tm, n_exp, n_tiles):
    t = pl.program_id(0)

    @pl.when(t == 0)
    def _():
        zero_ref[...] = jnp.zeros(zero_ref.shape, zero_ref.dtype)

        def zero_copy(start):
            return pltpu.make_async_copy(zero_ref, xs_hbm.at[pl.ds(pl.multiple_of(start, tm), tm), :], zsem)

        def start_one(e, carry):
            @pl.when(last_ref[2 * e + 1] == 1)
            def _():
                zero_copy(last_ref[2 * e]).start()
            return carry

        def wait_one(e, carry):
            @pl.when(last_ref[2 * e + 1] == 1)
            def _():
                zero_copy(last_ref[2 * e]).wait()
            return carry

        def start_tail(i, carry):
            zero_copy(i * tm).start()
            return carry

        def wait_tail(i, carry):
            zero_copy(i * tm).wait()
            return carry

        lax.fori_loop(0, n_exp, start_one, 0)
        lax.fori_loop(nu_ref[0], n_tiles, start_tail, 0)
        lax.fori_loop(0, n_exp, wait_one, 0)
        lax.fori_loop(nu_ref[0], n_tiles, wait_tail, 0)

    base = t * (tr * TOP_K)

    def body(r, carry):
        for k in range(TOP_K):
            row = slot_ref[base + r * TOP_K + k]
            pltpu.make_async_copy(h_ref.at[pl.ds(r, 1), :], xs_hbm.at[pl.ds(row, 1), :], sem).start()
        return carry
    lax.fori_loop(0, tr, body, 0, unroll=2)

    for _ in range(TOP_K):
        pltpu.make_async_copy(h_ref, xs_hbm.at[pl.ds(0, tr), :], sem).wait()


def _dispatch_call(h2p, slot, last_tile, n_used, rows, n_tiles, n_exp):
    w = h2p.shape[1]
    tr = DISPATCH_TILE
    tm = EXPERT_TILE
    assert rows % tr == 0
    grid_spec = pltpu.PrefetchScalarGridSpec(
        num_scalar_prefetch=3,
        grid=(rows // tr,),
        in_specs=[pl.BlockSpec((tr, w), lambda t, sl, lt, nu: (t, 0))],
        out_specs=pl.BlockSpec(memory_space=pl.ANY),
        scratch_shapes=[pltpu.VMEM((tm, w), jnp.int32), pltpu.SemaphoreType.DMA(()), pltpu.SemaphoreType.DMA(())],
    )
    return pl.pallas_call(
        functools.partial(_dispatch_body, tr=tr, tm=tm, n_exp=n_exp, n_tiles=n_tiles),
        grid_spec=grid_spec,
        out_shape=jax.ShapeDtypeStruct((n_tiles * tm, w), jnp.int32),
        compiler_params=_params(("arbitrary",)),
        name="expert_dispatch",
    )(slot, last_tile, n_used, h2p)


def _expert_up_body(te_ref, nu_ref, x_ref, wgu_ref, bgu_ref, sel_ref, a_ref, wb_ref):
    t = pl.program_id(0)
    n_used = nu_ref[0]

    @pl.when(t < n_used)
    def _():
        @pl.when(_expert_changed(te_ref, t))
        def _():
            wb_ref[...] = wgu_ref[...].astype(BF16)

        x_lo, x_hi = _unpack_halves(x_ref[...])
        half = x_lo.shape[1]
        gu = (jnp.dot(x_lo, wb_ref[:half, :], preferred_element_type=F32)
              + jnp.dot(x_hi, wb_ref[half:, :], preferred_element_type=F32)) + bgu_ref[...]
        n2 = gu.shape[1]
        even = lax.broadcasted_iota(jnp.int32, gu.shape, 1) % 2 == 0
        gate = jnp.minimum(gu, SWIGLU_LIMIT)
        up = jnp.clip(pltpu.roll(gu, n2 - 1, 1), -SWIGLU_LIMIT, SWIGLU_LIMIT)
        act = jnp.where(even, (up + 1.0) * (gate * jax.nn.sigmoid(SWIGLU_ALPHA * gate)), 0.0)
        a_ref[...] = jnp.dot(act.astype(BF16), sel_ref[...], preferred_element_type=F32).astype(a_ref.dtype)

    @pl.when(t >= n_used)
    def _():
        a_ref[...] = jnp.zeros(a_ref.shape, a_ref.dtype)


def _used_tile(t, nu):
    return jnp.minimum(t, nu[0] - 1)


def _expert_up_call(xs, w_gu, b_gu, layer, tile_expert, n_used, n_tiles):
    n_exp, d, ff2 = w_gu.shape[1:]
    ff = ff2 // 2
    tm = EXPERT_TILE
    sel = (jnp.arange(ff2, dtype=jnp.int32)[:, None] == 2 * jnp.arange(ff, dtype=jnp.int32)[None, :]).astype(BF16)
    grid_spec = pltpu.PrefetchScalarGridSpec(
        num_scalar_prefetch=2,
        grid=(n_tiles,),
        in_specs=[
            pl.BlockSpec((tm, d // 2), lambda t, te, nu: (_used_tile(t, nu), 0)),
            pl.BlockSpec((None, None, d, ff2), lambda t, te, nu: (layer, te[t], 0, 0)),
            pl.BlockSpec((None, None, 1, ff2), lambda t, te, nu: (layer, te[t], 0, 0)),
            pl.BlockSpec((ff2, ff), lambda t, te, nu: (0, 0)),
        ],
        out_specs=pl.BlockSpec((tm, ff), lambda t, te, nu: (t, 0)),
        scratch_shapes=[pltpu.VMEM((d, ff2), BF16)],
    )
    return pl.pallas_call(
        _expert_up_body,
        grid_spec=grid_spec,
        out_shape=jax.ShapeDtypeStruct((n_tiles * tm, ff), BF16),
        compiler_params=_params(("arbitrary",)),
        name="expert_up",
    )(tile_expert, n_used, xs, w_gu, b_gu.reshape(b_gu.shape[0], n_exp, 1, ff2), sel)


def _expert_down_body(te_ref, nu_ref, a_ref, wd_ref, bd_ref, y_ref, wb_ref):
    t = pl.program_id(0)

    @pl.when(t < nu_ref[0])
    def _():
        @pl.when(_expert_changed(te_ref, t))
        def _():
            wb_ref[...] = wd_ref[...].astype(BF16)
        y = jnp.dot(a_ref[...], wb_ref[...], preferred_element_type=F32) + bd_ref[...]
        y_ref[...] = _pack_halves(y)

    @pl.when(t >= nu_ref[0])
    def _():
        y_ref[...] = jnp.zeros(y_ref.shape, y_ref.dtype)


def _expert_down_call(act, w_down, b_down, layer, tile_expert, n_used, n_tiles):
    n_exp, ff, d = w_down.shape[1:]
    tm = EXPERT_TILE
    grid_spec = pltpu.PrefetchScalarGridSpec(
        num_scalar_prefetch=2,
        grid=(n_tiles,),
        in_specs=[
            pl.BlockSpec((tm, ff), lambda t, te, nu: (t, 0)),
            pl.BlockSpec((None, None, ff, d), lambda t, te, nu: (layer, te[t], 0, 0)),
            pl.BlockSpec((None, None, 1, d), lambda t, te, nu: (layer, te[t], 0, 0)),
        ],
        out_specs=pl.BlockSpec((tm, d // 2), lambda t, te, nu: (t, 0)),
        scratch_shapes=[pltpu.VMEM((ff, d), BF16)],
    )
    return pl.pallas_call(
        _expert_down_body,
        grid_spec=grid_spec,
        out_shape=jax.ShapeDtypeStruct((n_tiles * tm, d // 2), jnp.int32),
        compiler_params=_params(("arbitrary",)),
        name="expert_down",
    )(tile_expert, n_used, act, w_down, b_down.reshape(b_down.shape[0], n_exp, 1, d))


def _combine_body(pos_ref, y_hbm, w_ref, x_ref, g_ref, o_ref, ybuf, sem, *, tc):
    t = pl.program_id(0)
    slot = lax.rem(t, 2)

    def gather(tile, s):
        base = tile * (tc * TOP_K)

        def body(r, carry):
            for k in range(TOP_K):
                row = pos_ref[base + r * TOP_K + k]
                pltpu.make_async_copy(y_hbm.at[pl.ds(row, 1), :], ybuf.at[s, k, pl.ds(r, 1), :], sem.at[s]).start()
            return carry
        lax.fori_loop(0, tc, body, 0, unroll=2)

    @pl.when(t == 0)
    def _():
        gather(0, 0)

    @pl.when(t + 1 < pl.num_programs(0))
    def _():
        gather(t + 1, 1 - slot)

    for k in range(TOP_K):
        pltpu.make_async_copy(y_hbm.at[pl.ds(0, tc), :], ybuf.at[slot, k], sem.at[slot]).wait()
    half = ybuf.shape[-1]
    w = w_ref[...]
    total_lo = total_hi = None
    for k in range(TOP_K):
        lo, hi = _unpack_halves(ybuf[slot, k])
        wk = w[:, k:k + 1]
        total_lo = wk * lo.astype(F32) if k == 0 else total_lo + wk * lo.astype(F32)
        total_hi = wk * hi.astype(F32) if k == 0 else total_hi + wk * hi.astype(F32)
    o_ref[:, :half] = x_ref[:, :half] + g_ref[:, :half] * total_lo
    o_ref[:, half:] = x_ref[:, half:] + g_ref[:, half:] * total_hi


def _combine_call(y_sorted, slot, top_w, xa, modl, gate_idx, rows, seq, n_batch):
    d = xa.shape[1]
    tc = COMBINE_TILE
    assert rows % tc == 0 and seq % tc == 0
    tpb = seq // tc
    grid_spec = pltpu.PrefetchScalarGridSpec(
        num_scalar_prefetch=1,
        grid=(rows // tc,),
        in_specs=[
            pl.BlockSpec(memory_space=pl.ANY),
            pl.BlockSpec((tc, LANES), lambda t, pos: (t, 0)),
            pl.BlockSpec((tc, d), lambda t, pos: (t, 0)),
            pl.BlockSpec((None, 1, d), lambda t, pos: (jnp.minimum(t // tpb, n_batch) * 6 + gate_idx, 0, 0)),
        ],
        out_specs=pl.BlockSpec((tc, d), lambda t, pos: (t, 0)),
        scratch_shapes=[pltpu.VMEM((2, TOP_K, tc, d // 2), jnp.int32), pltpu.SemaphoreType.DMA((2,))],
    )
    return pl.pallas_call(
        functools.partial(_combine_body, tc=tc),
        grid_spec=grid_spec,
        out_shape=jax.ShapeDtypeStruct((rows, d), F32),
        compiler_params=_params(("arbitrary",)),
        name="expert_combine",
    )(slot, y_sorted, top_w, xa, modl)


def _axial_angles(seq, head_dim):
    rows = seq // GRID_W
    row = jnp.repeat(jnp.arange(rows, dtype=F32), GRID_W)
    col = jnp.tile(jnp.arange(GRID_W, dtype=F32), rows)
    n_freq = head_dim // 4
    inv_freq = ROPE_BASE ** (-jnp.arange(n_freq, dtype=F32) / n_freq)
    return jnp.concatenate([row[:, None] * inv_freq, col[:, None] * inv_freq], axis=-1)


def _rope_tables(seq, ret_head_dim):
    ang_a = _axial_angles(seq, ATT_HEAD_DIM)
    reps = LANES // ATT_HEAD_DIM
    cos_a = jnp.tile(jnp.concatenate([jnp.cos(ang_a), jnp.cos(ang_a)], axis=-1), (1, reps))
    sin_a = jnp.tile(jnp.concatenate([-jnp.sin(ang_a), jnp.sin(ang_a)], axis=-1), (1, reps))
    ang_r = _axial_angles(seq, ret_head_dim)
    return (cos_a, sin_a), (jnp.cos(ang_r), jnp.sin(ang_r))


def kernel(x, c, ctx, c_ctx, ada_w, ada_b, norm_mix_g, norm_ffn_g, w_in, att_sink, ret_decay_fwd,
           ret_decay_bwd, w_branch_att, w_branch_ret, branch_gate_b, w_out, router_w, router_b,
           exp_w_gu, exp_b_gu, exp_w_down, exp_b_down, final_norm_g):
    n_batch, seq, d = x.shape
    n_ctx = ctx.shape[1]
    depth = ada_w.shape[0]
    n_exp, ff = exp_w_down.shape[1], exp_w_down.shape[2]
    att_width = w_branch_att.shape[1]
    ret_width = w_branch_ret.shape[1]
    kv_width = att_width // ATT_GROUP
    sizes = (("k_att", kv_width), ("v_att", kv_width), ("k_ret", ret_width), ("v_ret", ret_width),
             ("q_att", att_width), ("q_ret", ret_width), ("g_ret", ret_width),
             ("gate_att", d), ("gate_ret", d))
    cols, off = {}, 0
    for name, width in sizes:
        cols[name] = off
        off += width
    in_cols = off
    assert in_cols == w_in.shape[2]

    lat_rows = n_batch * seq
    all_rows = lat_rows + n_batch * n_ctx
    xa = jnp.concatenate([x.reshape(lat_rows, d), ctx.reshape(n_batch * n_ctx, d)], axis=0)

    cond = jnp.zeros((16, d), F32).at[:n_batch].set(c).at[n_batch].set(c_ctx)
    mod = _mod_call(cond, ada_w, ada_b)
    rope_att, rope_ret = _rope_tables(seq, ret_width // RET_HEADS)

    for layer in range(depth):
        last = layer == depth - 1
        rows = lat_rows if last else all_rows
        modl = mod[layer, :n_batch + 1].reshape((n_batch + 1) * 6, 1, d)
        log_gamma = jnp.stack([jax.nn.log_sigmoid(ret_decay_fwd[layer].astype(F32)),
                               jax.nn.log_sigmoid(ret_decay_bwd[layer].astype(F32))])

        h = _norm_call(xa, norm_mix_g[layer], modl, all_rows, seq, n_batch, 0, 1)
        proj = _inproj_call(h, w_in, layer, all_rows, 0, in_cols)

        ya = _attn_call(proj, att_sink[layer], rope_att, cols, n_batch, seq, n_ctx, att_width, True)
        yr = _ret_call(proj, log_gamma, rope_ret, cols, n_batch, seq, n_ctx, ret_width, True)
        if not last:
            ya_c = _attn_call(proj, att_sink[layer], None, cols, n_batch, seq, n_ctx, att_width, False)
            yr_c = _ret_call(proj, log_gamma, None, cols, n_batch, seq, n_ctx, ret_width, False)
            ya = jnp.concatenate([ya, ya_c], axis=0)
            yr = jnp.concatenate([yr, yr_c], axis=0)

        z = _merge_gate_call(ya, yr, proj, w_branch_att, w_branch_ret, branch_gate_b, layer, rows, cols)
        xa = _resid_matmul_call(z, w_out, layer, xa, modl, 2, rows, seq, n_batch)

        h2, top_idx, top_w = _norm_call(xa, norm_ffn_g[layer], modl, rows, seq, n_batch, 3, 4,
                                        router=(router_w[layer], router_b[layer]))
        n_tiles = -(-rows * TOP_K // EXPERT_TILE) + n_exp
        tile_expert, n_used, slot, last_tile = _routing(top_idx[:, :TOP_K], n_exp, EXPERT_TILE, n_tiles)
        xs = _dispatch_call(h2, slot, last_tile, n_used, rows, n_tiles, n_exp)
        act = _expert_up_call(xs, exp_w_gu, exp_b_gu, layer, tile_expert, n_used, n_tiles)
        y_sorted = _expert_down_call(act, exp_w_down, exp_b_down, layer, tile_expert, n_used, n_tiles)
        xa = _combine_call(y_sorted, slot, top_w, xa, modl, 5, rows, seq, n_batch)

    out = _final_norm_call(xa, final_norm_g, lat_rows)
    return out.reshape(n_batch, seq, d)
```

```python
import functools

import jax
import jax.numpy as jnp
from jax import lax
from jax.experimental import pallas as pl
from jax.experimental.pallas import tpu as pltpu

F32 = jnp.float32
BF16 = jnp.bfloat16

GRID_W = 64
ATT_HEAD_DIM = 64
ATT_GROUP = 4
ATT_WINDOW = 128
ATT_BLOCK = 128
RET_HEADS = 8
RET_CHUNK = 128
RET_UNROLL_BWD = 4
RET_UNROLL_FWD = 8
TOP_K = 4
SWIGLU_ALPHA = 1.702
SWIGLU_LIMIT = 7.0
ROPE_BASE = 10000.0
NORM_EPS = 1e-6
GN_EPS = 1e-5
NEG_INF = -1e30
LOG2_E = 1.4426950408889634

LANES = 128
VMEM_LIMIT = 56 * 1024 * 1024


def _params(sem, vmem=VMEM_LIMIT):
    return pltpu.CompilerParams(dimension_semantics=sem, vmem_limit_bytes=vmem)


def _tile(n, cap, mult):
    for t in range(min(cap, n), 0, -1):
        if n % t == 0 and t % mult == 0:
            return t
    raise ValueError(f"no tile for {n} (cap {cap}, multiple of {mult})")


def _silu(v):
    return v * jax.nn.sigmoid(v)


def _mod_body(c_ref, w_ref, b_ref, o_ref):
    s = _silu(c_ref[...]).astype(BF16)
    o_ref[...] = jnp.dot(s, w_ref[...].astype(BF16), preferred_element_type=F32) + b_ref[...]


def _mod_call(cond, ada_w, ada_b):
    depth, d, n = ada_w.shape
    rows = cond.shape[0]
    tn = _tile(n, 512, LANES)
    return pl.pallas_call(
        _mod_body,
        grid=(depth, n // tn),
        in_specs=[
            pl.BlockSpec((rows, d), lambda l, j: (0, 0)),
            pl.BlockSpec((None, d, tn), lambda l, j: (l, 0, j)),
            pl.BlockSpec((None, 1, tn), lambda l, j: (l, 0, j)),
        ],
        out_specs=pl.BlockSpec((None, rows, tn), lambda l, j: (l, 0, j)),
        out_shape=jax.ShapeDtypeStruct((depth, rows, n), F32),
        compiler_params=_params(("arbitrary", "arbitrary")),
        name="adaln_mod",
    )(cond, ada_w, ada_b.reshape(depth, 1, n))


def _rms(x, g):
    ms = jnp.mean(x * x, axis=-1, keepdims=True)
    return x * lax.rsqrt(ms + NORM_EPS) * g


def _norm_mod_body(x_ref, g_ref, sh_ref, sc_ref, h_ref):
    h = _rms(x_ref[...], g_ref[...]) * (1.0 + sc_ref[...]) + sh_ref[...]
    h_ref[...] = h.astype(h_ref.dtype)


def _pack_halves(v):
    w = v.shape[1] // 2
    bits = lax.bitcast_convert_type(v.astype(BF16).astype(F32), jnp.int32)
    return lax.shift_right_logical(bits[:, :w], 16) | (bits[:, w:] & jnp.int32(-65536))


def _unpack_halves(p):
    lo = lax.bitcast_convert_type(lax.shift_left(p, 16), F32)
    hi = lax.bitcast_convert_type(p & jnp.int32(-65536), F32)
    return lo, hi


def _split_bf16(v):
    hi = v.astype(BF16)
    lo = (v - hi.astype(F32)).astype(BF16)
    return hi, lo


def _norm_router_body(x_ref, g_ref, sh_ref, sc_ref, rw_ref, rb_ref, h_ref, idx_ref, wt_ref):
    h = _rms(x_ref[...], g_ref[...]) * (1.0 + sc_ref[...]) + sh_ref[...]
    h_ref[...] = _pack_halves(h)
    h_hi, h_lo = _split_bf16(h)
    w_hi, w_lo = _split_bf16(rw_ref[...])
    logits = (jnp.dot(h_hi, w_hi, preferred_element_type=F32)
              + jnp.dot(h_hi, w_lo, preferred_element_type=F32)
              + jnp.dot(h_lo, w_hi, preferred_element_type=F32)) + rb_ref[...]
    n_exp = logits.shape[-1]
    lane = lax.broadcasted_iota(jnp.int32, logits.shape, 1).astype(F32)
    work = logits
    out_lane = lax.broadcasted_iota(jnp.int32, idx_ref.shape, 1)
    idx = jnp.zeros(idx_ref.shape, F32)
    ex = jnp.zeros(wt_ref.shape, F32)
    top = None
    total = None
    for k in range(TOP_K):
        m = jnp.max(work, axis=-1, keepdims=True)
        first = jnp.min(jnp.where(work == m, lane, float(n_exp)), axis=-1, keepdims=True)
        work = jnp.where(lane == first, -jnp.inf, work)
        if k == 0:
            top = m
        e = jnp.exp(m - top)
        total = e if k == 0 else total + e
        idx = jnp.where(out_lane == k, first, idx)
        ex = jnp.where(out_lane == k, e, ex)
    idx_ref[...] = idx.astype(jnp.int32)
    wt_ref[...] = ex / total


def _seg_map(tiles_per_batch, n_batch, which):
    def index(i):
        return (jnp.minimum(i // tiles_per_batch, n_batch) * 6 + which, 0, 0)
    return index


def _norm_call(xa, g, modl, rows, seq, n_batch, shift_idx, scale_idx, router=None):
    d = xa.shape[1]
    tr = 256
    assert rows % tr == 0 and seq % tr == 0
    tpb = seq // tr
    row_spec = pl.BlockSpec((tr, d), lambda i: (i, 0))
    in_specs = [
        row_spec,
        pl.BlockSpec((1, d), lambda i: (0, 0)),
        pl.BlockSpec((None, 1, d), _seg_map(tpb, n_batch, shift_idx)),
        pl.BlockSpec((None, 1, d), _seg_map(tpb, n_batch, scale_idx)),
    ]
    args = [xa, g.reshape(1, d), modl, modl]
    if router is None:
        return pl.pallas_call(
            _norm_mod_body, grid=(rows // tr,), in_specs=in_specs, out_specs=row_spec,
            out_shape=jax.ShapeDtypeStruct((rows, d), BF16),
            compiler_params=_params(("arbitrary",)), name="norm_mod",
        )(*args)
    rw, rb = router
    n_exp = rw.shape[1]
    in_specs += [pl.BlockSpec((d, n_exp), lambda i: (0, 0)), pl.BlockSpec((1, n_exp), lambda i: (0, 0))]
    args += [rw, rb.reshape(1, n_exp)]
    return pl.pallas_call(
        _norm_router_body, grid=(rows // tr,), in_specs=in_specs,
        out_specs=[pl.BlockSpec((tr, d // 2), lambda i: (i, 0)),
                   pl.BlockSpec((tr, LANES), lambda i: (i, 0)), pl.BlockSpec((tr, LANES), lambda i: (i, 0))],
        out_shape=[jax.ShapeDtypeStruct((rows, d // 2), jnp.int32), jax.ShapeDtypeStruct((rows, LANES), jnp.int32),
                   jax.ShapeDtypeStruct((rows, LANES), F32)],
        compiler_params=_params(("arbitrary",)), name="norm_router",
    )(*args)


def _inproj_body(a_ref, w_ref, o_ref, wb_ref):
    @pl.when(pl.program_id(1) == 0)
    def _():
        wb_ref[...] = w_ref[...].astype(BF16)
    o_ref[...] = jnp.dot(a_ref[...], wb_ref[...], preferred_element_type=F32).astype(o_ref.dtype)


def _inproj_call(h, w_in, layer, rows, col_lo, col_hi):
    d = h.shape[1]
    tn = 512
    assert col_lo % tn == 0 and col_hi % tn == 0
    tm = _tile(rows, 1152, 16)
    j0 = col_lo // tn
    return pl.pallas_call(
        _inproj_body,
        grid=((col_hi - col_lo) // tn, rows // tm),
        in_specs=[
            pl.BlockSpec((tm, d), lambda j, i: (i, 0)),
            pl.BlockSpec((None, d, tn), lambda j, i: (layer, 0, j0 + j)),
        ],
        out_specs=pl.BlockSpec((tm, tn), lambda j, i: (i, j)),
        out_shape=jax.ShapeDtypeStruct((rows, col_hi - col_lo), BF16),
        scratch_shapes=[pltpu.VMEM((d, tn), BF16)],
        compiler_params=_params(("arbitrary", "arbitrary")),
        name="in_proj",
    )(h, w_in)


def _attn_body(*refs, local, scale):
    if local:
        (sink_ref, q_ref, kp_ref, kc_ref, kn_ref, vp_ref, vc_ref, vn_ref, kx_ref, vx_ref,
         cp_ref, cc_ref, cn_ref, sp_ref, sc_ref, sn_ref, bias_ref, o_ref) = refs
    else:
        sink_ref, q_ref, kx_ref, vx_ref, o_ref = refs
    pair = pl.program_id(1)
    n = pl.program_id(2)
    blk = ATT_BLOCK
    half = ATT_HEAD_DIM // 2

    lane_q = lax.broadcasted_iota(jnp.int32, (blk, LANES), 1)
    first_half = (lane_q % ATT_HEAD_DIM) < half

    def rope(v, c_ref, s_ref):
        rot = jnp.where(first_half, pltpu.roll(v, LANES - half, 1), pltpu.roll(v, half, 1))
        return v * c_ref[...] + rot * s_ref[...]

    if local:
        keys = jnp.concatenate([
            rope(kp_ref[...].astype(F32), cp_ref, sp_ref),
            rope(kc_ref[...].astype(F32), cc_ref, sc_ref),
            rope(kn_ref[...].astype(F32), cn_ref, sn_ref),
            kx_ref[...].astype(F32)], axis=0)
        vals = jnp.concatenate([vp_ref[...], vc_ref[...], vn_ref[...], vx_ref[...]], axis=0).astype(F32)
    else:
        keys = kx_ref[...].astype(F32)
        vals = vx_ref[...].astype(F32)
    nk = keys.shape[0]

    lane_k = lax.broadcasted_iota(jnp.int32, (nk, LANES), 1)
    low = lane_k < ATT_HEAD_DIM

    def placed(v):
        swapped = pltpu.roll(v, ATT_HEAD_DIM, 1)
        return [[jnp.where(low, v, 0.0).astype(BF16), jnp.where(low, 0.0, swapped).astype(BF16)],
                [jnp.where(low, swapped, 0.0).astype(BF16), jnp.where(low, 0.0, v).astype(BF16)]]

    k_var = placed(keys)
    v_var = placed(vals)

    top_rows = lax.broadcasted_iota(jnp.int32, (2 * blk, 1), 0) < blk

    n_slab = q_ref.shape[1] // LANES
    q_slabs = []
    for c in range(n_slab):
        qc = q_ref[:, c * LANES:(c + 1) * LANES].astype(F32)
        if local:
            qc = rope(qc, cc_ref, sc_ref)
        q_slabs.append((qc * (scale * LOG2_E)).astype(BF16))

    out = [jnp.zeros((blk, LANES), F32) for _ in range(n_slab)]
    combos = [(j, hh) for j in range(2) for hh in range(2)]
    scores = []
    for j, hh in combos:
        lhs = jnp.concatenate([q_slabs[2 * j], q_slabs[2 * j + 1]], axis=0)
        s = lax.dot_general(lhs, k_var[j][hh], (((1,), (1,)), ((), ())),
                            preferred_element_type=F32)
        if local:
            s = s + bias_ref[...]
        scores.append(s)
    probs = []
    for (j, hh), s in zip(combos, scores):
        head0 = pair * (2 * ATT_GROUP) + ATT_GROUP * j + hh
        sink = jnp.where(top_rows, sink_ref[head0], sink_ref[head0 + 2]) * LOG2_E
        m = jnp.maximum(jnp.max(s, axis=-1, keepdims=True), sink)
        e = jnp.exp2(s - m)
        denom = jnp.sum(e, axis=-1, keepdims=True) + jnp.exp2(sink - m)
        probs.append((e.astype(BF16), 1.0 / denom))
    for (j, hh), (e, inv) in zip(combos, probs):
        o = jnp.dot(e, v_var[j][hh], preferred_element_type=F32) * inv
        out[2 * j] = out[2 * j] + o[:blk]
        out[2 * j + 1] = out[2 * j + 1] + o[blk:]
    for c in range(n_slab):
        o_ref[:, c * LANES:(c + 1) * LANES] = out[c].astype(o_ref.dtype)


def _attn_call(proj, sink, rope_tabs, cols, n_batch, seq, n_ctx, att_width, local):
    blk = ATT_BLOCK
    nb = seq // blk
    pair_w = 2 * ATT_HEAD_DIM
    q_w = pair_w * ATT_GROUP
    n_pair = att_width // q_w
    k0 = cols["k_att"] // pair_w
    v0 = cols["v_att"] // pair_w
    q0 = cols["q_att"] // q_w
    ctx_row0 = (n_batch * seq) // n_ctx
    kx_spec = pl.BlockSpec((n_ctx, pair_w), lambda b, p, n: (ctx_row0 + b, k0 + p))
    vx_spec = pl.BlockSpec((n_ctx, pair_w), lambda b, p, n: (ctx_row0 + b, v0 + p))
    sink_spec = pl.BlockSpec(memory_space=pltpu.SMEM)
    scale = ATT_HEAD_DIM ** -0.5
    if local:
        cos_t, sin_t = rope_tabs

        def prev(n):
            return jnp.maximum(n - 1, 0)

        def nxt(n):
            return jnp.minimum(n + 1, nb - 1)

        def kv(c0, f):
            return pl.BlockSpec((blk, pair_w), lambda b, p, n: (b * nb + f(n), c0 + p))

        def tab(f):
            return pl.BlockSpec((blk, LANES), lambda b, p, n: (f(n), 0))

        ident = lambda n: n
        assert nb >= 2
        r = jnp.arange(2 * blk, dtype=jnp.int32)[:, None] % blk
        c = jnp.arange(3 * blk + n_ctx, dtype=jnp.int32)[None, :]
        band = (jnp.abs(c - blk - r) <= ATT_WINDOW) | (c >= 3 * blk)
        edge = jnp.stack([band & (c >= blk), band, band & ((c < 2 * blk) | (c >= 3 * blk))])
        bias = jnp.where(edge, 0.0, NEG_INF).astype(F32)
        bias_spec = pl.BlockSpec((None, 2 * blk, 3 * blk + n_ctx),
                                 lambda b, p, n: ((n > 0).astype(jnp.int32) + (n == nb - 1).astype(jnp.int32), 0, 0))
        in_specs = [sink_spec,
                    pl.BlockSpec((blk, q_w), lambda b, p, n: (b * nb + n, q0 + p)),
                    kv(k0, prev), kv(k0, ident), kv(k0, nxt),
                    kv(v0, prev), kv(v0, ident), kv(v0, nxt),
                    kx_spec, vx_spec,
                    tab(prev), tab(ident), tab(nxt), tab(prev), tab(ident), tab(nxt), bias_spec]
        args = [sink, proj] + [proj] * 6 + [proj, proj] + [cos_t] * 3 + [sin_t] * 3 + [bias]
        grid = (n_batch, n_pair, nb)
        rows = n_batch * seq
        out_spec = pl.BlockSpec((blk, q_w), lambda b, p, n: (b * nb + n, p))
    else:
        nqb = n_ctx // blk
        qrow0 = (n_batch * seq) // blk
        in_specs = [sink_spec,
                    pl.BlockSpec((blk, q_w), lambda b, p, n: (qrow0 + b * nqb + n, q0 + p)),
                    kx_spec, vx_spec]
        args = [sink, proj, proj, proj]
        grid = (n_batch, n_pair, nqb)
        rows = n_batch * n_ctx
        out_spec = pl.BlockSpec((blk, q_w), lambda b, p, n: (b * nqb + n, p))
    return pl.pallas_call(
        functools.partial(_attn_body, local=local, scale=scale),
        grid=grid, in_specs=in_specs, out_specs=out_spec,
        out_shape=jax.ShapeDtypeStruct((rows, att_width), BF16),
        compiler_params=_params(("arbitrary",) * 3),
        name="window_attn" if local else "ctx_attn",
    )(*args)


def _dot_tn(a, b):
    return lax.dot_general(a, b, (((0,), (0,)), ((), ())), preferred_element_type=F32)


def _ret_body(*refs, latent, n_chunks, n_ctx, k_scale):
    if latent:
        (lg_ref, q_ref, k_ref, v_ref, g_ref, kx_ref, vx_ref, cos_ref, sin_ref,
         o_ref, qs_ref, ks_ref, cb_ref, st_ref, sf_ref) = refs
    else:
        lg_ref, q_ref, k_ref, v_ref, g_ref, o_ref, qs_ref, ks_ref, cb_ref, st_ref, sf_ref = refs
    head = pl.program_id(1)
    lg_f = lg_ref[0, head]
    lg_b = lg_ref[1, head]
    c = RET_CHUNK
    hd = q_ref.shape[1]
    hh = hd // 2
    idx = lax.broadcasted_iota(jnp.int32, (c, 1), 0).astype(F32)
    qdec_f = jnp.exp(lg_f * (idx + 1.0))
    kdec_f = jnp.exp(lg_f * (c - 1.0 - idx))
    qdec_b = jnp.exp(lg_b * (c - idx))
    kdec_b = jnp.exp(lg_b * idx)
    one = jnp.ones((1, 1), F32)
    cdec_f = jnp.exp(lg_f * c * one)
    cdec_b = jnp.exp(lg_b * c * one)
    rel = (lax.broadcasted_iota(jnp.int32, (c, c), 0) - lax.broadcasted_iota(jnp.int32, (c, c), 1)).astype(F32)
    dmat = (jnp.where(rel >= 0, jnp.exp(lg_f * jnp.maximum(rel, 0.0)), 0.0)
            + jnp.where(rel <= 0, jnp.exp(lg_b * jnp.maximum(-rel, 0.0)), 0.0))

    def rope(v, rows):
        v1 = v[:, :hh]
        v2 = v[:, hh:]
        cs = cos_ref[rows, :]
        sn = sin_ref[rows, :]
        return jnp.concatenate([v1 * cs - v2 * sn, v2 * cs + v1 * sn], axis=1)

    if latent:
        t = lax.broadcasted_iota(jnp.int32, (n_ctx, 1), 0).astype(F32)
        kx = kx_ref[...].astype(F32) * k_scale
        sf_ref[...] = _dot_tn((kx * jnp.exp(lg_f * (n_ctx - 1.0 - t))).astype(BF16), vx_ref[...])
        st_ref[...] = _dot_tn((kx * jnp.exp(lg_b * t)).astype(BF16), vx_ref[...])
    else:
        sf_ref[...] = jnp.zeros((hd, hd), F32)
        st_ref[...] = jnp.zeros((hd, hd), F32)


    def bwd(step, carry):
        rows = pl.ds(pl.multiple_of((n_chunks - 1 - step) * c, c), c)
        q = q_ref[rows, :].astype(F32)
        k = k_ref[rows, :].astype(F32)
        if latent:
            q = rope(q, rows)
            k = rope(k, rows)
        k = k * k_scale
        qs_ref[rows, :] = q.astype(BF16)
        ks_ref[rows, :] = k.astype(BF16)
        state = st_ref[...]
        cb_ref[rows, :] = jnp.dot((q * qdec_b).astype(BF16), state.astype(BF16), preferred_element_type=F32)
        st_ref[...] = cdec_b * state + _dot_tn((k * kdec_b).astype(BF16), v_ref[rows, :])
        return carry

    lax.fori_loop(0, n_chunks, bwd, 0, unroll=min(RET_UNROLL_BWD, n_chunks))

    st_ref[...] = sf_ref[...]

    def fwd(step, carry):
        rows = pl.ds(pl.multiple_of(step * c, c), c)
        qb = qs_ref[rows, :]
        kb = ks_ref[rows, :]
        v = v_ref[rows, :]
        scores = lax.dot_general(qb, kb, (((1,), (1,)), ((), ())), preferred_element_type=F32) * dmat
        o = jnp.dot(scores.astype(BF16), v, preferred_element_type=F32)
        state = st_ref[...]
        o = o + jnp.dot((qb.astype(F32) * qdec_f).astype(BF16), state.astype(BF16), preferred_element_type=F32)
        o = o + cb_ref[rows, :]
        st_ref[...] = cdec_f * state + _dot_tn((kb.astype(F32) * kdec_f).astype(BF16), v)
        mu = jnp.mean(o, axis=-1, keepdims=True)
        dev = o - mu
        var = jnp.mean(dev * dev, axis=-1, keepdims=True)
        y = dev * lax.rsqrt(var + GN_EPS)
        o_ref[rows, :] = (_silu(g_ref[rows, :].astype(F32)) * y).astype(o_ref.dtype)
        return carry

    lax.fori_loop(0, n_chunks, fwd, 0, unroll=min(RET_UNROLL_FWD, n_chunks))


def _ret_call(proj, log_gamma, rope_tabs, cols, n_batch, seq, n_ctx, ret_width, latent):
    hd = ret_width // RET_HEADS
    t_len = seq if latent else n_ctx
    assert t_len % RET_CHUNK == 0
    k0, v0, q0, g0 = (cols[name] // hd for name in ("k_ret", "v_ret", "q_ret", "g_ret"))
    ctx_row0 = (n_batch * seq) // n_ctx
    row0 = 0 if latent else ctx_row0

    def seq_spec(c0):
        return pl.BlockSpec((t_len, hd), lambda b, h: (row0 + b, c0 + h))

    in_specs = [pl.BlockSpec(memory_space=pltpu.SMEM),
                seq_spec(q0), seq_spec(k0), seq_spec(v0), seq_spec(g0)]
    args = [log_gamma, proj, proj, proj, proj]
    if latent:
        cos_t, sin_t = rope_tabs
        in_specs += [pl.BlockSpec((n_ctx, hd), lambda b, h: (ctx_row0 + b, k0 + h)),
                     pl.BlockSpec((n_ctx, hd), lambda b, h: (ctx_row0 + b, v0 + h)),
                     pl.BlockSpec((t_len, hd // 2), lambda b, h: (0, 0)),
                     pl.BlockSpec((t_len, hd // 2), lambda b, h: (0, 0))]
        args += [proj, proj, cos_t, sin_t]
    return pl.pallas_call(
        functools.partial(_ret_body, latent=latent, n_chunks=t_len // RET_CHUNK, n_ctx=n_ctx,
                          k_scale=hd ** -0.5),
        grid=(n_batch, RET_HEADS), in_specs=in_specs,
        out_specs=pl.BlockSpec((t_len, hd), lambda b, h: (b, h)),
        out_shape=jax.ShapeDtypeStruct((n_batch * t_len, ret_width), BF16),
        scratch_shapes=[pltpu.VMEM((t_len, hd), BF16), pltpu.VMEM((t_len, hd), BF16),
                        pltpu.VMEM((t_len, hd), F32), pltpu.VMEM((hd, hd), F32), pltpu.VMEM((hd, hd), F32)],
        compiler_params=_params(("arbitrary", "arbitrary")),
        name="retention" if latent else "ctx_retention",
    )(*args)


def _merge_gate_body(ya_ref, yr_ref, ga_ref, gr_ref, wa_ref, wr_ref, ba_ref, br_ref, z_ref, wab_ref, wrb_ref):
    @pl.when(pl.program_id(1) == 0)
    def _():
        wab_ref[...] = wa_ref[...].astype(BF16)
        wrb_ref[...] = wr_ref[...].astype(BF16)
    a = jnp.dot(ya_ref[...], wab_ref[...], preferred_element_type=F32)
    r = jnp.dot(yr_ref[...], wrb_ref[...], preferred_element_type=F32)
    ga = jax.nn.sigmoid(ga_ref[...].astype(F32) + ba_ref[...])
    gr = jax.nn.sigmoid(gr_ref[...].astype(F32) + br_ref[...])
    z_ref[...] = (ga * a + gr * r).astype(z_ref.dtype)


def _merge_gate_call(ya, yr, proj, w_att, w_ret, gate_b, layer, rows, cols):
    d = w_att.shape[2]
    wa_rows, wr_rows = w_att.shape[1], w_ret.shape[1]
    tn = 512
    tm = _tile(rows, 1024, 16)
    ga0 = cols["gate_att"] // tn
    gr0 = cols["gate_ret"] // tn
    gb = gate_b.reshape(gate_b.shape[0], 1, 2 * d)
    return pl.pallas_call(
        _merge_gate_body,
        grid=(d // tn, rows // tm),
        in_specs=[
            pl.BlockSpec((tm, wa_rows), lambda j, i: (i, 0)),
            pl.BlockSpec((tm, wr_rows), lambda j, i: (i, 0)),
            pl.BlockSpec((tm, tn), lambda j, i: (i, ga0 + j)),
            pl.BlockSpec((tm, tn), lambda j, i: (i, gr0 + j)),
            pl.BlockSpec((None, wa_rows, tn), lambda j, i: (layer, 0, j)),
            pl.BlockSpec((None, wr_rows, tn), lambda j, i: (layer, 0, j)),
            pl.BlockSpec((None, 1, tn), lambda j, i: (layer, 0, j)),
            pl.BlockSpec((None, 1, tn), lambda j, i: (layer, 0, d // tn + j)),
        ],
        out_specs=pl.BlockSpec((tm, tn), lambda j, i: (i, j)),
        out_shape=jax.ShapeDtypeStruct((rows, d), BF16),
        scratch_shapes=[pltpu.VMEM((wa_rows, tn), BF16), pltpu.VMEM((wr_rows, tn), BF16)],
        compiler_params=_params(("arbitrary", "arbitrary")),
        name="merge_gate",
    )(ya, yr, proj, proj, w_att, w_ret, gb, gb)


def _resid_matmul_body(z_ref, w_ref, x_ref, g_ref, o_ref, wb_ref):
    @pl.when(pl.program_id(1) == 0)
    def _():
        wb_ref[...] = w_ref[...].astype(BF16)
    y = jnp.dot(z_ref[...], wb_ref[...], preferred_element_type=F32)
    o_ref[...] = x_ref[...] + g_ref[...] * y


def _resid_matmul_call(z, w, layer, xa, modl, gate_idx, rows, seq, n_batch):
    k, d = w.shape[1], w.shape[2]
    tn = 512
    tm = 512
    assert rows % tm == 0 and seq % tm == 0
    tpb = seq // tm

    def gate_map(j, i):
        return (jnp.minimum(i // tpb, n_batch) * 6 + gate_idx, 0, j)

    return pl.pallas_call(
        _resid_matmul_body,
        grid=(d // tn, rows // tm),
        in_specs=[
            pl.BlockSpec((tm, k), lambda j, i: (i, 0)),
            pl.BlockSpec((None, k, tn), lambda j, i: (layer, 0, j)),
            pl.BlockSpec((tm, tn), lambda j, i: (i, j)),
            pl.BlockSpec((None, 1, tn), gate_map),
        ],
        out_specs=pl.BlockSpec((tm, tn), lambda j, i: (i, j)),
        out_shape=jax.ShapeDtypeStruct((rows, d), F32),
        scratch_shapes=[pltpu.VMEM((k, tn), BF16)],
        compiler_params=_params(("arbitrary", "arbitrary")),
        name="resid_matmul",
    )(z, w, xa, modl)


EXPERT_TILE = 256
COMBINE_TILE = 128
DISPATCH_TILE = 256
COMBINE_CHUNK = 256


def _routing(top_idx, n_exp, tm, n_tiles):
    n_assign = top_idx.size
    e_flat = top_idx.reshape(n_assign)
    onehot = (e_flat[:, None] == jnp.arange(n_exp, dtype=jnp.int32)[None, :]).astype(jnp.int32)
    running = jnp.cumsum(onehot, axis=0)
    counts = running[-1]
    group = (counts + tm - 1) // tm * tm
    group_end = jnp.cumsum(group)
    slot = jnp.sum(onehot * (running - 1 + (group_end - group)[None, :]), axis=1).astype(jnp.int32)
    n_used = (group_end[-1] // tm).astype(jnp.int32).reshape(1)
    tile_start = jnp.arange(n_tiles, dtype=jnp.int32) * tm
    tile_expert = jnp.sum((group_end[None, :] <= tile_start[:, None]).astype(jnp.int32), axis=1)
    tile_expert = jnp.minimum(tile_expert, n_exp - 1).astype(jnp.int32)
    last_tile = jnp.stack([jnp.maximum(group_end - tm, 0), (group > 0).astype(jnp.int32)], axis=1)
    return tile_expert, n_used, slot, last_tile.reshape(2 * n_exp).astype(jnp.int32)


def _expert_changed(te_ref, t):
    return jnp.logical_or(t == 0, te_ref[t] != te_ref[jnp.maximum(t - 1, 0)])


def _dispatch_body(slot_ref, last_ref, nu_ref, h_ref, xs_hbm, zero_ref, sem, zsem, *, tr, tm, n_exp, n_tiles):
    t = pl.program_id(0)

    @pl.when(t == 0)
    def _():
        zero_ref[...] = jnp.zeros(zero_ref.shape, zero_ref.dtype)

        def zero_copy(start):
            return pltpu.make_async_copy(zero_ref, xs_hbm.at[pl.ds(pl.multiple_of(start, tm), tm), :], zsem)

        def start_one(e, carry):
            @pl.when(last_ref[2 * e + 1] == 1)
            def _():
                zero_copy(last_ref[2 * e]).start()
            return carry

        def wait_one(e, carry):
            @pl.when(last_ref[2 * e + 1] == 1)
            def _():
                zero_copy(last_ref[2 * e]).wait()
            return carry

        def start_tail(i, carry):
            zero_copy(i * tm).start()
            return carry

        def wait_tail(i, carry):
            zero_copy(i * tm).wait()
            return carry

        lax.fori_loop(0, n_exp, start_one, 0)
        lax.fori_loop(nu_ref[0], n_tiles, start_tail, 0)
        lax.fori_loop(0, n_exp, wait_one, 0)
        lax.fori_loop(nu_ref[0], n_tiles, wait_tail, 0)

    base = t * (tr * TOP_K)

    def body(r, carry):
        for k in range(TOP_K):
            row = slot_ref[base + r * TOP_K + k]
            pltpu.async_copy(h_ref.at[pl.ds(r, 1), :], xs_hbm.at[pl.ds(row, 1), :], sem, priority=k % 2)
        return carry
    lax.fori_loop(0, tr, body, 0, unroll=2)

    for _ in range(TOP_K):
        pltpu.make_async_copy(h_ref, xs_hbm.at[pl.ds(0, tr), :], sem).wait()


def _dispatch_call(h2p, slot, last_tile, n_used, rows, n_tiles, n_exp):
    w = h2p.shape[1]
    tr = DISPATCH_TILE
    tm = EXPERT_TILE
    assert rows % tr == 0
    grid_spec = pltpu.PrefetchScalarGridSpec(
        num_scalar_prefetch=3,
        grid=(rows // tr,),
        in_specs=[pl.BlockSpec((tr, w), lambda t, sl, lt, nu: (t, 0))],
        out_specs=pl.BlockSpec(memory_space=pl.ANY),
        scratch_shapes=[pltpu.VMEM((tm, w), jnp.int32), pltpu.SemaphoreType.DMA(()), pltpu.SemaphoreType.DMA(())],
    )
    return pl.pallas_call(
        functools.partial(_dispatch_body, tr=tr, tm=tm, n_exp=n_exp, n_tiles=n_tiles),
        grid_spec=grid_spec,
        out_shape=jax.ShapeDtypeStruct((n_tiles * tm, w), jnp.int32),
        compiler_params=_params(("arbitrary",)),
        name="expert_dispatch",
    )(slot, last_tile, n_used, h2p)


def _expert_up_body(te_ref, nu_ref, x_ref, wgu_ref, bgu_ref, sel_ref, a_ref, wb_ref):
    t = pl.program_id(0)
    n_used = nu_ref[0]

    @pl.when(t < n_used)
    def _():
        @pl.when(_expert_changed(te_ref, t))
        def _():
            wb_ref[...] = wgu_ref[...].astype(BF16)

        x_lo, x_hi = _unpack_halves(x_ref[...])
        half = x_lo.shape[1]
        gu = (jnp.dot(x_lo.astype(BF16), wb_ref[:half, :], preferred_element_type=F32)
              + jnp.dot(x_hi.astype(BF16), wb_ref[half:, :], preferred_element_type=F32)) + bgu_ref[...]
        n2 = gu.shape[1]
        even = lax.broadcasted_iota(jnp.int32, gu.shape, 1) % 2 == 0
        gate = jnp.minimum(gu, SWIGLU_LIMIT)
        up = jnp.clip(pltpu.roll(gu, n2 - 1, 1), -SWIGLU_LIMIT, SWIGLU_LIMIT)
        act = jnp.where(even, (up + 1.0) * (gate * jax.nn.sigmoid(SWIGLU_ALPHA * gate)), 0.0)
        a_ref[...] = jnp.dot(act.astype(BF16), sel_ref[...], preferred_element_type=F32).astype(a_ref.dtype)

    @pl.when(t >= n_used)
    def _():
        a_ref[...] = jnp.zeros(a_ref.shape, a_ref.dtype)


def _used_tile(t, nu):
    return jnp.minimum(t, nu[0] - 1)


def _expert_up_call(xs, w_gu, b_gu, layer, tile_expert, n_used, n_tiles):
    n_exp, d, ff2 = w_gu.shape[1:]
    ff = ff2 // 2
    tm = EXPERT_TILE
    sel = (jnp.arange(ff2, dtype=jnp.int32)[:, None] == 2 * jnp.arange(ff, dtype=jnp.int32)[None, :]).astype(BF16)
    grid_spec = pltpu.PrefetchScalarGridSpec(
        num_scalar_prefetch=2,
        grid=(n_tiles,),
        in_specs=[
            pl.BlockSpec((tm, d // 2), lambda t, te, nu: (_used_tile(t, nu), 0)),
            pl.BlockSpec((None, None, d, ff2), lambda t, te, nu: (layer, te[t], 0, 0)),
            pl.BlockSpec((None, None, 1, ff2), lambda t, te, nu: (layer, te[t], 0, 0)),
            pl.BlockSpec((ff2, ff), lambda t, te, nu: (0, 0)),
        ],
        out_specs=pl.BlockSpec((tm, ff), lambda t, te, nu: (t, 0)),
        scratch_shapes=[pltpu.VMEM((d, ff2), BF16)],
    )
    return pl.pallas_call(
        _expert_up_body,
        grid_spec=grid_spec,
        out_shape=jax.ShapeDtypeStruct((n_tiles * tm, ff), BF16),
        compiler_params=_params(("arbitrary",)),
        name="expert_up",
    )(tile_expert, n_used, xs, w_gu, b_gu.reshape(b_gu.shape[0], n_exp, 1, ff2), sel)


def _expert_down_body(te_ref, nu_ref, a_ref, wd_ref, bd_ref, y_ref, wb_ref):
    t = pl.program_id(0)

    @pl.when(t < nu_ref[0])
    def _():
        @pl.when(_expert_changed(te_ref, t))
        def _():
            wb_ref[...] = wd_ref[...].astype(BF16)
        y = jnp.dot(a_ref[...], wb_ref[...], preferred_element_type=F32) + bd_ref[...]
        y_ref[...] = _pack_halves(y)

    @pl.when(t >= nu_ref[0])
    def _():
        y_ref[...] = jnp.zeros(y_ref.shape, y_ref.dtype)


def _expert_down_call(act, w_down, b_down, layer, tile_expert, n_used, n_tiles):
    n_exp, ff, d = w_down.shape[1:]
    tm = EXPERT_TILE
    grid_spec = pltpu.PrefetchScalarGridSpec(
        num_scalar_prefetch=2,
        grid=(n_tiles,),
        in_specs=[
            pl.BlockSpec((tm, ff), lambda t, te, nu: (t, 0)),
            pl.BlockSpec((None, None, ff, d), lambda t, te, nu: (layer, te[t], 0, 0)),
            pl.BlockSpec((None, None, 1, d), lambda t, te, nu: (layer, te[t], 0, 0)),
        ],
        out_specs=pl.BlockSpec((tm, d // 2), lambda t, te, nu: (t, 0)),
        scratch_shapes=[pltpu.VMEM((ff, d), BF16)],
    )
    return pl.pallas_call(
        _expert_down_body,
        grid_spec=grid_spec,
        out_shape=jax.ShapeDtypeStruct((n_tiles * tm, d // 2), jnp.int32),
        compiler_params=_params(("arbitrary",)),
        name="expert_down",
    )(tile_expert, n_used, act, w_down, b_down.reshape(b_down.shape[0], n_exp, 1, d))


def _combine_body(pos_ref, y_hbm, w_ref, x_ref, g_ref, *refs, tc, post):
    if post == "next":
        ng_ref, nsh_ref, nsc_ref, o_ref, h_ref, ybuf, sem = refs
    elif post == "final":
        ng_ref, o_ref, ybuf, sem = refs
    else:
        o_ref, ybuf, sem = refs
    t = pl.program_id(0)
    slot = lax.rem(t, 2)

    def gather(tile, s):
        base = tile * (tc * TOP_K)

        def body(r, carry):
            for k in range(TOP_K):
                row = pos_ref[base + r * TOP_K + k]
                pltpu.async_copy(y_hbm.at[pl.ds(row, 1), :], ybuf.at[s, k, pl.ds(r, 1), :], sem.at[s],
                                 priority=k % 2)
            return carry
        lax.fori_loop(0, tc, body, 0, unroll=2)

    @pl.when(t == 0)
    def _():
        gather(0, 0)

    @pl.when(t + 1 < pl.num_programs(0))
    def _():
        gather(t + 1, 1 - slot)

    for k in range(TOP_K):
        pltpu.make_async_copy(y_hbm.at[pl.ds(0, tc), :], ybuf.at[slot, k], sem.at[slot]).wait()
    half = ybuf.shape[-1]
    ch = COMBINE_CHUNK
    w = w_ref[...]
    wk = [jnp.broadcast_to(w[:, k:k + 1], (tc, LANES)) for k in range(TOP_K)]
    sq = jnp.zeros((tc, 1), F32)
    for c0 in range(0, half, ch):
        acc_lo = acc_hi = None
        for k in range(TOP_K):
            lo, hi = _unpack_halves(ybuf[slot, k, :, c0:c0 + ch])
            wt = jnp.tile(wk[k], (1, ch // LANES))
            acc_lo = wt * lo if k == 0 else acc_lo + wt * lo
            acc_hi = wt * hi if k == 0 else acc_hi + wt * hi
        for acc, c in ((acc_lo, c0), (acc_hi, half + c0)):
            new = x_ref[:, c:c + ch] + g_ref[:, c:c + ch] * acc
            o_ref[:, c:c + ch] = new
            if post is not None:
                sq = sq + jnp.sum(new * new, axis=-1, keepdims=True)
    if post is None:
        return
    inv = lax.rsqrt(sq * (1.0 / (2 * half)) + NORM_EPS)
    for c in range(0, 2 * half, ch):
        norm = o_ref[:, c:c + ch] * inv * ng_ref[:, c:c + ch]
        if post == "final":
            o_ref[:, c:c + ch] = norm
        else:
            h_ref[:, c:c + ch] = (norm * (1.0 + nsc_ref[:, c:c + ch]) + nsh_ref[:, c:c + ch]).astype(h_ref.dtype)


def _combine_call(y_sorted, slot, top_w, xa, modl, gate_idx, rows, seq, n_batch, post=None, post_args=()):
    d = xa.shape[1]
    tc = COMBINE_TILE
    assert rows % tc == 0 and seq % tc == 0
    tpb = seq // tc

    def seg_row(which):
        return lambda t, pos: (jnp.minimum(t // tpb, n_batch) * 6 + which, 0, 0)

    row_spec = pl.BlockSpec((tc, d), lambda t, pos: (t, 0))
    in_specs = [
        pl.BlockSpec(memory_space=pl.ANY),
        pl.BlockSpec((tc, LANES), lambda t, pos: (t, 0)),
        row_spec,
        pl.BlockSpec((None, 1, d), seg_row(gate_idx)),
    ]
    args = [slot, y_sorted, top_w, xa, modl]
    out_specs = row_spec
    out_shape = jax.ShapeDtypeStruct((rows, d), F32)
    if post is not None:
        in_specs.append(pl.BlockSpec((1, d), lambda t, pos: (0, 0)))
        args.append(post_args[0].reshape(1, d))
    if post == "next":
        in_specs += [pl.BlockSpec((None, 1, d), seg_row(0)), pl.BlockSpec((None, 1, d), seg_row(1))]
        args += [post_args[1], post_args[1]]
        out_specs = [row_spec, row_spec]
        out_shape = [out_shape, jax.ShapeDtypeStruct((rows, d), BF16)]
    grid_spec = pltpu.PrefetchScalarGridSpec(
        num_scalar_prefetch=1,
        grid=(rows // tc,),
        in_specs=in_specs,
        out_specs=out_specs,
        scratch_shapes=[pltpu.VMEM((2, TOP_K, tc, d // 2), jnp.int32), pltpu.SemaphoreType.DMA((2,))],
    )
    return pl.pallas_call(
        functools.partial(_combine_body, tc=tc, post=post),
        grid_spec=grid_spec,
        out_shape=out_shape,
        compiler_params=_params(("arbitrary",)),
        name="expert_combine",
    )(*args)


def _axial_angles(seq, head_dim):
    rows = seq // GRID_W
    row = jnp.repeat(jnp.arange(rows, dtype=F32), GRID_W)
    col = jnp.tile(jnp.arange(GRID_W, dtype=F32), rows)
    n_freq = head_dim // 4
    inv_freq = ROPE_BASE ** (-jnp.arange(n_freq, dtype=F32) / n_freq)
    return jnp.concatenate([row[:, None] * inv_freq, col[:, None] * inv_freq], axis=-1)


def _rope_tables(seq, ret_head_dim):
    ang_a = _axial_angles(seq, ATT_HEAD_DIM)
    reps = LANES // ATT_HEAD_DIM
    cos_a = jnp.tile(jnp.concatenate([jnp.cos(ang_a), jnp.cos(ang_a)], axis=-1), (1, reps))
    sin_a = jnp.tile(jnp.concatenate([-jnp.sin(ang_a), jnp.sin(ang_a)], axis=-1), (1, reps))
    ang_r = _axial_angles(seq, ret_head_dim)
    return (cos_a, sin_a), (jnp.cos(ang_r), jnp.sin(ang_r))


def kernel(x, c, ctx, c_ctx, ada_w, ada_b, norm_mix_g, norm_ffn_g, w_in, att_sink, ret_decay_fwd,
           ret_decay_bwd, w_branch_att, w_branch_ret, branch_gate_b, w_out, router_w, router_b,
           exp_w_gu, exp_b_gu, exp_w_down, exp_b_down, final_norm_g):
    n_batch, seq, d = x.shape
    n_ctx = ctx.shape[1]
    depth = ada_w.shape[0]
    n_exp, ff = exp_w_down.shape[1], exp_w_down.shape[2]
    att_width = w_branch_att.shape[1]
    ret_width = w_branch_ret.shape[1]
    kv_width = att_width // ATT_GROUP
    sizes = (("k_att", kv_width), ("v_att", kv_width), ("k_ret", ret_width), ("v_ret", ret_width),
             ("q_att", att_width), ("q_ret", ret_width), ("g_ret", ret_width),
             ("gate_att", d), ("gate_ret", d))
    cols, off = {}, 0
    for name, width in sizes:
        cols[name] = off
        off += width
    in_cols = off
    assert in_cols == w_in.shape[2]

    lat_rows = n_batch * seq
    all_rows = lat_rows + n_batch * n_ctx
    xa = jnp.concatenate([x.reshape(lat_rows, d), ctx.reshape(n_batch * n_ctx, d)], axis=0)

    cond = jnp.zeros((16, d), F32).at[:n_batch].set(c).at[n_batch].set(c_ctx)
    mod = _mod_call(cond, ada_w, ada_b)
    rope_att, rope_ret = _rope_tables(seq, ret_width // RET_HEADS)

    modls = [mod[layer, :n_batch + 1].reshape((n_batch + 1) * 6, 1, d) for layer in range(depth)]
    h = _norm_call(xa, norm_mix_g[0], modls[0], all_rows, seq, n_batch, 0, 1)
    out = None
    for layer in range(depth):
        last = layer == depth - 1
        rows = lat_rows if last else all_rows
        modl = modls[layer]
        log_gamma = jnp.stack([jax.nn.log_sigmoid(ret_decay_fwd[layer].astype(F32)),
                               jax.nn.log_sigmoid(ret_decay_bwd[layer].astype(F32))])

        proj = _inproj_call(h, w_in, layer, all_rows, 0, in_cols)

        ya = _attn_call(proj, att_sink[layer], rope_att, cols, n_batch, seq, n_ctx, att_width, True)
        yr = _ret_call(proj, log_gamma, rope_ret, cols, n_batch, seq, n_ctx, ret_width, True)
        if not last:
            ya_c = _attn_call(proj, att_sink[layer], None, cols, n_batch, seq, n_ctx, att_width, False)
            yr_c = _ret_call(proj, log_gamma, None, cols, n_batch, seq, n_ctx, ret_width, False)
            ya = jnp.concatenate([ya, ya_c], axis=0)
            yr = jnp.concatenate([yr, yr_c], axis=0)

        z = _merge_gate_call(ya, yr, proj, w_branch_att, w_branch_ret, branch_gate_b, layer, rows, cols)
        xa = _resid_matmul_call(z, w_out, layer, xa, modl, 2, rows, seq, n_batch)

        h2, top_idx, top_w = _norm_call(xa, norm_ffn_g[layer], modl, rows, seq, n_batch, 3, 4,
                                        router=(router_w[layer], router_b[layer]))
        n_tiles = -(-rows * TOP_K // EXPERT_TILE) + n_exp
        tile_expert, n_used, slot, last_tile = _routing(top_idx[:, :TOP_K], n_exp, EXPERT_TILE, n_tiles)
        xs = _dispatch_call(h2, slot, last_tile, n_used, rows, n_tiles, n_exp)
        act = _expert_up_call(xs, exp_w_gu, exp_b_gu, layer, tile_expert, n_used, n_tiles)
        y_sorted = _expert_down_call(act, exp_w_down, exp_b_down, layer, tile_expert, n_used, n_tiles)
        if last:
            out = _combine_call(y_sorted, slot, top_w, xa, modl, 5, rows, seq, n_batch,
                                post="final", post_args=(final_norm_g,))
        else:
            xa, h = _combine_call(y_sorted, slot, top_w, xa, modl, 5, rows, seq, n_batch,
                                  post="next", post_args=(norm_mix_g[layer + 1], modls[layer + 1]))
    return out.reshape(n_batch, seq, d)
```

```python
import functools

import jax
import jax.numpy as jnp
from jax import lax
from jax.experimental import pallas as pl
from jax.experimental.pallas import tpu as pltpu

F32 = jnp.float32
BF16 = jnp.bfloat16

GRID_W = 64
ATT_HEAD_DIM = 64
ATT_GROUP = 4
ATT_WINDOW = 128
ATT_BLOCK = 128
RET_HEADS = 8
RET_CHUNK = 128
RET_UNROLL_BWD = 4
RET_UNROLL_FWD = 8
TOP_K = 4
SWIGLU_ALPHA = 1.702
SWIGLU_LIMIT = 7.0
ROPE_BASE = 10000.0
NORM_EPS = 1e-6
GN_EPS = 1e-5
NEG_INF = -1e30
LOG2_E = 1.4426950408889634

LANES = 128
VMEM_LIMIT = 56 * 1024 * 1024
EXPERT_VMEM_LIMIT = 60 * 1024 * 1024


def _params(sem, vmem=VMEM_LIMIT):
    return pltpu.CompilerParams(dimension_semantics=sem, vmem_limit_bytes=vmem)


def _tile(n, cap, mult):
    for t in range(min(cap, n), 0, -1):
        if n % t == 0 and t % mult == 0:
            return t
    raise ValueError(f"no tile for {n} (cap {cap}, multiple of {mult})")


def _silu(v):
    return v * jax.nn.sigmoid(v)


def _mod_body(c_ref, w_ref, b_ref, o_ref):
    s = _silu(c_ref[...]).astype(BF16)
    o_ref[...] = jnp.dot(s, w_ref[...].astype(BF16), preferred_element_type=F32) + b_ref[...]


def _mod_call(cond, ada_w, ada_b):
    depth, d, n = ada_w.shape
    rows = cond.shape[0]
    tn = _tile(n, 512, LANES)
    return pl.pallas_call(
        _mod_body,
        grid=(depth, n // tn),
        in_specs=[
            pl.BlockSpec((rows, d), lambda l, j: (0, 0)),
            pl.BlockSpec((None, d, tn), lambda l, j: (l, 0, j)),
            pl.BlockSpec((None, 1, tn), lambda l, j: (l, 0, j)),
        ],
        out_specs=pl.BlockSpec((None, rows, tn), lambda l, j: (l, 0, j)),
        out_shape=jax.ShapeDtypeStruct((depth, rows, n), F32),
        compiler_params=_params(("arbitrary", "arbitrary")),
        name="adaln_mod",
    )(cond, ada_w, ada_b.reshape(depth, 1, n))


def _rms(x, g):
    ms = jnp.mean(x * x, axis=-1, keepdims=True)
    return x * lax.rsqrt(ms + NORM_EPS) * g


def _norm_mod_body(x_ref, g_ref, sh_ref, sc_ref, h_ref):
    h = _rms(x_ref[...], g_ref[...]) * (1.0 + sc_ref[...]) + sh_ref[...]
    h_ref[...] = h.astype(h_ref.dtype)


def _pack_halves(v):
    w = v.shape[1] // 2
    bits = lax.bitcast_convert_type(v.astype(BF16).astype(F32), jnp.int32)
    return lax.shift_right_logical(bits[:, :w], 16) | (bits[:, w:] & jnp.int32(-65536))


def _unpack_halves(p):
    lo = lax.bitcast_convert_type(lax.shift_left(p, 16), F32)
    hi = lax.bitcast_convert_type(p & jnp.int32(-65536), F32)
    return lo, hi


def _split_bf16(v):
    hi = v.astype(BF16)
    lo = (v - hi.astype(F32)).astype(BF16)
    return hi, lo


def _norm_router_body(x_ref, g_ref, sh_ref, sc_ref, rw_ref, rb_ref, h_ref, idx_ref, wt_ref):
    h = _rms(x_ref[...], g_ref[...]) * (1.0 + sc_ref[...]) + sh_ref[...]
    h_ref[...] = _pack_halves(h)
    h_hi, h_lo = _split_bf16(h)
    w_hi, w_lo = _split_bf16(rw_ref[...])
    logits = (jnp.dot(h_hi, w_hi, preferred_element_type=F32)
              + jnp.dot(h_hi, w_lo, preferred_element_type=F32)
              + jnp.dot(h_lo, w_hi, preferred_element_type=F32)) + rb_ref[...]
    n_exp = logits.shape[-1]
    lane = lax.broadcasted_iota(jnp.int32, logits.shape, 1).astype(F32)
    work = logits
    out_lane = lax.broadcasted_iota(jnp.int32, idx_ref.shape, 1)
    idx = jnp.zeros(idx_ref.shape, F32)
    ex = jnp.zeros(wt_ref.shape, F32)
    top = None
    total = None
    for k in range(TOP_K):
        m = jnp.max(work, axis=-1, keepdims=True)
        first = jnp.min(jnp.where(work == m, lane, float(n_exp)), axis=-1, keepdims=True)
        work = jnp.where(lane == first, -jnp.inf, work)
        if k == 0:
            top = m
        e = jnp.exp(m - top)
        total = e if k == 0 else total + e
        idx = jnp.where(out_lane == k, first, idx)
        ex = jnp.where(out_lane == k, e, ex)
    idx_ref[...] = idx.astype(jnp.int32)
    wt_ref[...] = ex / total


def _seg_map(tiles_per_batch, n_batch, which):
    def index(i):
        return (jnp.minimum(i // tiles_per_batch, n_batch) * 6 + which, 0, 0)
    return index


def _norm_call(xa, g, modl, rows, seq, n_batch, shift_idx, scale_idx, router=None):
    d = xa.shape[1]
    tr = 256
    assert rows % tr == 0 and seq % tr == 0
    tpb = seq // tr
    row_spec = pl.BlockSpec((tr, d), lambda i: (i, 0))
    in_specs = [
        row_spec,
        pl.BlockSpec((1, d), lambda i: (0, 0)),
        pl.BlockSpec((None, 1, d), _seg_map(tpb, n_batch, shift_idx)),
        pl.BlockSpec((None, 1, d), _seg_map(tpb, n_batch, scale_idx)),
    ]
    args = [xa, g.reshape(1, d), modl, modl]
    if router is None:
        return pl.pallas_call(
            _norm_mod_body, grid=(rows // tr,), in_specs=in_specs, out_specs=row_spec,
            out_shape=jax.ShapeDtypeStruct((rows, d), BF16),
            compiler_params=_params(("arbitrary",)), name="norm_mod",
        )(*args)
    rw, rb = router
    n_exp = rw.shape[1]
    in_specs += [pl.BlockSpec((d, n_exp), lambda i: (0, 0)), pl.BlockSpec((1, n_exp), lambda i: (0, 0))]
    args += [rw, rb.reshape(1, n_exp)]
    return pl.pallas_call(
        _norm_router_body, grid=(rows // tr,), in_specs=in_specs,
        out_specs=[pl.BlockSpec((tr, d // 2), lambda i: (i, 0)),
                   pl.BlockSpec((tr, LANES), lambda i: (i, 0)), pl.BlockSpec((tr, LANES), lambda i: (i, 0))],
        out_shape=[jax.ShapeDtypeStruct((rows, d // 2), jnp.int32), jax.ShapeDtypeStruct((rows, LANES), jnp.int32),
                   jax.ShapeDtypeStruct((rows, LANES), F32)],
        compiler_params=_params(("arbitrary",)), name="norm_router",
    )(*args)


def _inproj_body(a_ref, w_ref, o_ref, wb_ref):
    @pl.when(pl.program_id(1) == 0)
    def _():
        wb_ref[...] = w_ref[...].astype(BF16)
    o_ref[...] = jnp.dot(a_ref[...], wb_ref[...], preferred_element_type=F32).astype(o_ref.dtype)


def _inproj_call(h, w_in, layer, rows, col_lo, col_hi):
    d = h.shape[1]
    tn = 512
    assert col_lo % tn == 0 and col_hi % tn == 0
    tm = _tile(rows, 1152, 16)
    j0 = col_lo // tn
    return pl.pallas_call(
        _inproj_body,
        grid=((col_hi - col_lo) // tn, rows // tm),
        in_specs=[
            pl.BlockSpec((tm, d), lambda j, i: (i, 0)),
            pl.BlockSpec((None, d, tn), lambda j, i: (layer, 0, j0 + j)),
        ],
        out_specs=pl.BlockSpec((tm, tn), lambda j, i: (i, j)),
        out_shape=jax.ShapeDtypeStruct((rows, col_hi - col_lo), BF16),
        scratch_shapes=[pltpu.VMEM((d, tn), BF16)],
        compiler_params=_params(("arbitrary", "arbitrary")),
        name="in_proj",
    )(h, w_in)


def _attn_body(*refs, local, scale):
    if local:
        (sink_ref, q_ref, kp_ref, kc_ref, kn_ref, vp_ref, vc_ref, vn_ref, kx_ref, vx_ref,
         cp_ref, cc_ref, cn_ref, sp_ref, sc_ref, sn_ref, bias_ref, o_ref) = refs
    else:
        sink_ref, q_ref, kx_ref, vx_ref, o_ref = refs
    pair = pl.program_id(1)
    n = pl.program_id(2)
    blk = ATT_BLOCK
    half = ATT_HEAD_DIM // 2

    lane_q = lax.broadcasted_iota(jnp.int32, (blk, LANES), 1)
    first_half = (lane_q % ATT_HEAD_DIM) < half

    def rope(v, c_ref, s_ref):
        rot = jnp.where(first_half, pltpu.roll(v, LANES - half, 1), pltpu.roll(v, half, 1))
        return v * c_ref[...] + rot * s_ref[...]

    if local:
        keys = jnp.concatenate([
            rope(kp_ref[...].astype(F32), cp_ref, sp_ref),
            rope(kc_ref[...].astype(F32), cc_ref, sc_ref),
            rope(kn_ref[...].astype(F32), cn_ref, sn_ref),
            kx_ref[...].astype(F32)], axis=0)
        vals = jnp.concatenate([vp_ref[...], vc_ref[...], vn_ref[...], vx_ref[...]], axis=0).astype(F32)
    else:
        keys = kx_ref[...].astype(F32)
        vals = vx_ref[...].astype(F32)
    nk = keys.shape[0]

    lane_k = lax.broadcasted_iota(jnp.int32, (nk, LANES), 1)
    low = lane_k < ATT_HEAD_DIM

    def placed(v):
        swapped = pltpu.roll(v, ATT_HEAD_DIM, 1)
        return [[jnp.where(low, v, 0.0).astype(BF16), jnp.where(low, 0.0, swapped).astype(BF16)],
                [jnp.where(low, swapped, 0.0).astype(BF16), jnp.where(low, 0.0, v).astype(BF16)]]

    k_var = placed(keys)
    v_var = placed(vals)

    top_rows = lax.broadcasted_iota(jnp.int32, (2 * blk, 1), 0) < blk

    n_slab = q_ref.shape[1] // LANES
    q_slabs = []
    for c in range(n_slab):
        qc = q_ref[:, c * LANES:(c + 1) * LANES].astype(F32)
        if local:
            qc = rope(qc, cc_ref, sc_ref)
        q_slabs.append((qc * (scale * LOG2_E)).astype(BF16))

    out = [jnp.zeros((blk, LANES), F32) for _ in range(n_slab)]
    combos = [(j, hh) for j in range(2) for hh in range(2)]
    scores = []
    for j, hh in combos:
        lhs = jnp.concatenate([q_slabs[2 * j], q_slabs[2 * j + 1]], axis=0)
        s = lax.dot_general(lhs, k_var[j][hh], (((1,), (1,)), ((), ())),
                            preferred_element_type=F32)
        if local:
            s = s + bias_ref[...]
        scores.append(s)
    probs = []
    for (j, hh), s in zip(combos, scores):
        head0 = pair * (2 * ATT_GROUP) + ATT_GROUP * j + hh
        sink = jnp.where(top_rows, sink_ref[head0], sink_ref[head0 + 2]) * LOG2_E
        m = jnp.maximum(jnp.max(s, axis=-1, keepdims=True), sink)
        e = jnp.exp2(s - m)
        denom = jnp.sum(e, axis=-1, keepdims=True) + jnp.exp2(sink - m)
        probs.append((e.astype(BF16), 1.0 / denom))
    for (j, hh), (e, inv) in zip(combos, probs):
        o = jnp.dot(e, v_var[j][hh], preferred_element_type=F32) * inv
        out[2 * j] = out[2 * j] + o[:blk]
        out[2 * j + 1] = out[2 * j + 1] + o[blk:]
    for c in range(n_slab):
        o_ref[:, c * LANES:(c + 1) * LANES] = out[c].astype(o_ref.dtype)


def _attn_call(proj, sink, rope_tabs, cols, n_batch, seq, n_ctx, att_width, local):
    blk = ATT_BLOCK
    nb = seq // blk
    pair_w = 2 * ATT_HEAD_DIM
    q_w = pair_w * ATT_GROUP
    n_pair = att_width // q_w
    k0 = cols["k_att"] // pair_w
    v0 = cols["v_att"] // pair_w
    q0 = cols["q_att"] // q_w
    ctx_row0 = (n_batch * seq) // n_ctx
    kx_spec = pl.BlockSpec((n_ctx, pair_w), lambda b, p, n: (ctx_row0 + b, k0 + p))
    vx_spec = pl.BlockSpec((n_ctx, pair_w), lambda b, p, n: (ctx_row0 + b, v0 + p))
    sink_spec = pl.BlockSpec(memory_space=pltpu.SMEM)
    scale = ATT_HEAD_DIM ** -0.5
    if local:
        cos_t, sin_t = rope_tabs

        def prev(n):
            return jnp.maximum(n - 1, 0)

        def nxt(n):
            return jnp.minimum(n + 1, nb - 1)

        def kv(c0, f):
            return pl.BlockSpec((blk, pair_w), lambda b, p, n: (b * nb + f(n), c0 + p))

        def tab(f):
            return pl.BlockSpec((blk, LANES), lambda b, p, n: (f(n), 0))

        ident = lambda n: n
        assert nb >= 2
        r = jnp.arange(2 * blk, dtype=jnp.int32)[:, None] % blk
        c = jnp.arange(3 * blk + n_ctx, dtype=jnp.int32)[None, :]
        band = (jnp.abs(c - blk - r) <= ATT_WINDOW) | (c >= 3 * blk)
        edge = jnp.stack([band & (c >= blk), band, band & ((c < 2 * blk) | (c >= 3 * blk))])
        bias = jnp.where(edge, 0.0, NEG_INF).astype(F32)
        bias_spec = pl.BlockSpec((None, 2 * blk, 3 * blk + n_ctx),
                                 lambda b, p, n: ((n > 0).astype(jnp.int32) + (n == nb - 1).astype(jnp.int32), 0, 0))
        in_specs = [sink_spec,
                    pl.BlockSpec((blk, q_w), lambda b, p, n: (b * nb + n, q0 + p)),
                    kv(k0, prev), kv(k0, ident), kv(k0, nxt),
                    kv(v0, prev), kv(v0, ident), kv(v0, nxt),
                    kx_spec, vx_spec,
                    tab(prev), tab(ident), tab(nxt), tab(prev), tab(ident), tab(nxt), bias_spec]
        args = [sink, proj] + [proj] * 6 + [proj, proj] + [cos_t] * 3 + [sin_t] * 3 + [bias]
        grid = (n_batch, n_pair, nb)
        rows = n_batch * seq
        out_spec = pl.BlockSpec((blk, q_w), lambda b, p, n: (b * nb + n, p))
    else:
        nqb = n_ctx // blk
        qrow0 = (n_batch * seq) // blk
        in_specs = [sink_spec,
                    pl.BlockSpec((blk, q_w), lambda b, p, n: (qrow0 + b * nqb + n, q0 + p)),
                    kx_spec, vx_spec]
        args = [sink, proj, proj, proj]
        grid = (n_batch, n_pair, nqb)
        rows = n_batch * n_ctx
        out_spec = pl.BlockSpec((blk, q_w), lambda b, p, n: (b * nqb + n, p))
    return pl.pallas_call(
        functools.partial(_attn_body, local=local, scale=scale),
        grid=grid, in_specs=in_specs, out_specs=out_spec,
        out_shape=jax.ShapeDtypeStruct((rows, att_width), BF16),
        compiler_params=_params(("arbitrary",) * 3),
        name="window_attn" if local else "ctx_attn",
    )(*args)


def _dot_tn(a, b):
    return lax.dot_general(a, b, (((0,), (0,)), ((), ())), preferred_element_type=F32)


def _ret_body(*refs, latent, n_chunks, n_ctx, k_scale):
    if latent:
        (lg_ref, q_ref, k_ref, v_ref, g_ref, kx_ref, vx_ref, cos_ref, sin_ref,
         o_ref, qs_ref, ks_ref, cb_ref, st_ref, sf_ref) = refs
    else:
        lg_ref, q_ref, k_ref, v_ref, g_ref, o_ref, qs_ref, ks_ref, cb_ref, st_ref, sf_ref = refs
    head = pl.program_id(1)
    lg_f = lg_ref[0, head]
    lg_b = lg_ref[1, head]
    c = RET_CHUNK
    hd = q_ref.shape[1]
    hh = hd // 2
    idx = lax.broadcasted_iota(jnp.int32, (c, 1), 0).astype(F32)
    qdec_f = jnp.exp(lg_f * (idx + 1.0))
    kdec_f = jnp.exp(lg_f * (c - 1.0 - idx))
    qdec_b = jnp.exp(lg_b * (c - idx))
    kdec_b = jnp.exp(lg_b * idx)
    one = jnp.ones((1, 1), F32)
    cdec_f = jnp.exp(lg_f * c * one)
    cdec_b = jnp.exp(lg_b * c * one)
    rel = (lax.broadcasted_iota(jnp.int32, (c, c), 0) - lax.broadcasted_iota(jnp.int32, (c, c), 1)).astype(F32)
    dmat = (jnp.where(rel >= 0, jnp.exp(lg_f * jnp.maximum(rel, 0.0)), 0.0)
            + jnp.where(rel <= 0, jnp.exp(lg_b * jnp.maximum(-rel, 0.0)), 0.0))

    def rope(v, rows):
        v1 = v[:, :hh]
        v2 = v[:, hh:]
        cs = cos_ref[rows, :]
        sn = sin_ref[rows, :]
        return jnp.concatenate([v1 * cs - v2 * sn, v2 * cs + v1 * sn], axis=1)

    if latent:
        t = lax.broadcasted_iota(jnp.int32, (n_ctx, 1), 0).astype(F32)
        kx = kx_ref[...].astype(F32) * k_scale
        sf_ref[...] = _dot_tn((kx * jnp.exp(lg_f * (n_ctx - 1.0 - t))).astype(BF16), vx_ref[...])
        st_ref[...] = _dot_tn((kx * jnp.exp(lg_b * t)).astype(BF16), vx_ref[...])
    else:
        sf_ref[...] = jnp.zeros((hd, hd), F32)
        st_ref[...] = jnp.zeros((hd, hd), F32)


    def bwd(step, carry):
        rows = pl.ds(pl.multiple_of((n_chunks - 1 - step) * c, c), c)
        q = q_ref[rows, :].astype(F32)
        k = k_ref[rows, :].astype(F32)
        if latent:
            q = rope(q, rows)
            k = rope(k, rows)
        k = k * k_scale
        qs_ref[rows, :] = q.astype(BF16)
        ks_ref[rows, :] = k.astype(BF16)
        state = st_ref[...]
        cb_ref[rows, :] = jnp.dot((q * qdec_b).astype(BF16), state.astype(BF16), preferred_element_type=F32)
        st_ref[...] = cdec_b * state + _dot_tn((k * kdec_b).astype(BF16), v_ref[rows, :])
        return carry

    lax.fori_loop(0, n_chunks, bwd, 0, unroll=min(RET_UNROLL_BWD, n_chunks))

    st_ref[...] = sf_ref[...]

    def fwd(step, carry):
        rows = pl.ds(pl.multiple_of(step * c, c), c)
        qb = qs_ref[rows, :]
        kb = ks_ref[rows, :]
        v = v_ref[rows, :]
        scores = lax.dot_general(qb, kb, (((1,), (1,)), ((), ())), preferred_element_type=F32) * dmat
        o = jnp.dot(scores.astype(BF16), v, preferred_element_type=F32)
        state = st_ref[...]
        o = o + jnp.dot((qb.astype(F32) * qdec_f).astype(BF16), state.astype(BF16), preferred_element_type=F32)
        o = o + cb_ref[rows, :]
        st_ref[...] = cdec_f * state + _dot_tn((kb.astype(F32) * kdec_f).astype(BF16), v)
        mu = jnp.mean(o, axis=-1, keepdims=True)
        dev = o - mu
        var = jnp.mean(dev * dev, axis=-1, keepdims=True)
        y = dev * lax.rsqrt(var + GN_EPS)
        o_ref[rows, :] = (_silu(g_ref[rows, :].astype(F32)) * y).astype(o_ref.dtype)
        return carry

    lax.fori_loop(0, n_chunks, fwd, 0, unroll=min(RET_UNROLL_FWD, n_chunks))


def _ret_call(proj, log_gamma, rope_tabs, cols, n_batch, seq, n_ctx, ret_width, latent):
    hd = ret_width // RET_HEADS
    t_len = seq if latent else n_ctx
    assert t_len % RET_CHUNK == 0
    k0, v0, q0, g0 = (cols[name] // hd for name in ("k_ret", "v_ret", "q_ret", "g_ret"))
    ctx_row0 = (n_batch * seq) // n_ctx
    row0 = 0 if latent else ctx_row0

    def seq_spec(c0):
        return pl.BlockSpec((t_len, hd), lambda b, h: (row0 + b, c0 + h))

    in_specs = [pl.BlockSpec(memory_space=pltpu.SMEM),
                seq_spec(q0), seq_spec(k0), seq_spec(v0), seq_spec(g0)]
    args = [log_gamma, proj, proj, proj, proj]
    if latent:
        cos_t, sin_t = rope_tabs
        in_specs += [pl.BlockSpec((n_ctx, hd), lambda b, h: (ctx_row0 + b, k0 + h)),
                     pl.BlockSpec((n_ctx, hd), lambda b, h: (ctx_row0 + b, v0 + h)),
                     pl.BlockSpec((t_len, hd // 2), lambda b, h: (0, 0)),
                     pl.BlockSpec((t_len, hd // 2), lambda b, h: (0, 0))]
        args += [proj, proj, cos_t, sin_t]
    return pl.pallas_call(
        functools.partial(_ret_body, latent=latent, n_chunks=t_len // RET_CHUNK, n_ctx=n_ctx,
                          k_scale=hd ** -0.5),
        grid=(n_batch, RET_HEADS), in_specs=in_specs,
        out_specs=pl.BlockSpec((t_len, hd), lambda b, h: (b, h)),
        out_shape=jax.ShapeDtypeStruct((n_batch * t_len, ret_width), BF16),
        scratch_shapes=[pltpu.VMEM((t_len, hd), BF16), pltpu.VMEM((t_len, hd), BF16),
                        pltpu.VMEM((t_len, hd), F32), pltpu.VMEM((hd, hd), F32), pltpu.VMEM((hd, hd), F32)],
        compiler_params=_params(("arbitrary", "arbitrary")),
        name="retention" if latent else "ctx_retention",
    )(*args)


def _merge_gate_body(ya_ref, yr_ref, ga_ref, gr_ref, wa_ref, wr_ref, ba_ref, br_ref, z_ref, wab_ref, wrb_ref):
    @pl.when(pl.program_id(1) == 0)
    def _():
        wab_ref[...] = wa_ref[...].astype(BF16)
        wrb_ref[...] = wr_ref[...].astype(BF16)
    a = jnp.dot(ya_ref[...], wab_ref[...], preferred_element_type=F32)
    r = jnp.dot(yr_ref[...], wrb_ref[...], preferred_element_type=F32)
    ga = jax.nn.sigmoid(ga_ref[...].astype(F32) + ba_ref[...])
    gr = jax.nn.sigmoid(gr_ref[...].astype(F32) + br_ref[...])
    z_ref[...] = (ga * a + gr * r).astype(z_ref.dtype)


def _merge_gate_call(ya, yr, proj, w_att, w_ret, gate_b, layer, rows, cols):
    d = w_att.shape[2]
    wa_rows, wr_rows = w_att.shape[1], w_ret.shape[1]
    tn = 512
    tm = _tile(rows, 1024, 16)
    ga0 = cols["gate_att"] // tn
    gr0 = cols["gate_ret"] // tn
    gb = gate_b.reshape(gate_b.shape[0], 1, 2 * d)
    return pl.pallas_call(
        _merge_gate_body,
        grid=(d // tn, rows // tm),
        in_specs=[
            pl.BlockSpec((tm, wa_rows), lambda j, i: (i, 0)),
            pl.BlockSpec((tm, wr_rows), lambda j, i: (i, 0)),
            pl.BlockSpec((tm, tn), lambda j, i: (i, ga0 + j)),
            pl.BlockSpec((tm, tn), lambda j, i: (i, gr0 + j)),
            pl.BlockSpec((None, wa_rows, tn), lambda j, i: (layer, 0, j)),
            pl.BlockSpec((None, wr_rows, tn), lambda j, i: (layer, 0, j)),
            pl.BlockSpec((None, 1, tn), lambda j, i: (layer, 0, j)),
            pl.BlockSpec((None, 1, tn), lambda j, i: (layer, 0, d // tn + j)),
        ],
        out_specs=pl.BlockSpec((tm, tn), lambda j, i: (i, j)),
        out_shape=jax.ShapeDtypeStruct((rows, d), BF16),
        scratch_shapes=[pltpu.VMEM((wa_rows, tn), BF16), pltpu.VMEM((wr_rows, tn), BF16)],
        compiler_params=_params(("arbitrary", "arbitrary")),
        name="merge_gate",
    )(ya, yr, proj, proj, w_att, w_ret, gb, gb)


def _resid_matmul_body(z_ref, w_ref, x_ref, g_ref, o_ref, wb_ref):
    @pl.when(pl.program_id(1) == 0)
    def _():
        wb_ref[...] = w_ref[...].astype(BF16)
    y = jnp.dot(z_ref[...], wb_ref[...], preferred_element_type=F32)
    o_ref[...] = x_ref[...] + g_ref[...] * y


def _resid_matmul_call(z, w, layer, xa, modl, gate_idx, rows, seq, n_batch):
    k, d = w.shape[1], w.shape[2]
    tn = 512
    tm = 512
    assert rows % tm == 0 and seq % tm == 0
    tpb = seq // tm

    def gate_map(j, i):
        return (jnp.minimum(i // tpb, n_batch) * 6 + gate_idx, 0, j)

    return pl.pallas_call(
        _resid_matmul_body,
        grid=(d // tn, rows // tm),
        in_specs=[
            pl.BlockSpec((tm, k), lambda j, i: (i, 0)),
            pl.BlockSpec((None, k, tn), lambda j, i: (layer, 0, j)),
            pl.BlockSpec((tm, tn), lambda j, i: (i, j)),
            pl.BlockSpec((None, 1, tn), gate_map),
        ],
        out_specs=pl.BlockSpec((tm, tn), lambda j, i: (i, j)),
        out_shape=jax.ShapeDtypeStruct((rows, d), F32),
        scratch_shapes=[pltpu.VMEM((k, tn), BF16)],
        compiler_params=_params(("arbitrary", "arbitrary")),
        name="resid_matmul",
    )(z, w, xa, modl)


EXPERT_TILE = 256
COMBINE_TILE = 128
DISPATCH_TILE = 256
COMBINE_CHUNK = 256


def _routing(top_idx, n_exp, tm, n_tiles):
    n_assign = top_idx.size
    e_flat = top_idx.reshape(n_assign)
    onehot = (e_flat[:, None] == jnp.arange(n_exp, dtype=jnp.int32)[None, :]).astype(jnp.int32)
    running = jnp.cumsum(onehot, axis=0)
    counts = running[-1]
    group = (counts + tm - 1) // tm * tm
    group_end = jnp.cumsum(group)
    slot = jnp.sum(onehot * (running - 1 + (group_end - group)[None, :]), axis=1).astype(jnp.int32)
    n_used = (group_end[-1] // tm).astype(jnp.int32).reshape(1)
    tile_start = jnp.arange(n_tiles, dtype=jnp.int32) * tm
    tile_expert = jnp.sum((group_end[None, :] <= tile_start[:, None]).astype(jnp.int32), axis=1)
    tile_expert = jnp.minimum(tile_expert, n_exp - 1).astype(jnp.int32)
    last_tile = jnp.stack([jnp.maximum(group_end - tm, 0), (group > 0).astype(jnp.int32)], axis=1)
    ids = jnp.arange(n_exp, dtype=jnp.int32)
    later = (ids[None, :] > ids[:, None]) & (group[None, :] > 0)
    next_of = jnp.min(jnp.where(later, ids[None, :], n_exp), axis=1)
    next_of = jnp.where(next_of == n_exp, -1, next_of)
    next_expert = jnp.sum((tile_expert[:, None] == ids[None, :]) * next_of[None, :], axis=1).astype(jnp.int32)
    return tile_expert, n_used, slot, last_tile.reshape(2 * n_exp).astype(jnp.int32), next_expert


def _expert_changed(te_ref, t):
    return jnp.logical_or(t == 0, te_ref[t] != te_ref[jnp.maximum(t - 1, 0)])


def _dispatch_body(slot_ref, last_ref, nu_ref, h_ref, xs_hbm, zero_ref, sem, zsem, *, tr, tm, n_exp, n_tiles):
    t = pl.program_id(0)

    @pl.when(t == 0)
    def _():
        zero_ref[...] = jnp.zeros(zero_ref.shape, zero_ref.dtype)

        def zero_copy(start):
            return pltpu.make_async_copy(zero_ref, xs_hbm.at[pl.ds(pl.multiple_of(start, tm), tm), :], zsem)

        def start_one(e, carry):
            @pl.when(last_ref[2 * e + 1] == 1)
            def _():
                zero_copy(last_ref[2 * e]).start()
            return carry

        def wait_one(e, carry):
            @pl.when(last_ref[2 * e + 1] == 1)
            def _():
                zero_copy(last_ref[2 * e]).wait()
            return carry

        def start_tail(i, carry):
            zero_copy(i * tm).start()
            return carry

        def wait_tail(i, carry):
            zero_copy(i * tm).wait()
            return carry

        lax.fori_loop(0, n_exp, start_one, 0)
        lax.fori_loop(nu_ref[0], n_tiles, start_tail, 0)
        lax.fori_loop(0, n_exp, wait_one, 0)
        lax.fori_loop(nu_ref[0], n_tiles, wait_tail, 0)

    base = t * (tr * TOP_K)

    def body(r, carry):
        for k in range(TOP_K):
            row = slot_ref[base + r * TOP_K + k]
            pltpu.async_copy(h_ref.at[pl.ds(r, 1), :], xs_hbm.at[pl.ds(row, 1), :], sem, priority=k % 2)
        return carry
    lax.fori_loop(0, tr, body, 0, unroll=2)

    for _ in range(TOP_K):
        pltpu.make_async_copy(h_ref, xs_hbm.at[pl.ds(0, tr), :], sem).wait()


def _dispatch_call(h2p, slot, last_tile, n_used, rows, n_tiles, n_exp):
    w = h2p.shape[1]
    tr = DISPATCH_TILE
    tm = EXPERT_TILE
    assert rows % tr == 0
    grid_spec = pltpu.PrefetchScalarGridSpec(
        num_scalar_prefetch=3,
        grid=(rows // tr,),
        in_specs=[pl.BlockSpec((tr, w), lambda t, sl, lt, nu: (t, 0))],
        out_specs=pl.BlockSpec(memory_space=pl.ANY),
        scratch_shapes=[pltpu.VMEM((tm, w), jnp.int32), pltpu.SemaphoreType.DMA(()), pltpu.SemaphoreType.DMA(())],
    )
    return pl.pallas_call(
        functools.partial(_dispatch_body, tr=tr, tm=tm, n_exp=n_exp, n_tiles=n_tiles),
        grid_spec=grid_spec,
        out_shape=jax.ShapeDtypeStruct((n_tiles * tm, w), jnp.int32),
        compiler_params=_params(("arbitrary",)),
        name="expert_dispatch",
    )(slot, last_tile, n_used, h2p)


def _expert_body(te_ref, nu_ref, nx_ref, x_ref, wgu_hbm, bgu_ref, sel_ref, wd_hbm, bd_ref, y_ref,
                 sgu_ref, sd_ref, wgu_ref, wd_ref, sem, *, layer):
    t = pl.program_id(0)
    n_used = nu_ref[0]

    def weight_copies(e):
        return (pltpu.make_async_copy(wgu_hbm.at[layer, e], sgu_ref, sem.at[0]),
                pltpu.make_async_copy(wd_hbm.at[layer, e], sd_ref, sem.at[1]))

    def fetch(e):
        for copy in weight_copies(e):
            copy.start()

    @pl.when(t == 0)
    def _():
        fetch(te_ref[0])

    @pl.when(t < n_used)
    def _():
        @pl.when(_expert_changed(te_ref, t))
        def _():
            for copy in weight_copies(te_ref[t]):
                copy.wait()
            wgu_ref[...] = sgu_ref[...].astype(BF16)
            wd_ref[...] = sd_ref[...].astype(BF16)

            @pl.when(nx_ref[t] >= 0)
            def _():
                fetch(nx_ref[t])

        x_lo, x_hi = _unpack_halves(x_ref[...])
        half = x_lo.shape[1]
        gu = (jnp.dot(x_lo.astype(BF16), wgu_ref[:half, :], preferred_element_type=F32)
              + jnp.dot(x_hi.astype(BF16), wgu_ref[half:, :], preferred_element_type=F32)) + bgu_ref[...]
        n2 = gu.shape[1]
        even = lax.broadcasted_iota(jnp.int32, gu.shape, 1) % 2 == 0
        gate = jnp.minimum(gu, SWIGLU_LIMIT)
        up = jnp.clip(pltpu.roll(gu, n2 - 1, 1), -SWIGLU_LIMIT, SWIGLU_LIMIT)
        act = jnp.where(even, (up + 1.0) * (gate * jax.nn.sigmoid(SWIGLU_ALPHA * gate)), 0.0)
        a = jnp.dot(act.astype(BF16), sel_ref[...], preferred_element_type=F32).astype(BF16)
        y = jnp.dot(a, wd_ref[...], preferred_element_type=F32) + bd_ref[...]
        y_ref[...] = _pack_halves(y)

    @pl.when(t >= n_used)
    def _():
        y_ref[...] = jnp.zeros(y_ref.shape, y_ref.dtype)


def _used_tile(t, nu):
    return jnp.minimum(t, nu[0] - 1)


def _expert_call(xs, w_gu, b_gu, w_down, b_down, layer, tile_expert, n_used, next_expert, n_tiles):
    n_exp, d, ff2 = w_gu.shape[1:]
    ff = ff2 // 2
    tm = EXPERT_TILE
    sel = (jnp.arange(ff2, dtype=jnp.int32)[:, None] == 2 * jnp.arange(ff, dtype=jnp.int32)[None, :]).astype(BF16)
    grid_spec = pltpu.PrefetchScalarGridSpec(
        num_scalar_prefetch=3,
        grid=(n_tiles,),
        in_specs=[
            pl.BlockSpec((tm, d // 2), lambda t, te, nu, nx: (_used_tile(t, nu), 0)),
            pl.BlockSpec(memory_space=pl.ANY),
            pl.BlockSpec((None, None, 1, ff2), lambda t, te, nu, nx: (layer, te[t], 0, 0)),
            pl.BlockSpec((ff2, ff), lambda t, te, nu, nx: (0, 0)),
            pl.BlockSpec(memory_space=pl.ANY),
            pl.BlockSpec((None, None, 1, d), lambda t, te, nu, nx: (layer, te[t], 0, 0)),
        ],
        out_specs=pl.BlockSpec((tm, d // 2), lambda t, te, nu, nx: (t, 0)),
        scratch_shapes=[pltpu.VMEM((d, ff2), F32), pltpu.VMEM((ff, d), F32),
                        pltpu.VMEM((d, ff2), BF16), pltpu.VMEM((ff, d), BF16),
                        pltpu.SemaphoreType.DMA((2,))],
    )
    return pl.pallas_call(
        functools.partial(_expert_body, layer=layer),
        grid_spec=grid_spec,
        out_shape=jax.ShapeDtypeStruct((n_tiles * tm, d // 2), jnp.int32),
        compiler_params=_params(("arbitrary",), EXPERT_VMEM_LIMIT),
        name="expert_mlp",
    )(tile_expert, n_used, next_expert, xs, w_gu, b_gu.reshape(b_gu.shape[0], n_exp, 1, ff2), sel,
      w_down, b_down.reshape(b_down.shape[0], n_exp, 1, d))


def _combine_body(pos_ref, y_hbm, w_ref, x_ref, g_ref, *refs, tc, post):
    if post == "next":
        ng_ref, nsh_ref, nsc_ref, o_ref, h_ref, ybuf, sem = refs
    elif post == "final":
        ng_ref, o_ref, ybuf, sem = refs
    else:
        o_ref, ybuf, sem = refs
    t = pl.program_id(0)
    slot = lax.rem(t, 2)

    def gather(tile, s):
        base = tile * (tc * TOP_K)

        def body(r, carry):
            for k in range(TOP_K):
                row = pos_ref[base + r * TOP_K + k]
                pltpu.async_copy(y_hbm.at[pl.ds(row, 1), :], ybuf.at[s, k, pl.ds(r, 1), :], sem.at[s],
                                 priority=k % 2)
            return carry
        lax.fori_loop(0, tc, body, 0, unroll=2)

    @pl.when(t == 0)
    def _():
        gather(0, 0)

    @pl.when(t + 1 < pl.num_programs(0))
    def _():
        gather(t + 1, 1 - slot)

    for k in range(TOP_K):
        pltpu.make_async_copy(y_hbm.at[pl.ds(0, tc), :], ybuf.at[slot, k], sem.at[slot]).wait()
    half = ybuf.shape[-1]
    ch = COMBINE_CHUNK
    w = w_ref[...]
    wk = [jnp.broadcast_to(w[:, k:k + 1], (tc, LANES)) for k in range(TOP_K)]
    sq = jnp.zeros((tc, 1), F32)
    for c0 in range(0, half, ch):
        acc_lo = acc_hi = None
        for k in range(TOP_K):
            lo, hi = _unpack_halves(ybuf[slot, k, :, c0:c0 + ch])
            wt = jnp.tile(wk[k], (1, ch // LANES))
            acc_lo = wt * lo if k == 0 else acc_lo + wt * lo
            acc_hi = wt * hi if k == 0 else acc_hi + wt * hi
        for acc, c in ((acc_lo, c0), (acc_hi, half + c0)):
            new = x_ref[:, c:c + ch] + g_ref[:, c:c + ch] * acc
            o_ref[:, c:c + ch] = new
            if post is not None:
                sq = sq + jnp.sum(new * new, axis=-1, keepdims=True)
    if post is None:
        return
    inv = lax.rsqrt(sq * (1.0 / (2 * half)) + NORM_EPS)
    for c in range(0, 2 * half, ch):
        norm = o_ref[:, c:c + ch] * inv * ng_ref[:, c:c + ch]
        if post == "final":
            o_ref[:, c:c + ch] = norm
        else:
            h_ref[:, c:c + ch] = (norm * (1.0 + nsc_ref[:, c:c + ch]) + nsh_ref[:, c:c + ch]).astype(h_ref.dtype)


def _combine_call(y_sorted, slot, top_w, xa, modl, gate_idx, rows, seq, n_batch, post=None, post_args=()):
    d = xa.shape[1]
    tc = COMBINE_TILE
    assert rows % tc == 0 and seq % tc == 0
    tpb = seq // tc

    def seg_row(which):
        return lambda t, pos: (jnp.minimum(t // tpb, n_batch) * 6 + which, 0, 0)

    row_spec = pl.BlockSpec((tc, d), lambda t, pos: (t, 0))
    in_specs = [
        pl.BlockSpec(memory_space=pl.ANY),
        pl.BlockSpec((tc, LANES), lambda t, pos: (t, 0)),
        row_spec,
        pl.BlockSpec((None, 1, d), seg_row(gate_idx)),
    ]
    args = [slot, y_sorted, top_w, xa, modl]
    out_specs = row_spec
    out_shape = jax.ShapeDtypeStruct((rows, d), F32)
    if post is not None:
        in_specs.append(pl.BlockSpec((1, d), lambda t, pos: (0, 0)))
        args.append(post_args[0].reshape(1, d))
    if post == "next":
        in_specs += [pl.BlockSpec((None, 1, d), seg_row(0)), pl.BlockSpec((None, 1, d), seg_row(1))]
        args += [post_args[1], post_args[1]]
        out_specs = [row_spec, row_spec]
        out_shape = [out_shape, jax.ShapeDtypeStruct((rows, d), BF16)]
    grid_spec = pltpu.PrefetchScalarGridSpec(
        num_scalar_prefetch=1,
        grid=(rows // tc,),
        in_specs=in_specs,
        out_specs=out_specs,
        scratch_shapes=[pltpu.VMEM((2, TOP_K, tc, d // 2), jnp.int32), pltpu.SemaphoreType.DMA((2,))],
    )
    return pl.pallas_call(
        functools.partial(_combine_body, tc=tc, post=post),
        grid_spec=grid_spec,
        out_shape=out_shape,
        compiler_params=_params(("arbitrary",)),
        name="expert_combine",
    )(*args)


def _axial_angles(seq, head_dim):
    rows = seq // GRID_W
    row = jnp.repeat(jnp.arange(rows, dtype=F32), GRID_W)
    col = jnp.tile(jnp.arange(GRID_W, dtype=F32), rows)
    n_freq = head_dim // 4
    inv_freq = ROPE_BASE ** (-jnp.arange(n_freq, dtype=F32) / n_freq)
    return jnp.concatenate([row[:, None] * inv_freq, col[:, None] * inv_freq], axis=-1)


def _rope_tables(seq, ret_head_dim):
    ang_a = _axial_angles(seq, ATT_HEAD_DIM)
    reps = LANES // ATT_HEAD_DIM
    cos_a = jnp.tile(jnp.concatenate([jnp.cos(ang_a), jnp.cos(ang_a)], axis=-1), (1, reps))
    sin_a = jnp.tile(jnp.concatenate([-jnp.sin(ang_a), jnp.sin(ang_a)], axis=-1), (1, reps))
    ang_r = _axial_angles(seq, ret_head_dim)
    return (cos_a, sin_a), (jnp.cos(ang_r), jnp.sin(ang_r))


def kernel(x, c, ctx, c_ctx, ada_w, ada_b, norm_mix_g, norm_ffn_g, w_in, att_sink, ret_decay_fwd,
           ret_decay_bwd, w_branch_att, w_branch_ret, branch_gate_b, w_out, router_w, router_b,
           exp_w_gu, exp_b_gu, exp_w_down, exp_b_down, final_norm_g):
    n_batch, seq, d = x.shape
    n_ctx = ctx.shape[1]
    depth = ada_w.shape[0]
    n_exp, ff = exp_w_down.shape[1], exp_w_down.shape[2]
    att_width = w_branch_att.shape[1]
    ret_width = w_branch_ret.shape[1]
    kv_width = att_width // ATT_GROUP
    sizes = (("k_att", kv_width), ("v_att", kv_width), ("k_ret", ret_width), ("v_ret", ret_width),
             ("q_att", att_width), ("q_ret", ret_width), ("g_ret", ret_width),
             ("gate_att", d), ("gate_ret", d))
    cols, off = {}, 0
    for name, width in sizes:
        cols[name] = off
        off += width
    in_cols = off
    assert in_cols == w_in.shape[2]

    lat_rows = n_batch * seq
    all_rows = lat_rows + n_batch * n_ctx
    xa = jnp.concatenate([x.reshape(lat_rows, d), ctx.reshape(n_batch * n_ctx, d)], axis=0)

    cond = jnp.zeros((16, d), F32).at[:n_batch].set(c).at[n_batch].set(c_ctx)
    mod = _mod_call(cond, ada_w, ada_b)
    rope_att, rope_ret = _rope_tables(seq, ret_width // RET_HEADS)

    modls = [mod[layer, :n_batch + 1].reshape((n_batch + 1) * 6, 1, d) for layer in range(depth)]
    h = _norm_call(xa, norm_mix_g[0], modls[0], all_rows, seq, n_batch, 0, 1)
    out = None
    for layer in range(depth):
        last = layer == depth - 1
        rows = lat_rows if last else all_rows
        modl = modls[layer]
        log_gamma = jnp.stack([jax.nn.log_sigmoid(ret_decay_fwd[layer].astype(F32)),
                               jax.nn.log_sigmoid(ret_decay_bwd[layer].astype(F32))])

        proj = _inproj_call(h, w_in, layer, all_rows, 0, in_cols)

        ya = _attn_call(proj, att_sink[layer], rope_att, cols, n_batch, seq, n_ctx, att_width, True)
        yr = _ret_call(proj, log_gamma, rope_ret, cols, n_batch, seq, n_ctx, ret_width, True)
        if not last:
            ya_c = _attn_call(proj, att_sink[layer], None, cols, n_batch, seq, n_ctx, att_width, False)
            yr_c = _ret_call(proj, log_gamma, None, cols, n_batch, seq, n_ctx, ret_width, False)
            ya = jnp.concatenate([ya, ya_c], axis=0)
            yr = jnp.concatenate([yr, yr_c], axis=0)

        z = _merge_gate_call(ya, yr, proj, w_branch_att, w_branch_ret, branch_gate_b, layer, rows, cols)
        xa = _resid_matmul_call(z, w_out, layer, xa, modl, 2, rows, seq, n_batch)

        h2, top_idx, top_w = _norm_call(xa, norm_ffn_g[layer], modl, rows, seq, n_batch, 3, 4,
                                        router=(router_w[layer], router_b[layer]))
        n_tiles = -(-rows * TOP_K // EXPERT_TILE) + n_exp
        tile_expert, n_used, slot, last_tile, next_expert = _routing(
            top_idx[:, :TOP_K], n_exp, EXPERT_TILE, n_tiles)
        xs = _dispatch_call(h2, slot, last_tile, n_used, rows, n_tiles, n_exp)
        y_sorted = _expert_call(xs, exp_w_gu, exp_b_gu, exp_w_down, exp_b_down, layer,
                                tile_expert, n_used, next_expert, n_tiles)
        if last:
            out = _combine_call(y_sorted, slot, top_w, xa, modl, 5, rows, seq, n_batch,
                                post="final", post_args=(final_norm_g,))
        else:
            xa, h = _combine_call(y_sorted, slot, top_w, xa, modl, 5, rows, seq, n_batch,
                                  post="next", post_args=(norm_mix_g[layer + 1], modls[layer + 1]))
    return out.reshape(n_batch, seq, d)
```

```python
import functools

import jax
import jax.numpy as jnp
from jax import lax
from jax.experimental import pallas as pl
from jax.experimental.pallas import tpu as pltpu

F32 = jnp.float32
BF16 = jnp.bfloat16

GRID_W = 64
ATT_HEAD_DIM = 64
ATT_GROUP = 4
ATT_WINDOW = 128
ATT_BLOCK = 128
RET_HEADS = 8
RET_CHUNK = 128
RET_UNROLL_BWD = 4
RET_UNROLL_FWD = 8
TOP_K = 4
SWIGLU_ALPHA = 1.702
SWIGLU_LIMIT = 7.0
ROPE_BASE = 10000.0
NORM_EPS = 1e-6
GN_EPS = 1e-5
NEG_INF = -1e30
LOG2_E = 1.4426950408889634

LANES = 128
VMEM_LIMIT = 56 * 1024 * 1024
EXPERT_VMEM_LIMIT = 60 * 1024 * 1024


def _params(sem, vmem=VMEM_LIMIT):
    return pltpu.CompilerParams(dimension_semantics=sem, vmem_limit_bytes=vmem)


def _tile(n, cap, mult):
    for t in range(min(cap, n), 0, -1):
        if n % t == 0 and t % mult == 0:
            return t
    raise ValueError(f"no tile for {n} (cap {cap}, multiple of {mult})")


def _silu(v):
    return v * jax.nn.sigmoid(v)


def _mod_body(c_ref, w_ref, b_ref, o_ref):
    s = _silu(c_ref[...]).astype(BF16)
    o_ref[...] = jnp.dot(s, w_ref[...].astype(BF16), preferred_element_type=F32) + b_ref[...]


def _mod_call(cond, ada_w, ada_b):
    depth, d, n = ada_w.shape
    rows = cond.shape[0]
    tn = _tile(n, 512, LANES)
    return pl.pallas_call(
        _mod_body,
        grid=(depth, n // tn),
        in_specs=[
            pl.BlockSpec((rows, d), lambda l, j: (0, 0)),
            pl.BlockSpec((None, d, tn), lambda l, j: (l, 0, j)),
            pl.BlockSpec((None, 1, tn), lambda l, j: (l, 0, j)),
        ],
        out_specs=pl.BlockSpec((None, rows, tn), lambda l, j: (l, 0, j)),
        out_shape=jax.ShapeDtypeStruct((depth, rows, n), F32),
        compiler_params=_params(("arbitrary", "arbitrary")),
        name="adaln_mod",
    )(cond, ada_w, ada_b.reshape(depth, 1, n))


def _rms(x, g):
    ms = jnp.mean(x * x, axis=-1, keepdims=True)
    return x * lax.rsqrt(ms + NORM_EPS) * g


def _norm_mod_body(x_ref, g_ref, sh_ref, sc_ref, h_ref):
    h = _rms(x_ref[...], g_ref[...]) * (1.0 + sc_ref[...]) + sh_ref[...]
    h_ref[...] = h.astype(h_ref.dtype)


def _pack_halves(v):
    w = v.shape[1] // 2
    bits = lax.bitcast_convert_type(v.astype(BF16).astype(F32), jnp.int32)
    return lax.shift_right_logical(bits[:, :w], 16) | (bits[:, w:] & jnp.int32(-65536))


def _unpack_halves(p):
    lo = lax.bitcast_convert_type(lax.shift_left(p, 16), F32)
    hi = lax.bitcast_convert_type(p & jnp.int32(-65536), F32)
    return lo, hi


def _split_bf16(v):
    hi = v.astype(BF16)
    lo = (v - hi.astype(F32)).astype(BF16)
    return hi, lo


def _norm_router_body(x_ref, g_ref, sh_ref, sc_ref, rw_ref, rb_ref, h_ref, idx_ref, wt_ref):
    h = _rms(x_ref[...], g_ref[...]) * (1.0 + sc_ref[...]) + sh_ref[...]
    h_ref[...] = _pack_halves(h)
    h_hi, h_lo = _split_bf16(h)
    w_hi, w_lo = _split_bf16(rw_ref[...])
    logits = (jnp.dot(h_hi, w_hi, preferred_element_type=F32)
              + jnp.dot(h_hi, w_lo, preferred_element_type=F32)
              + jnp.dot(h_lo, w_hi, preferred_element_type=F32)) + rb_ref[...]
    n_exp = logits.shape[-1]
    lane = lax.broadcasted_iota(jnp.int32, logits.shape, 1).astype(F32)
    work = logits
    out_lane = lax.broadcasted_iota(jnp.int32, idx_ref.shape, 1)
    idx = jnp.zeros(idx_ref.shape, F32)
    ex = jnp.zeros(wt_ref.shape, F32)
    top = None
    total = None
    for k in range(TOP_K):
        m = jnp.max(work, axis=-1, keepdims=True)
        first = jnp.min(jnp.where(work == m, lane, float(n_exp)), axis=-1, keepdims=True)
        work = jnp.where(lane == first, -jnp.inf, work)
        if k == 0:
            top = m
        e = jnp.exp(m - top)
        total = e if k == 0 else total + e
        idx = jnp.where(out_lane == k, first, idx)
        ex = jnp.where(out_lane == k, e, ex)
    idx_ref[...] = idx.astype(jnp.int32)
    wt_ref[...] = ex / total


def _seg_map(tiles_per_batch, n_batch, which):
    def index(i):
        return (jnp.minimum(i // tiles_per_batch, n_batch) * 6 + which, 0, 0)
    return index


def _norm_call(xa, g, modl, rows, seq, n_batch, shift_idx, scale_idx, router=None):
    d = xa.shape[1]
    tr = 256
    assert rows % tr == 0 and seq % tr == 0
    tpb = seq // tr
    row_spec = pl.BlockSpec((tr, d), lambda i: (i, 0))
    in_specs = [
        row_spec,
        pl.BlockSpec((1, d), lambda i: (0, 0)),
        pl.BlockSpec((None, 1, d), _seg_map(tpb, n_batch, shift_idx)),
        pl.BlockSpec((None, 1, d), _seg_map(tpb, n_batch, scale_idx)),
    ]
    args = [xa, g.reshape(1, d), modl, modl]
    if router is None:
        return pl.pallas_call(
            _norm_mod_body, grid=(rows // tr,), in_specs=in_specs, out_specs=row_spec,
            out_shape=jax.ShapeDtypeStruct((rows, d), BF16),
            compiler_params=_params(("arbitrary",)), name="norm_mod",
        )(*args)
    rw, rb = router
    n_exp = rw.shape[1]
    in_specs += [pl.BlockSpec((d, n_exp), lambda i: (0, 0)), pl.BlockSpec((1, n_exp), lambda i: (0, 0))]
    args += [rw, rb.reshape(1, n_exp)]
    return pl.pallas_call(
        _norm_router_body, grid=(rows // tr,), in_specs=in_specs,
        out_specs=[pl.BlockSpec((tr, d // 2), lambda i: (i, 0)),
                   pl.BlockSpec((tr, LANES), lambda i: (i, 0)), pl.BlockSpec((tr, LANES), lambda i: (i, 0))],
        out_shape=[jax.ShapeDtypeStruct((rows, d // 2), jnp.int32), jax.ShapeDtypeStruct((rows, LANES), jnp.int32),
                   jax.ShapeDtypeStruct((rows, LANES), F32)],
        compiler_params=_params(("arbitrary",)), name="norm_router",
    )(*args)


def _inproj_body(a_ref, w_ref, o_ref, wb_ref):
    @pl.when(pl.program_id(1) == 0)
    def _():
        wb_ref[...] = w_ref[...].astype(BF16)
    o_ref[...] = jnp.dot(a_ref[...], wb_ref[...], preferred_element_type=F32).astype(o_ref.dtype)


def _inproj_call(h, w_in, layer, rows, col_lo, col_hi):
    d = h.shape[1]
    tn = 1024
    assert col_lo % tn == 0 and col_hi % tn == 0
    tm = _tile(rows, 576, 16)
    j0 = col_lo // tn
    return pl.pallas_call(
        _inproj_body,
        grid=((col_hi - col_lo) // tn, rows // tm),
        in_specs=[
            pl.BlockSpec((tm, d), lambda j, i: (i, 0)),
            pl.BlockSpec((None, d, tn), lambda j, i: (layer, 0, j0 + j)),
        ],
        out_specs=pl.BlockSpec((tm, tn), lambda j, i: (i, j)),
        out_shape=jax.ShapeDtypeStruct((rows, col_hi - col_lo), BF16),
        scratch_shapes=[pltpu.VMEM((d, tn), BF16)],
        compiler_params=_params(("arbitrary", "arbitrary")),
        name="in_proj",
    )(h, w_in)


def _attn_body(*refs, local, scale):
    if local:
        (sink_ref, q_ref, kp_ref, kc_ref, kn_ref, vp_ref, vc_ref, vn_ref, kx_ref, vx_ref,
         cp_ref, cc_ref, cn_ref, sp_ref, sc_ref, sn_ref, bias_ref, o_ref) = refs
    else:
        sink_ref, q_ref, kx_ref, vx_ref, o_ref = refs
    pair = pl.program_id(1)
    n = pl.program_id(2)
    blk = ATT_BLOCK
    half = ATT_HEAD_DIM // 2

    lane_q = lax.broadcasted_iota(jnp.int32, (blk, LANES), 1)
    first_half = (lane_q % ATT_HEAD_DIM) < half

    def rope(v, c_ref, s_ref):
        rot = jnp.where(first_half, pltpu.roll(v, LANES - half, 1), pltpu.roll(v, half, 1))
        return v * c_ref[...] + rot * s_ref[...]

    if local:
        keys = jnp.concatenate([
            rope(kp_ref[...].astype(F32), cp_ref, sp_ref),
            rope(kc_ref[...].astype(F32), cc_ref, sc_ref),
            rope(kn_ref[...].astype(F32), cn_ref, sn_ref),
            kx_ref[...].astype(F32)], axis=0)
        vals = jnp.concatenate([vp_ref[...], vc_ref[...], vn_ref[...], vx_ref[...]], axis=0).astype(F32)
    else:
        keys = kx_ref[...].astype(F32)
        vals = vx_ref[...].astype(F32)
    nk = keys.shape[0]

    lane_k = lax.broadcasted_iota(jnp.int32, (nk, LANES), 1)
    low = lane_k < ATT_HEAD_DIM

    def placed(v):
        swapped = pltpu.roll(v, ATT_HEAD_DIM, 1)
        return [[jnp.where(low, v, 0.0).astype(BF16), jnp.where(low, 0.0, swapped).astype(BF16)],
                [jnp.where(low, swapped, 0.0).astype(BF16), jnp.where(low, 0.0, v).astype(BF16)]]

    k_var = placed(keys)
    v_var = placed(vals)

    top_rows = lax.broadcasted_iota(jnp.int32, (2 * blk, 1), 0) < blk

    n_slab = q_ref.shape[1] // LANES
    q_slabs = []
    for c in range(n_slab):
        qc = q_ref[:, c * LANES:(c + 1) * LANES].astype(F32)
        if local:
            qc = rope(qc, cc_ref, sc_ref)
        q_slabs.append((qc * (scale * LOG2_E)).astype(BF16))

    out = [jnp.zeros((blk, LANES), F32) for _ in range(n_slab)]
    combos = [(j, hh) for j in range(2) for hh in range(2)]
    scores = []
    for j, hh in combos:
        lhs = jnp.concatenate([q_slabs[2 * j], q_slabs[2 * j + 1]], axis=0)
        s = lax.dot_general(lhs, k_var[j][hh], (((1,), (1,)), ((), ())),
                            preferred_element_type=F32)
        if local:
            s = s + bias_ref[...]
        scores.append(s)
    probs = []
    for (j, hh), s in zip(combos, scores):
        head0 = pair * (2 * ATT_GROUP) + ATT_GROUP * j + hh
        sink = jnp.where(top_rows, sink_ref[head0], sink_ref[head0 + 2]) * LOG2_E
        m = jnp.maximum(jnp.max(s, axis=-1, keepdims=True), sink)
        e = jnp.exp2(s - m)
        denom = jnp.sum(e, axis=-1, keepdims=True) + jnp.exp2(sink - m)
        probs.append((e.astype(BF16), 1.0 / denom))
    for (j, hh), (e, inv) in zip(combos, probs):
        o = jnp.dot(e, v_var[j][hh], preferred_element_type=F32) * inv
        out[2 * j] = out[2 * j] + o[:blk]
        out[2 * j + 1] = out[2 * j + 1] + o[blk:]
    for c in range(n_slab):
        o_ref[:, c * LANES:(c + 1) * LANES] = out[c].astype(o_ref.dtype)


def _attn_call(proj_kv, proj_q, sink, rope_tabs, cols, n_batch, seq, n_ctx, att_width, local):
    blk = ATT_BLOCK
    nb = seq // blk
    pair_w = 2 * ATT_HEAD_DIM
    q_w = pair_w * ATT_GROUP
    n_pair = att_width // q_w
    k0 = cols["k_att"] // pair_w
    v0 = cols["v_att"] // pair_w
    q0 = cols["q_att"] // q_w
    ctx_row0 = (n_batch * seq) // n_ctx
    kx_spec = pl.BlockSpec((n_ctx, pair_w), lambda b, p, n: (ctx_row0 + b, k0 + p))
    vx_spec = pl.BlockSpec((n_ctx, pair_w), lambda b, p, n: (ctx_row0 + b, v0 + p))
    sink_spec = pl.BlockSpec(memory_space=pltpu.SMEM)
    scale = ATT_HEAD_DIM ** -0.5
    if local:
        cos_t, sin_t = rope_tabs

        def prev(n):
            return jnp.maximum(n - 1, 0)

        def nxt(n):
            return jnp.minimum(n + 1, nb - 1)

        def kv(c0, f):
            return pl.BlockSpec((blk, pair_w), lambda b, p, n: (b * nb + f(n), c0 + p))

        def tab(f):
            return pl.BlockSpec((blk, LANES), lambda b, p, n: (f(n), 0))

        ident = lambda n: n
        assert nb >= 2
        r = jnp.arange(2 * blk, dtype=jnp.int32)[:, None] % blk
        c = jnp.arange(3 * blk + n_ctx, dtype=jnp.int32)[None, :]
        band = (jnp.abs(c - blk - r) <= ATT_WINDOW) | (c >= 3 * blk)
        edge = jnp.stack([band & (c >= blk), band, band & ((c < 2 * blk) | (c >= 3 * blk))])
        bias = jnp.where(edge, 0.0, NEG_INF).astype(F32)
        bias_spec = pl.BlockSpec((None, 2 * blk, 3 * blk + n_ctx),
                                 lambda b, p, n: ((n > 0).astype(jnp.int32) + (n == nb - 1).astype(jnp.int32), 0, 0))
        in_specs = [sink_spec,
                    pl.BlockSpec((blk, q_w), lambda b, p, n: (b * nb + n, q0 + p)),
                    kv(k0, prev), kv(k0, ident), kv(k0, nxt),
                    kv(v0, prev), kv(v0, ident), kv(v0, nxt),
                    kx_spec, vx_spec,
                    tab(prev), tab(ident), tab(nxt), tab(prev), tab(ident), tab(nxt), bias_spec]
        args = [sink, proj_q] + [proj_kv] * 6 + [proj_kv, proj_kv] + [cos_t] * 3 + [sin_t] * 3 + [bias]
        grid = (n_batch, n_pair, nb)
        rows = n_batch * seq
        out_spec = pl.BlockSpec((blk, q_w), lambda b, p, n: (b * nb + n, p))
    else:
        nqb = n_ctx // blk
        qrow0 = (n_batch * seq) // blk
        in_specs = [sink_spec,
                    pl.BlockSpec((blk, q_w), lambda b, p, n: (qrow0 + b * nqb + n, q0 + p)),
                    kx_spec, vx_spec]
        args = [sink, proj_q, proj_kv, proj_kv]
        grid = (n_batch, n_pair, nqb)
        rows = n_batch * n_ctx
        out_spec = pl.BlockSpec((blk, q_w), lambda b, p, n: (b * nqb + n, p))
    return pl.pallas_call(
        functools.partial(_attn_body, local=local, scale=scale),
        grid=grid, in_specs=in_specs, out_specs=out_spec,
        out_shape=jax.ShapeDtypeStruct((rows, att_width), BF16),
        compiler_params=_params(("arbitrary",) * 3),
        name="window_attn" if local else "ctx_attn",
    )(*args)


def _dot_tn(a, b):
    return lax.dot_general(a, b, (((0,), (0,)), ((), ())), preferred_element_type=F32)


def _ret_body(*refs, latent, n_chunks, n_ctx, k_scale):
    if latent:
        (lg_ref, q_ref, k_ref, v_ref, g_ref, kx_ref, vx_ref, cos_ref, sin_ref,
         o_ref, qs_ref, ks_ref, cb_ref, st_ref, sf_ref) = refs
    else:
        lg_ref, q_ref, k_ref, v_ref, g_ref, o_ref, qs_ref, ks_ref, cb_ref, st_ref, sf_ref = refs
    head = pl.program_id(1)
    lg_f = lg_ref[0, head]
    lg_b = lg_ref[1, head]
    c = RET_CHUNK
    hd = q_ref.shape[1]
    hh = hd // 2
    idx = lax.broadcasted_iota(jnp.int32, (c, 1), 0).astype(F32)
    qdec_f = jnp.exp(lg_f * (idx + 1.0))
    kdec_f = jnp.exp(lg_f * (c - 1.0 - idx))
    qdec_b = jnp.exp(lg_b * (c - idx))
    kdec_b = jnp.exp(lg_b * idx)
    one = jnp.ones((1, 1), F32)
    cdec_f = jnp.exp(lg_f * c * one)
    cdec_b = jnp.exp(lg_b * c * one)
    rel = (lax.broadcasted_iota(jnp.int32, (c, c), 0) - lax.broadcasted_iota(jnp.int32, (c, c), 1)).astype(F32)
    dmat = (jnp.where(rel >= 0, jnp.exp(lg_f * jnp.maximum(rel, 0.0)), 0.0)
            + jnp.where(rel <= 0, jnp.exp(lg_b * jnp.maximum(-rel, 0.0)), 0.0))

    def rope(v, rows):
        v1 = v[:, :hh]
        v2 = v[:, hh:]
        cs = cos_ref[rows, :]
        sn = sin_ref[rows, :]
        return jnp.concatenate([v1 * cs - v2 * sn, v2 * cs + v1 * sn], axis=1)

    if latent:
        t = lax.broadcasted_iota(jnp.int32, (n_ctx, 1), 0).astype(F32)
        kx = kx_ref[...].astype(F32) * k_scale
        sf_ref[...] = _dot_tn((kx * jnp.exp(lg_f * (n_ctx - 1.0 - t))).astype(BF16), vx_ref[...])
        st_ref[...] = _dot_tn((kx * jnp.exp(lg_b * t)).astype(BF16), vx_ref[...])
    else:
        sf_ref[...] = jnp.zeros((hd, hd), F32)
        st_ref[...] = jnp.zeros((hd, hd), F32)


    def bwd(step, carry):
        rows = pl.ds(pl.multiple_of((n_chunks - 1 - step) * c, c), c)
        q = q_ref[rows, :].astype(F32)
        k = k_ref[rows, :].astype(F32)
        if latent:
            q = rope(q, rows)
            k = rope(k, rows)
        k = k * k_scale
        qs_ref[rows, :] = q.astype(BF16)
        ks_ref[rows, :] = k.astype(BF16)
        state = st_ref[...]
        cb_ref[rows, :] = jnp.dot((q * qdec_b).astype(BF16), state.astype(BF16), preferred_element_type=F32)
        st_ref[...] = cdec_b * state + _dot_tn((k * kdec_b).astype(BF16), v_ref[rows, :])
        return carry

    lax.fori_loop(0, n_chunks, bwd, 0, unroll=min(RET_UNROLL_BWD, n_chunks))

    st_ref[...] = sf_ref[...]

    def fwd(step, carry):
        rows = pl.ds(pl.multiple_of(step * c, c), c)
        qb = qs_ref[rows, :]
        kb = ks_ref[rows, :]
        v = v_ref[rows, :]
        scores = lax.dot_general(qb, kb, (((1,), (1,)), ((), ())), preferred_element_type=F32) * dmat
        o = jnp.dot(scores.astype(BF16), v, preferred_element_type=F32)
        state = st_ref[...]
        o = o + jnp.dot((qb.astype(F32) * qdec_f).astype(BF16), state.astype(BF16), preferred_element_type=F32)
        o = o + cb_ref[rows, :]
        st_ref[...] = cdec_f * state + _dot_tn((kb.astype(F32) * kdec_f).astype(BF16), v)
        mu = jnp.mean(o, axis=-1, keepdims=True)
        dev = o - mu
        var = jnp.mean(dev * dev, axis=-1, keepdims=True)
        y = dev * lax.rsqrt(var + GN_EPS)
        o_ref[rows, :] = (_silu(g_ref[rows, :].astype(F32)) * y).astype(o_ref.dtype)
        return carry

    lax.fori_loop(0, n_chunks, fwd, 0, unroll=min(RET_UNROLL_FWD, n_chunks))


def _ret_call(proj_kv, proj_q, log_gamma, rope_tabs, cols, n_batch, seq, n_ctx, ret_width, latent):
    hd = ret_width // RET_HEADS
    t_len = seq if latent else n_ctx
    assert t_len % RET_CHUNK == 0
    k0, v0, q0, g0 = (cols[name] // hd for name in ("k_ret", "v_ret", "q_ret", "g_ret"))
    ctx_row0 = (n_batch * seq) // n_ctx
    row0 = 0 if latent else ctx_row0

    def seq_spec(c0):
        return pl.BlockSpec((t_len, hd), lambda b, h: (row0 + b, c0 + h))

    in_specs = [pl.BlockSpec(memory_space=pltpu.SMEM),
                seq_spec(q0), seq_spec(k0), seq_spec(v0), seq_spec(g0)]
    args = [log_gamma, proj_q, proj_kv, proj_kv, proj_q]
    if latent:
        cos_t, sin_t = rope_tabs
        in_specs += [pl.BlockSpec((n_ctx, hd), lambda b, h: (ctx_row0 + b, k0 + h)),
                     pl.BlockSpec((n_ctx, hd), lambda b, h: (ctx_row0 + b, v0 + h)),
                     pl.BlockSpec((t_len, hd // 2), lambda b, h: (0, 0)),
                     pl.BlockSpec((t_len, hd // 2), lambda b, h: (0, 0))]
        args += [proj_kv, proj_kv, cos_t, sin_t]
    return pl.pallas_call(
        functools.partial(_ret_body, latent=latent, n_chunks=t_len // RET_CHUNK, n_ctx=n_ctx,
                          k_scale=hd ** -0.5),
        grid=(n_batch, RET_HEADS), in_specs=in_specs,
        out_specs=pl.BlockSpec((t_len, hd), lambda b, h: (b, h)),
        out_shape=jax.ShapeDtypeStruct((n_batch * t_len, ret_width), BF16),
        scratch_shapes=[pltpu.VMEM((t_len, hd), BF16), pltpu.VMEM((t_len, hd), BF16),
                        pltpu.VMEM((t_len, hd), F32), pltpu.VMEM((hd, hd), F32), pltpu.VMEM((hd, hd), F32)],
        compiler_params=_params(("arbitrary", "arbitrary")),
        name="retention" if latent else "ctx_retention",
    )(*args)


def _merge_gate_body(ya_ref, yr_ref, ga_ref, gr_ref, wa_ref, wr_ref, ba_ref, br_ref, z_ref, wab_ref, wrb_ref):
    @pl.when(pl.program_id(1) == 0)
    def _():
        wab_ref[...] = wa_ref[...].astype(BF16)
        wrb_ref[...] = wr_ref[...].astype(BF16)
    a = jnp.dot(ya_ref[...], wab_ref[...], preferred_element_type=F32)
    r = jnp.dot(yr_ref[...], wrb_ref[...], preferred_element_type=F32)
    ga = jax.nn.sigmoid(ga_ref[...].astype(F32) + ba_ref[...])
    gr = jax.nn.sigmoid(gr_ref[...].astype(F32) + br_ref[...])
    z_ref[...] = (ga * a + gr * r).astype(z_ref.dtype)


def _merge_gate_call(ya, yr, proj, w_att, w_ret, gate_b, layer, rows, cols):
    d = w_att.shape[2]
    wa_rows, wr_rows = w_att.shape[1], w_ret.shape[1]
    tn = 512
    tm = _tile(rows, 1024, 16)
    ga0 = cols["gate_att"] // tn
    gr0 = cols["gate_ret"] // tn
    gb = gate_b.reshape(gate_b.shape[0], 1, 2 * d)
    return pl.pallas_call(
        _merge_gate_body,
        grid=(d // tn, rows // tm),
        in_specs=[
            pl.BlockSpec((tm, wa_rows), lambda j, i: (i, 0)),
            pl.BlockSpec((tm, wr_rows), lambda j, i: (i, 0)),
            pl.BlockSpec((tm, tn), lambda j, i: (i, ga0 + j)),
            pl.BlockSpec((tm, tn), lambda j, i: (i, gr0 + j)),
            pl.BlockSpec((None, wa_rows, tn), lambda j, i: (layer, 0, j)),
            pl.BlockSpec((None, wr_rows, tn), lambda j, i: (layer, 0, j)),
            pl.BlockSpec((None, 1, tn), lambda j, i: (layer, 0, j)),
            pl.BlockSpec((None, 1, tn), lambda j, i: (layer, 0, d // tn + j)),
        ],
        out_specs=pl.BlockSpec((tm, tn), lambda j, i: (i, j)),
        out_shape=jax.ShapeDtypeStruct((rows, d), BF16),
        scratch_shapes=[pltpu.VMEM((wa_rows, tn), BF16), pltpu.VMEM((wr_rows, tn), BF16)],
        compiler_params=_params(("arbitrary", "arbitrary")),
        name="merge_gate",
    )(ya, yr, proj, proj, w_att, w_ret, gb, gb)


def _resid_matmul_body(z_ref, w_ref, x_ref, g_ref, o_ref, wb_ref):
    @pl.when(pl.program_id(1) == 0)
    def _():
        wb_ref[...] = w_ref[...].astype(BF16)
    y = jnp.dot(z_ref[...], wb_ref[...], preferred_element_type=F32)
    o_ref[...] = x_ref[...] + g_ref[...] * y


def _resid_matmul_call(z, w, layer, xa, modl, gate_idx, rows, seq, n_batch):
    k, d = w.shape[1], w.shape[2]
    tn = 512
    tm = 512
    assert rows % tm == 0 and seq % tm == 0
    tpb = seq // tm

    def gate_map(j, i):
        return (jnp.minimum(i // tpb, n_batch) * 6 + gate_idx, 0, j)

    return pl.pallas_call(
        _resid_matmul_body,
        grid=(d // tn, rows // tm),
        in_specs=[
            pl.BlockSpec((tm, k), lambda j, i: (i, 0)),
            pl.BlockSpec((None, k, tn), lambda j, i: (layer, 0, j)),
            pl.BlockSpec((tm, tn), lambda j, i: (i, j)),
            pl.BlockSpec((None, 1, tn), gate_map),
        ],
        out_specs=pl.BlockSpec((tm, tn), lambda j, i: (i, j)),
        out_shape=jax.ShapeDtypeStruct((rows, d), F32),
        scratch_shapes=[pltpu.VMEM((k, tn), BF16)],
        compiler_params=_params(("arbitrary", "arbitrary")),
        name="resid_matmul",
    )(z, w, xa, modl)


EXPERT_TILE = 256
COMBINE_TILE = 128
DISPATCH_TILE = 256
COMBINE_CHUNK = 256


def _routing(top_idx, n_exp, tm, n_tiles):
    n_assign = top_idx.size
    e_flat = top_idx.reshape(n_assign)
    onehot = (e_flat[:, None] == jnp.arange(n_exp, dtype=jnp.int32)[None, :]).astype(jnp.int32)
    running = jnp.cumsum(onehot, axis=0)
    counts = running[-1]
    group = (counts + tm - 1) // tm * tm
    group_end = jnp.cumsum(group)
    slot = jnp.sum(onehot * (running - 1 + (group_end - group)[None, :]), axis=1).astype(jnp.int32)
    n_used = (group_end[-1] // tm).astype(jnp.int32).reshape(1)
    tile_start = jnp.arange(n_tiles, dtype=jnp.int32) * tm
    tile_expert = jnp.sum((group_end[None, :] <= tile_start[:, None]).astype(jnp.int32), axis=1)
    tile_expert = jnp.minimum(tile_expert, n_exp - 1).astype(jnp.int32)
    last_tile = jnp.stack([jnp.maximum(group_end - tm, 0), (group > 0).astype(jnp.int32)], axis=1)
    ids = jnp.arange(n_exp, dtype=jnp.int32)
    later = (ids[None, :] > ids[:, None]) & (group[None, :] > 0)
    next_of = jnp.min(jnp.where(later, ids[None, :], n_exp), axis=1)
    next_of = jnp.where(next_of == n_exp, -1, next_of)
    next_expert = jnp.sum((tile_expert[:, None] == ids[None, :]) * next_of[None, :], axis=1).astype(jnp.int32)
    return tile_expert, n_used, slot, last_tile.reshape(2 * n_exp).astype(jnp.int32), next_expert


def _expert_changed(te_ref, t):
    return jnp.logical_or(t == 0, te_ref[t] != te_ref[jnp.maximum(t - 1, 0)])


def _dispatch_body(slot_ref, last_ref, nu_ref, h_ref, xs_hbm, zero_ref, sem, zsem, *, tr, tm, n_exp, n_tiles):
    t = pl.program_id(0)

    @pl.when(t == 0)
    def _():
        zero_ref[...] = jnp.zeros(zero_ref.shape, zero_ref.dtype)

        def zero_copy(start):
            return pltpu.make_async_copy(zero_ref, xs_hbm.at[pl.ds(pl.multiple_of(start, tm), tm), :], zsem)

        def start_one(e, carry):
            @pl.when(last_ref[2 * e + 1] == 1)
            def _():
                zero_copy(last_ref[2 * e]).start()
            return carry

        def wait_one(e, carry):
            @pl.when(last_ref[2 * e + 1] == 1)
            def _():
                zero_copy(last_ref[2 * e]).wait()
            return carry

        def start_tail(i, carry):
            zero_copy(i * tm).start()
            return carry

        def wait_tail(i, carry):
            zero_copy(i * tm).wait()
            return carry

        lax.fori_loop(0, n_exp, start_one, 0)
        lax.fori_loop(nu_ref[0], n_tiles, start_tail, 0)
        lax.fori_loop(0, n_exp, wait_one, 0)
        lax.fori_loop(nu_ref[0], n_tiles, wait_tail, 0)

    base = t * (tr * TOP_K)

    def body(r, carry):
        for k in range(TOP_K):
            row = slot_ref[base + r * TOP_K + k]
            pltpu.async_copy(h_ref.at[pl.ds(r, 1), :], xs_hbm.at[pl.ds(row, 1), :], sem, priority=k % 2)
        return carry
    lax.fori_loop(0, tr, body, 0, unroll=2)

    for _ in range(TOP_K):
        pltpu.make_async_copy(h_ref, xs_hbm.at[pl.ds(0, tr), :], sem).wait()


def _dispatch_call(h2p, slot, last_tile, n_used, rows, n_tiles, n_exp):
    w = h2p.shape[1]
    tr = DISPATCH_TILE
    tm = EXPERT_TILE
    assert rows % tr == 0
    grid_spec = pltpu.PrefetchScalarGridSpec(
        num_scalar_prefetch=3,
        grid=(rows // tr,),
        in_specs=[pl.BlockSpec((tr, w), lambda t, sl, lt, nu: (t, 0))],
        out_specs=pl.BlockSpec(memory_space=pl.ANY),
        scratch_shapes=[pltpu.VMEM((tm, w), jnp.int32), pltpu.SemaphoreType.DMA(()), pltpu.SemaphoreType.DMA(())],
    )
    return pl.pallas_call(
        functools.partial(_dispatch_body, tr=tr, tm=tm, n_exp=n_exp, n_tiles=n_tiles),
        grid_spec=grid_spec,
        out_shape=jax.ShapeDtypeStruct((n_tiles * tm, w), jnp.int32),
        compiler_params=_params(("arbitrary",)),
        name="expert_dispatch",
    )(slot, last_tile, n_used, h2p)


def _expert_body(te_ref, nu_ref, nx_ref, x_ref, wgu_hbm, bgu_ref, sel_ref, wd_hbm, bd_ref, y_ref,
                 sgu_ref, sd_ref, wgu_ref, wd_ref, sem, *, layer):
    t = pl.program_id(0)
    n_used = nu_ref[0]

    def weight_copies(e):
        return (pltpu.make_async_copy(wgu_hbm.at[layer, e], sgu_ref, sem.at[0]),
                pltpu.make_async_copy(wd_hbm.at[layer, e], sd_ref, sem.at[1]))

    def fetch(e):
        for copy in weight_copies(e):
            copy.start()

    @pl.when(t == 0)
    def _():
        fetch(te_ref[0])

    @pl.when(t < n_used)
    def _():
        @pl.when(_expert_changed(te_ref, t))
        def _():
            for copy in weight_copies(te_ref[t]):
                copy.wait()
            wgu_ref[...] = sgu_ref[...].astype(BF16)
            wd_ref[...] = sd_ref[...].astype(BF16)

            @pl.when(nx_ref[t] >= 0)
            def _():
                fetch(nx_ref[t])

        x_lo, x_hi = _unpack_halves(x_ref[...])
        half = x_lo.shape[1]
        gu = (jnp.dot(x_lo.astype(BF16), wgu_ref[:half, :], preferred_element_type=F32)
              + jnp.dot(x_hi.astype(BF16), wgu_ref[half:, :], preferred_element_type=F32)) + bgu_ref[...]
        n2 = gu.shape[1]
        even = lax.broadcasted_iota(jnp.int32, gu.shape, 1) % 2 == 0
        gate = jnp.minimum(gu, SWIGLU_LIMIT)
        up = jnp.clip(pltpu.roll(gu, n2 - 1, 1), -SWIGLU_LIMIT, SWIGLU_LIMIT)
        act = jnp.where(even, (up + 1.0) * (gate * jax.nn.sigmoid(SWIGLU_ALPHA * gate)), 0.0)
        a = jnp.dot(act.astype(BF16), sel_ref[...], preferred_element_type=F32).astype(BF16)
        y = jnp.dot(a, wd_ref[...], preferred_element_type=F32) + bd_ref[...]
        y_ref[...] = _pack_halves(y)

    @pl.when(t >= n_used)
    def _():
        y_ref[...] = jnp.zeros(y_ref.shape, y_ref.dtype)


def _used_tile(t, nu):
    return jnp.minimum(t, nu[0] - 1)


def _expert_call(xs, w_gu, b_gu, w_down, b_down, layer, tile_expert, n_used, next_expert, n_tiles):
    n_exp, d, ff2 = w_gu.shape[1:]
    ff = ff2 // 2
    tm = EXPERT_TILE
    sel = (jnp.arange(ff2, dtype=jnp.int32)[:, None] == 2 * jnp.arange(ff, dtype=jnp.int32)[None, :]).astype(BF16)
    grid_spec = pltpu.PrefetchScalarGridSpec(
        num_scalar_prefetch=3,
        grid=(n_tiles,),
        in_specs=[
            pl.BlockSpec((tm, d // 2), lambda t, te, nu, nx: (_used_tile(t, nu), 0)),
            pl.BlockSpec(memory_space=pl.ANY),
            pl.BlockSpec((None, None, 1, ff2), lambda t, te, nu, nx: (layer, te[t], 0, 0)),
            pl.BlockSpec((ff2, ff), lambda t, te, nu, nx: (0, 0)),
            pl.BlockSpec(memory_space=pl.ANY),
            pl.BlockSpec((None, None, 1, d), lambda t, te, nu, nx: (layer, te[t], 0, 0)),
        ],
        out_specs=pl.BlockSpec((tm, d // 2), lambda t, te, nu, nx: (t, 0)),
        scratch_shapes=[pltpu.VMEM((d, ff2), F32), pltpu.VMEM((ff, d), F32),
                        pltpu.VMEM((d, ff2), BF16), pltpu.VMEM((ff, d), BF16),
                        pltpu.SemaphoreType.DMA((2,))],
    )
    return pl.pallas_call(
        functools.partial(_expert_body, layer=layer),
        grid_spec=grid_spec,
        out_shape=jax.ShapeDtypeStruct((n_tiles * tm, d // 2), jnp.int32),
        compiler_params=_params(("arbitrary",), EXPERT_VMEM_LIMIT),
        name="expert_mlp",
    )(tile_expert, n_used, next_expert, xs, w_gu, b_gu.reshape(b_gu.shape[0], n_exp, 1, ff2), sel,
      w_down, b_down.reshape(b_down.shape[0], n_exp, 1, d))


def _combine_body(pos_ref, y_hbm, w_ref, x_ref, g_ref, *refs, tc, post):
    if post == "next":
        ng_ref, nsh_ref, nsc_ref, o_ref, h_ref, ybuf, sem = refs
    elif post == "final":
        ng_ref, o_ref, ybuf, sem = refs
    else:
        o_ref, ybuf, sem = refs
    t = pl.program_id(0)
    slot = lax.rem(t, 2)

    def gather(tile, s):
        base = tile * (tc * TOP_K)

        def body(r, carry):
            for k in range(TOP_K):
                row = pos_ref[base + r * TOP_K + k]
                pltpu.async_copy(y_hbm.at[pl.ds(row, 1), :], ybuf.at[s, k, pl.ds(r, 1), :], sem.at[s],
                                 priority=k % 2)
            return carry
        lax.fori_loop(0, tc, body, 0, unroll=2)

    @pl.when(t == 0)
    def _():
        gather(0, 0)

    @pl.when(t + 1 < pl.num_programs(0))
    def _():
        gather(t + 1, 1 - slot)

    for k in range(TOP_K):
        pltpu.make_async_copy(y_hbm.at[pl.ds(0, tc), :], ybuf.at[slot, k], sem.at[slot]).wait()
    half = ybuf.shape[-1]
    ch = COMBINE_CHUNK
    w = w_ref[...]
    wk = [jnp.broadcast_to(w[:, k:k + 1], (tc, LANES)) for k in range(TOP_K)]
    sq = jnp.zeros((tc, 1), F32)
    for c0 in range(0, half, ch):
        acc_lo = acc_hi = None
        for k in range(TOP_K):
            lo, hi = _unpack_halves(ybuf[slot, k, :, c0:c0 + ch])
            wt = jnp.tile(wk[k], (1, ch // LANES))
            acc_lo = wt * lo if k == 0 else acc_lo + wt * lo
            acc_hi = wt * hi if k == 0 else acc_hi + wt * hi
        for acc, c in ((acc_lo, c0), (acc_hi, half + c0)):
            new = x_ref[:, c:c + ch] + g_ref[:, c:c + ch] * acc
            o_ref[:, c:c + ch] = new
            if post is not None:
                sq = sq + jnp.sum(new * new, axis=-1, keepdims=True)
    if post is None:
        return
    inv = lax.rsqrt(sq * (1.0 / (2 * half)) + NORM_EPS)
    for c in range(0, 2 * half, ch):
        norm = o_ref[:, c:c + ch] * inv * ng_ref[:, c:c + ch]
        if post == "final":
            o_ref[:, c:c + ch] = norm
        else:
            h_ref[:, c:c + ch] = (norm * (1.0 + nsc_ref[:, c:c + ch]) + nsh_ref[:, c:c + ch]).astype(h_ref.dtype)


def _combine_call(y_sorted, slot, top_w, xa, modl, gate_idx, rows, seq, n_batch, post=None, post_args=()):
    d = xa.shape[1]
    tc = COMBINE_TILE
    assert rows % tc == 0 and seq % tc == 0
    tpb = seq // tc

    def seg_row(which):
        return lambda t, pos: (jnp.minimum(t // tpb, n_batch) * 6 + which, 0, 0)

    row_spec = pl.BlockSpec((tc, d), lambda t, pos: (t, 0))
    in_specs = [
        pl.BlockSpec(memory_space=pl.ANY),
        pl.BlockSpec((tc, LANES), lambda t, pos: (t, 0)),
        row_spec,
        pl.BlockSpec((None, 1, d), seg_row(gate_idx)),
    ]
    args = [slot, y_sorted, top_w, xa, modl]
    out_specs = row_spec
    out_shape = jax.ShapeDtypeStruct((rows, d), F32)
    if post is not None:
        in_specs.append(pl.BlockSpec((1, d), lambda t, pos: (0, 0)))
        args.append(post_args[0].reshape(1, d))
    if post == "next":
        in_specs += [pl.BlockSpec((None, 1, d), seg_row(0)), pl.BlockSpec((None, 1, d), seg_row(1))]
        args += [post_args[1], post_args[1]]
        out_specs = [row_spec, row_spec]
        out_shape = [out_shape, jax.ShapeDtypeStruct((rows, d), BF16)]
    grid_spec = pltpu.PrefetchScalarGridSpec(
        num_scalar_prefetch=1,
        grid=(rows // tc,),
        in_specs=in_specs,
        out_specs=out_specs,
        scratch_shapes=[pltpu.VMEM((2, TOP_K, tc, d // 2), jnp.int32), pltpu.SemaphoreType.DMA((2,))],
    )
    return pl.pallas_call(
        functools.partial(_combine_body, tc=tc, post=post),
        grid_spec=grid_spec,
        out_shape=out_shape,
        compiler_params=_params(("arbitrary",)),
        name="expert_combine",
    )(*args)


def _axial_angles(seq, head_dim):
    rows = seq // GRID_W
    row = jnp.repeat(jnp.arange(rows, dtype=F32), GRID_W)
    col = jnp.tile(jnp.arange(GRID_W, dtype=F32), rows)
    n_freq = head_dim // 4
    inv_freq = ROPE_BASE ** (-jnp.arange(n_freq, dtype=F32) / n_freq)
    return jnp.concatenate([row[:, None] * inv_freq, col[:, None] * inv_freq], axis=-1)


def _rope_tables(seq, ret_head_dim):
    ang_a = _axial_angles(seq, ATT_HEAD_DIM)
    reps = LANES // ATT_HEAD_DIM
    cos_a = jnp.tile(jnp.concatenate([jnp.cos(ang_a), jnp.cos(ang_a)], axis=-1), (1, reps))
    sin_a = jnp.tile(jnp.concatenate([-jnp.sin(ang_a), jnp.sin(ang_a)], axis=-1), (1, reps))
    ang_r = _axial_angles(seq, ret_head_dim)
    return (cos_a, sin_a), (jnp.cos(ang_r), jnp.sin(ang_r))


def kernel(x, c, ctx, c_ctx, ada_w, ada_b, norm_mix_g, norm_ffn_g, w_in, att_sink, ret_decay_fwd,
           ret_decay_bwd, w_branch_att, w_branch_ret, branch_gate_b, w_out, router_w, router_b,
           exp_w_gu, exp_b_gu, exp_w_down, exp_b_down, final_norm_g):
    n_batch, seq, d = x.shape
    n_ctx = ctx.shape[1]
    depth = ada_w.shape[0]
    n_exp, ff = exp_w_down.shape[1], exp_w_down.shape[2]
    att_width = w_branch_att.shape[1]
    ret_width = w_branch_ret.shape[1]
    kv_width = att_width // ATT_GROUP
    sizes = (("k_att", kv_width), ("v_att", kv_width), ("k_ret", ret_width), ("v_ret", ret_width),
             ("q_att", att_width), ("q_ret", ret_width), ("g_ret", ret_width),
             ("gate_att", d), ("gate_ret", d))
    cols, off = {}, 0
    kv_cols = 2 * kv_width + 2 * ret_width
    for name, width in sizes:
        cols[name] = off if off < kv_cols else off - kv_cols
        off += width
    in_cols = off
    assert in_cols == w_in.shape[2]

    lat_rows = n_batch * seq
    all_rows = lat_rows + n_batch * n_ctx
    xa = jnp.concatenate([x.reshape(lat_rows, d), ctx.reshape(n_batch * n_ctx, d)], axis=0)

    cond = jnp.zeros((16, d), F32).at[:n_batch].set(c).at[n_batch].set(c_ctx)
    mod = _mod_call(cond, ada_w, ada_b)
    rope_att, rope_ret = _rope_tables(seq, ret_width // RET_HEADS)

    modls = [mod[layer, :n_batch + 1].reshape((n_batch + 1) * 6, 1, d) for layer in range(depth)]
    h = _norm_call(xa, norm_mix_g[0], modls[0], all_rows, seq, n_batch, 0, 1)
    out = None
    for layer in range(depth):
        last = layer == depth - 1
        rows = lat_rows if last else all_rows
        modl = modls[layer]
        log_gamma = jnp.stack([jax.nn.log_sigmoid(ret_decay_fwd[layer].astype(F32)),
                               jax.nn.log_sigmoid(ret_decay_bwd[layer].astype(F32))])

        proj_kv = _inproj_call(h, w_in, layer, all_rows, 0, kv_cols)
        proj_q = _inproj_call(h, w_in, layer, rows, kv_cols, in_cols)

        ya = _attn_call(proj_kv, proj_q, att_sink[layer], rope_att, cols, n_batch, seq, n_ctx, att_width, True)
        yr = _ret_call(proj_kv, proj_q, log_gamma, rope_ret, cols, n_batch, seq, n_ctx, ret_width, True)
        if not last:
            ya_c = _attn_call(proj_kv, proj_q, att_sink[layer], None, cols, n_batch, seq, n_ctx, att_width, False)
            yr_c = _ret_call(proj_kv, proj_q, log_gamma, None, cols, n_batch, seq, n_ctx, ret_width, False)
            ya = jnp.concatenate([ya, ya_c], axis=0)
            yr = jnp.concatenate([yr, yr_c], axis=0)

        z = _merge_gate_call(ya, yr, proj_q, w_branch_att, w_branch_ret, branch_gate_b, layer, rows, cols)
        xa = _resid_matmul_call(z, w_out, layer, xa, modl, 2, rows, seq, n_batch)

        h2, top_idx, top_w = _norm_call(xa, norm_ffn_g[layer], modl, rows, seq, n_batch, 3, 4,
                                        router=(router_w[layer], router_b[layer]))
        n_tiles = -(-rows * TOP_K // EXPERT_TILE) + n_exp
        tile_expert, n_used, slot, last_tile, next_expert = _routing(
            top_idx[:, :TOP_K], n_exp, EXPERT_TILE, n_tiles)
        xs = _dispatch_call(h2, slot, last_tile, n_used, rows, n_tiles, n_exp)
        y_sorted = _expert_call(xs, exp_w_gu, exp_b_gu, exp_w_down, exp_b_down, layer,
                                tile_expert, n_used, next_expert, n_tiles)
        if last:
            out = _combine_call(y_sorted, slot, top_w, xa, modl, 5, rows, seq, n_batch,
                                post="final", post_args=(final_norm_g,))
        else:
            xa, h = _combine_call(y_sorted, slot, top_w, xa, modl, 5, rows, seq, n_batch,
                                  post="next", post_args=(norm_mix_g[layer + 1], modls[layer + 1]))
    return out.reshape(n_batch, seq, d)
```

```python
import functools
import math

import jax
import jax.numpy as jnp
from jax import lax
from jax.experimental import pallas as pl
from jax.experimental.pallas import tpu as pltpu

F32 = jnp.float32
BF16 = jnp.bfloat16

GRID_W = 64
ATT_HEAD_DIM = 64
ATT_GROUP = 4
ATT_WINDOW = 128
ATT_BLOCK = 128
ATT_PAIRS_PER_STEP = 4
RET_HEADS = 8
RET_CHUNK = 128
RET_UNROLL_BWD = 4
RET_UNROLL_FWD = 8
TOP_K = 4
SWIGLU_ALPHA = 1.702
SWIGLU_LIMIT = 7.0
ROPE_BASE = 10000.0
NORM_EPS = 1e-6
GN_EPS = 1e-5
NEG_INF = -1e30
LOG2_E = 1.4426950408889634

LANES = 128
VMEM_LIMIT = 56 * 1024 * 1024
EXPERT_VMEM_LIMIT = 60 * 1024 * 1024


def _params(sem, vmem=VMEM_LIMIT):
    return pltpu.CompilerParams(dimension_semantics=sem, vmem_limit_bytes=vmem)


def _tile(n, cap, mult):
    for t in range(min(cap, n), 0, -1):
        if n % t == 0 and t % mult == 0:
            return t
    raise ValueError(f"no tile for {n} (cap {cap}, multiple of {mult})")


def _silu(v):
    return v * jax.nn.sigmoid(v)


def _mod_body(c_ref, w_ref, b_ref, o_ref):
    s = _silu(c_ref[...]).astype(BF16)
    o_ref[...] = jnp.dot(s, w_ref[...].astype(BF16), preferred_element_type=F32) + b_ref[...]


def _mod_call(cond, ada_w, ada_b):
    depth, d, n = ada_w.shape
    rows = cond.shape[0]
    tn = _tile(n, 512, LANES)
    return pl.pallas_call(
        _mod_body,
        grid=(depth, n // tn),
        in_specs=[
            pl.BlockSpec((rows, d), lambda l, j: (0, 0)),
            pl.BlockSpec((None, d, tn), lambda l, j: (l, 0, j)),
            pl.BlockSpec((None, 1, tn), lambda l, j: (l, 0, j)),
        ],
        out_specs=pl.BlockSpec((None, rows, tn), lambda l, j: (l, 0, j)),
        out_shape=jax.ShapeDtypeStruct((depth, rows, n), F32),
        compiler_params=_params(("arbitrary", "arbitrary")),
        name="adaln_mod",
    )(cond, ada_w, ada_b.reshape(depth, 1, n))


def _rms(x, g):
    ms = jnp.mean(x * x, axis=-1, keepdims=True)
    return x * lax.rsqrt(ms + NORM_EPS) * g


def _row_sources(src, tr, block_cols, col_of, row_axis):
    if not isinstance(src, tuple):
        return [src], [pl.BlockSpec((tr, block_cols), lambda *g: (g[row_axis], col_of(*g)))], None
    lat, cx = src
    assert lat.shape[0] % tr == 0 and cx.shape[0] % tr == 0
    n_lat = lat.shape[0] // tr
    specs = [pl.BlockSpec((tr, block_cols), lambda *g: (jnp.minimum(g[row_axis], n_lat - 1), col_of(*g))),
             pl.BlockSpec((tr, block_cols), lambda *g: (jnp.maximum(g[row_axis] - n_lat, 0), col_of(*g)))]
    return [lat, cx], specs, n_lat


def _read_rows(refs, n_lat, row_axis):
    if n_lat is None:
        return refs[0][...]
    return jnp.where(pl.program_id(row_axis) < n_lat, refs[0][...], refs[1][...])


def _norm_mod_body(*refs, n_lat):
    n_src = 1 if n_lat is None else 2
    g_ref, sh_ref, sc_ref, h_ref = refs[n_src:]
    h = _rms(_read_rows(refs[:n_src], n_lat, 0), g_ref[...]) * (1.0 + sc_ref[...]) + sh_ref[...]
    h_ref[...] = h.astype(h_ref.dtype)


def _pack_halves(v):
    w = v.shape[1] // 2
    bits = lax.bitcast_convert_type(v.astype(BF16).astype(F32), jnp.int32)
    return lax.shift_right_logical(bits[:, :w], 16) | (bits[:, w:] & jnp.int32(-65536))


def _unpack_halves(p):
    lo = lax.bitcast_convert_type(lax.shift_left(p, 16), F32)
    hi = lax.bitcast_convert_type(p & jnp.int32(-65536), F32)
    return lo, hi


def _split_bf16(v):
    hi = v.astype(BF16)
    lo = (v - hi.astype(F32)).astype(BF16)
    return hi, lo


def _norm_router_body(x_ref, g_ref, sh_ref, sc_ref, rw_ref, rb_ref, h_ref, idx_ref, wt_ref):
    h = _rms(x_ref[...], g_ref[...]) * (1.0 + sc_ref[...]) + sh_ref[...]
    h_ref[...] = _pack_halves(h)
    h_hi, h_lo = _split_bf16(h)
    w_hi, w_lo = _split_bf16(rw_ref[...])
    logits = (jnp.dot(h_hi, w_hi, preferred_element_type=F32)
              + jnp.dot(h_hi, w_lo, preferred_element_type=F32)
              + jnp.dot(h_lo, w_hi, preferred_element_type=F32)) + rb_ref[...]
    n_exp = logits.shape[-1]
    lane = lax.broadcasted_iota(jnp.int32, logits.shape, 1).astype(F32)
    work = logits
    out_lane = lax.broadcasted_iota(jnp.int32, idx_ref.shape, 1)
    idx = jnp.zeros(idx_ref.shape, F32)
    ex = jnp.zeros(wt_ref.shape, F32)
    top = None
    total = None
    for k in range(TOP_K):
        m = jnp.max(work, axis=-1, keepdims=True)
        first = jnp.min(jnp.where(work == m, lane, float(n_exp)), axis=-1, keepdims=True)
        work = jnp.where(lane == first, -jnp.inf, work)
        if k == 0:
            top = m
        e = jnp.exp(m - top)
        total = e if k == 0 else total + e
        idx = jnp.where(out_lane == k, first, idx)
        ex = jnp.where(out_lane == k, e, ex)
    idx_ref[...] = idx.astype(jnp.int32)
    wt_ref[...] = ex / total


def _seg_map(tiles_per_batch, n_batch, which):
    def index(i):
        return (jnp.minimum(i // tiles_per_batch, n_batch) * 6 + which, 0, 0)
    return index


def _norm_call(xa, g, modl, rows, seq, n_batch, shift_idx, scale_idx, router=None):
    d = g.shape[0]
    tr = 256
    assert rows % tr == 0 and seq % tr == 0
    tpb = seq // tr
    row_spec = pl.BlockSpec((tr, d), lambda i: (i, 0))
    srcs, src_specs, n_lat = _row_sources(xa, tr, d, lambda i: 0, 0)
    in_specs = src_specs + [
        pl.BlockSpec((1, d), lambda i: (0, 0)),
        pl.BlockSpec((None, 1, d), _seg_map(tpb, n_batch, shift_idx)),
        pl.BlockSpec((None, 1, d), _seg_map(tpb, n_batch, scale_idx)),
    ]
    args = srcs + [g.reshape(1, d), modl, modl]
    if router is None:
        return pl.pallas_call(
            functools.partial(_norm_mod_body, n_lat=n_lat), grid=(rows // tr,), in_specs=in_specs,
            out_specs=row_spec, out_shape=jax.ShapeDtypeStruct((rows, d), BF16),
            compiler_params=_params(("arbitrary",)), name="norm_mod",
        )(*args)
    assert n_lat is None
    rw, rb = router
    n_exp = rw.shape[1]
    in_specs += [pl.BlockSpec((d, n_exp), lambda i: (0, 0)), pl.BlockSpec((1, n_exp), lambda i: (0, 0))]
    args += [rw, rb.reshape(1, n_exp)]
    return pl.pallas_call(
        _norm_router_body, grid=(rows // tr,), in_specs=in_specs,
        out_specs=[pl.BlockSpec((tr, d // 2), lambda i: (i, 0)),
                   pl.BlockSpec((tr, LANES), lambda i: (i, 0)), pl.BlockSpec((tr, LANES), lambda i: (i, 0))],
        out_shape=[jax.ShapeDtypeStruct((rows, d // 2), jnp.int32), jax.ShapeDtypeStruct((rows, LANES), jnp.int32),
                   jax.ShapeDtypeStruct((rows, LANES), F32)],
        compiler_params=_params(("arbitrary",)), name="norm_router",
    )(*args)


def _inproj_body(a_ref, w_ref, o_ref, wb_ref):
    @pl.when(pl.program_id(1) == 0)
    def _():
        wb_ref[...] = w_ref[...].astype(BF16)
    o_ref[...] = jnp.dot(a_ref[...], wb_ref[...], preferred_element_type=F32).astype(o_ref.dtype)


def _inproj_call(h, w_in, layer, rows, col_lo, col_hi):
    d = h.shape[1]
    tn = 1024
    assert col_lo % tn == 0 and col_hi % tn == 0
    tm = _tile(rows, 576, 16)
    j0 = col_lo // tn
    return pl.pallas_call(
        _inproj_body,
        grid=((col_hi - col_lo) // tn, rows // tm),
        in_specs=[
            pl.BlockSpec((tm, d), lambda j, i: (i, 0)),
            pl.BlockSpec((None, d, tn), lambda j, i: (layer, 0, j0 + j)),
        ],
        out_specs=pl.BlockSpec((tm, tn), lambda j, i: (i, j)),
        out_shape=jax.ShapeDtypeStruct((rows, col_hi - col_lo), BF16),
        scratch_shapes=[pltpu.VMEM((d, tn), BF16)],
        compiler_params=_params(("arbitrary", "arbitrary")),
        name="in_proj",
    )(h, w_in)


def _attn_body(*refs, local, scale):
    if local:
        (sink_ref, q_ref, kp_ref, kc_ref, kn_ref, vp_ref, vc_ref, vn_ref, kx_ref, vx_ref,
         cp_ref, cc_ref, cn_ref, sp_ref, sc_ref, sn_ref, bias_ref, o_ref) = refs
    else:
        sink_ref, q_ref, kx_ref, vx_ref, o_ref = refs
    blk = ATT_BLOCK
    half = ATT_HEAD_DIM // 2
    pair_w = 2 * ATT_HEAD_DIM
    n_pairs = kx_ref.shape[1] // pair_w
    slabs_per_pair = ATT_GROUP

    lane_q = lax.broadcasted_iota(jnp.int32, (blk, LANES), 1)
    first_half = (lane_q % ATT_HEAD_DIM) < half

    def rope(v, c_ref, s_ref):
        rot = jnp.where(first_half, pltpu.roll(v, LANES - half, 1), pltpu.roll(v, half, 1))
        return v * c_ref[...] + rot * s_ref[...]

    def placed(v):
        low = lax.broadcasted_iota(jnp.int32, v.shape, 1) < ATT_HEAD_DIM
        swapped = pltpu.roll(v, ATT_HEAD_DIM, 1)
        return [[jnp.where(low, v, 0.0).astype(BF16), jnp.where(low, 0.0, swapped).astype(BF16)],
                [jnp.where(low, swapped, 0.0).astype(BF16), jnp.where(low, 0.0, v).astype(BF16)]]

    top_rows = lax.broadcasted_iota(jnp.int32, (2 * blk, 1), 0) < blk
    work = [(pp, j, hh) for pp in range(n_pairs) for j in range(2) for hh in range(2)]

    v_vars, scores = [], []
    for pp in range(n_pairs):
        lanes = slice(pp * pair_w, (pp + 1) * pair_w)
        if local:
            keys = jnp.concatenate([
                rope(kp_ref[:, lanes].astype(F32), cp_ref, sp_ref),
                rope(kc_ref[:, lanes].astype(F32), cc_ref, sc_ref),
                rope(kn_ref[:, lanes].astype(F32), cn_ref, sn_ref),
                kx_ref[:, lanes].astype(F32)], axis=0)
            vals = jnp.concatenate([vp_ref[:, lanes], vc_ref[:, lanes], vn_ref[:, lanes], vx_ref[:, lanes]],
                                   axis=0).astype(F32)
        else:
            keys = kx_ref[:, lanes].astype(F32)
            vals = vx_ref[:, lanes].astype(F32)
        k_var = placed(keys)
        v_vars.append(placed(vals))
        q_slabs = []
        for c in range(slabs_per_pair):
            c0 = (pp * slabs_per_pair + c) * LANES
            qc = q_ref[:, c0:c0 + LANES].astype(F32)
            if local:
                qc = rope(qc, cc_ref, sc_ref)
            q_slabs.append((qc * (scale * LOG2_E)).astype(BF16))
        for j in range(2):
            lhs = jnp.concatenate([q_slabs[2 * j], q_slabs[2 * j + 1]], axis=0)
            for hh in range(2):
                s = lax.dot_general(lhs, k_var[j][hh], (((1,), (1,)), ((), ())),
                                    preferred_element_type=F32)
                if local:
                    s = s + bias_ref[...]
                scores.append(s)

    probs = []
    for (pp, j, hh), s in zip(work, scores):
        head0 = (pl.program_id(1) * n_pairs + pp) * (2 * ATT_GROUP) + ATT_GROUP * j + hh
        sink = jnp.where(top_rows, sink_ref[head0], sink_ref[head0 + 2]) * LOG2_E
        m = jnp.maximum(jnp.max(s, axis=-1, keepdims=True), sink)
        e = jnp.exp2(s - m)
        denom = jnp.sum(e, axis=-1, keepdims=True) + jnp.exp2(sink - m)
        probs.append((e.astype(BF16), 1.0 / denom))

    out = [jnp.zeros((blk, LANES), F32) for _ in range(n_pairs * slabs_per_pair)]
    for (pp, j, hh), (e, inv) in zip(work, probs):
        o = jnp.dot(e, v_vars[pp][j][hh], preferred_element_type=F32) * inv
        c = pp * slabs_per_pair + 2 * j
        out[c] = out[c] + o[:blk]
        out[c + 1] = out[c + 1] + o[blk:]
    for c, val in enumerate(out):
        o_ref[:, c * LANES:(c + 1) * LANES] = val.astype(o_ref.dtype)


def _attn_call(proj_kv, proj_q, sink, rope_tabs, cols, n_batch, seq, n_ctx, att_width, local):
    blk = ATT_BLOCK
    nb = seq // blk
    pair_w = 2 * ATT_HEAD_DIM * ATT_PAIRS_PER_STEP
    q_w = pair_w * ATT_GROUP
    assert att_width % q_w == 0
    n_pair = att_width // q_w
    k0 = cols["k_att"] // pair_w
    v0 = cols["v_att"] // pair_w
    q0 = cols["q_att"] // q_w
    ctx_row0 = (n_batch * seq) // n_ctx
    kx_spec = pl.BlockSpec((n_ctx, pair_w), lambda b, p, n: (ctx_row0 + b, k0 + p))
    vx_spec = pl.BlockSpec((n_ctx, pair_w), lambda b, p, n: (ctx_row0 + b, v0 + p))
    sink_spec = pl.BlockSpec(memory_space=pltpu.SMEM)
    scale = ATT_HEAD_DIM ** -0.5
    if local:
        cos_t, sin_t = rope_tabs

        def prev(n):
            return jnp.maximum(n - 1, 0)

        def nxt(n):
            return jnp.minimum(n + 1, nb - 1)

        def kv(c0, f):
            return pl.BlockSpec((blk, pair_w), lambda b, p, n: (b * nb + f(n), c0 + p))

        def tab(f):
            return pl.BlockSpec((blk, LANES), lambda b, p, n: (f(n), 0))

        ident = lambda n: n
        assert nb >= 2
        r = jnp.arange(2 * blk, dtype=jnp.int32)[:, None] % blk
        c = jnp.arange(3 * blk + n_ctx, dtype=jnp.int32)[None, :]
        band = (jnp.abs(c - blk - r) <= ATT_WINDOW) | (c >= 3 * blk)
        edge = jnp.stack([band & (c >= blk), band, band & ((c < 2 * blk) | (c >= 3 * blk))])
        bias = jnp.where(edge, 0.0, NEG_INF).astype(F32)
        bias_spec = pl.BlockSpec((None, 2 * blk, 3 * blk + n_ctx),
                                 lambda b, p, n: ((n > 0).astype(jnp.int32) + (n == nb - 1).astype(jnp.int32), 0, 0))
        in_specs = [sink_spec,
                    pl.BlockSpec((blk, q_w), lambda b, p, n: (b * nb + n, q0 + p)),
                    kv(k0, prev), kv(k0, ident), kv(k0, nxt),
                    kv(v0, prev), kv(v0, ident), kv(v0, nxt),
                    kx_spec, vx_spec,
                    tab(prev), tab(ident), tab(nxt), tab(prev), tab(ident), tab(nxt), bias_spec]
        args = [sink, proj_q] + [proj_kv] * 6 + [proj_kv, proj_kv] + [cos_t] * 3 + [sin_t] * 3 + [bias]
        grid = (n_batch, n_pair, nb)
        rows = n_batch * seq
        out_spec = pl.BlockSpec((blk, q_w), lambda b, p, n: (b * nb + n, p))
    else:
        nqb = n_ctx // blk
        qrow0 = (n_batch * seq) // blk
        in_specs = [sink_spec,
                    pl.BlockSpec((blk, q_w), lambda b, p, n: (qrow0 + b * nqb + n, q0 + p)),
                    kx_spec, vx_spec]
        args = [sink, proj_q, proj_kv, proj_kv]
        grid = (n_batch, n_pair, nqb)
        rows = n_batch * n_ctx
        out_spec = pl.BlockSpec((blk, q_w), lambda b, p, n: (b * nqb + n, p))
    return pl.pallas_call(
        functools.partial(_attn_body, local=local, scale=scale),
        grid=grid, in_specs=in_specs, out_specs=out_spec,
        out_shape=jax.ShapeDtypeStruct((rows, att_width), BF16),
        compiler_params=_params(("arbitrary",) * 3),
        name="window_attn" if local else "ctx_attn",
    )(*args)


def _dot_tn(a, b):
    return lax.dot_general(a, b, (((0,), (0,)), ((), ())), preferred_element_type=F32)


def _ret_body(*refs, latent, n_chunks, n_ctx, k_scale):
    if latent:
        (lg_ref, q_ref, k_ref, v_ref, g_ref, kx_ref, vx_ref, cos_ref, sin_ref,
         o_ref, qs_ref, ks_ref, cb_ref, st_ref, sf_ref) = refs
    else:
        lg_ref, q_ref, k_ref, v_ref, g_ref, o_ref, qs_ref, ks_ref, cb_ref, st_ref, sf_ref = refs
    head = pl.program_id(1)
    lg_f = lg_ref[0, head]
    lg_b = lg_ref[1, head]
    c = RET_CHUNK
    hd = q_ref.shape[1]
    hh = hd // 2
    idx = lax.broadcasted_iota(jnp.int32, (c, 1), 0).astype(F32)
    qdec_f = jnp.exp(lg_f * (idx + 1.0))
    kdec_f = jnp.exp(lg_f * (c - 1.0 - idx))
    qdec_b = jnp.exp(lg_b * (c - idx))
    kdec_b = jnp.exp(lg_b * idx)
    one = jnp.ones((1, 1), F32)
    cdec_f = jnp.exp(lg_f * c * one)
    cdec_b = jnp.exp(lg_b * c * one)
    rel = (lax.broadcasted_iota(jnp.int32, (c, c), 0) - lax.broadcasted_iota(jnp.int32, (c, c), 1)).astype(F32)
    dmat = (jnp.where(rel >= 0, jnp.exp(lg_f * jnp.maximum(rel, 0.0)), 0.0)
            + jnp.where(rel <= 0, jnp.exp(lg_b * jnp.maximum(-rel, 0.0)), 0.0))

    def rope(v, rows):
        v1 = v[:, :hh]
        v2 = v[:, hh:]
        cs = cos_ref[rows, :]
        sn = sin_ref[rows, :]
        return jnp.concatenate([v1 * cs - v2 * sn, v2 * cs + v1 * sn], axis=1)

    if latent:
        t = lax.broadcasted_iota(jnp.int32, (n_ctx, 1), 0).astype(F32)
        kx = kx_ref[...].astype(F32) * k_scale
        sf_ref[...] = _dot_tn((kx * jnp.exp(lg_f * (n_ctx - 1.0 - t))).astype(BF16), vx_ref[...])
        st_ref[...] = _dot_tn((kx * jnp.exp(lg_b * t)).astype(BF16), vx_ref[...])
    else:
        sf_ref[...] = jnp.zeros((hd, hd), F32)
        st_ref[...] = jnp.zeros((hd, hd), F32)


    def bwd(step, carry):
        rows = pl.ds(pl.multiple_of((n_chunks - 1 - step) * c, c), c)
        q = q_ref[rows, :].astype(F32)
        k = k_ref[rows, :].astype(F32)
        if latent:
            q = rope(q, rows)
            k = rope(k, rows)
        k = k * k_scale
        qs_ref[rows, :] = q.astype(BF16)
        ks_ref[rows, :] = k.astype(BF16)
        state = st_ref[...]
        cb_ref[rows, :] = jnp.dot((q * qdec_b).astype(BF16), state.astype(BF16), preferred_element_type=F32)
        st_ref[...] = cdec_b * state + _dot_tn((k * kdec_b).astype(BF16), v_ref[rows, :])
        return carry

    lax.fori_loop(0, n_chunks, bwd, 0, unroll=min(RET_UNROLL_BWD, n_chunks))

    st_ref[...] = sf_ref[...]

    def fwd(step, carry):
        rows = pl.ds(pl.multiple_of(step * c, c), c)
        qb = qs_ref[rows, :]
        kb = ks_ref[rows, :]
        v = v_ref[rows, :]
        scores = lax.dot_general(qb, kb, (((1,), (1,)), ((), ())), preferred_element_type=F32) * dmat
        o = jnp.dot(scores.astype(BF16), v, preferred_element_type=F32)
        state = st_ref[...]
        o = o + jnp.dot((qb.astype(F32) * qdec_f).astype(BF16), state.astype(BF16), preferred_element_type=F32)
        o = o + cb_ref[rows, :]
        st_ref[...] = cdec_f * state + _dot_tn((kb.astype(F32) * kdec_f).astype(BF16), v)
        mu = jnp.mean(o, axis=-1, keepdims=True)
        dev = o - mu
        var = jnp.mean(dev * dev, axis=-1, keepdims=True)
        y = dev * lax.rsqrt(var + GN_EPS)
        o_ref[rows, :] = (_silu(g_ref[rows, :].astype(F32)) * y).astype(o_ref.dtype)
        return carry

    lax.fori_loop(0, n_chunks, fwd, 0, unroll=min(RET_UNROLL_FWD, n_chunks))


def _ret_call(proj_kv, proj_q, log_gamma, rope_tabs, cols, n_batch, seq, n_ctx, ret_width, latent):
    hd = ret_width // RET_HEADS
    t_len = seq if latent else n_ctx
    assert t_len % RET_CHUNK == 0
    k0, v0, q0, g0 = (cols[name] // hd for name in ("k_ret", "v_ret", "q_ret", "g_ret"))
    ctx_row0 = (n_batch * seq) // n_ctx
    row0 = 0 if latent else ctx_row0

    def seq_spec(c0):
        return pl.BlockSpec((t_len, hd), lambda b, h: (row0 + b, c0 + h))

    in_specs = [pl.BlockSpec(memory_space=pltpu.SMEM),
                seq_spec(q0), seq_spec(k0), seq_spec(v0), seq_spec(g0)]
    args = [log_gamma, proj_q, proj_kv, proj_kv, proj_q]
    if latent:
        cos_t, sin_t = rope_tabs
        in_specs += [pl.BlockSpec((n_ctx, hd), lambda b, h: (ctx_row0 + b, k0 + h)),
                     pl.BlockSpec((n_ctx, hd), lambda b, h: (ctx_row0 + b, v0 + h)),
                     pl.BlockSpec((t_len, hd // 2), lambda b, h: (0, 0)),
                     pl.BlockSpec((t_len, hd // 2), lambda b, h: (0, 0))]
        args += [proj_kv, proj_kv, cos_t, sin_t]
    return pl.pallas_call(
        functools.partial(_ret_body, latent=latent, n_chunks=t_len // RET_CHUNK, n_ctx=n_ctx,
                          k_scale=hd ** -0.5),
        grid=(n_batch, RET_HEADS), in_specs=in_specs,
        out_specs=pl.BlockSpec((t_len, hd), lambda b, h: (b, h)),
        out_shape=jax.ShapeDtypeStruct((n_batch * t_len, ret_width), BF16),
        scratch_shapes=[pltpu.VMEM((t_len, hd), BF16), pltpu.VMEM((t_len, hd), BF16),
                        pltpu.VMEM((t_len, hd), F32), pltpu.VMEM((hd, hd), F32), pltpu.VMEM((hd, hd), F32)],
        compiler_params=_params(("arbitrary", "arbitrary")),
        name="retention" if latent else "ctx_retention",
    )(*args)


def _merge_gate_body(*refs, n_lat):
    n_src = 1 if n_lat is None else 2
    ga_ref, gr_ref, wa_ref, wr_ref, ba_ref, br_ref, z_ref, wab_ref, wrb_ref = refs[2 * n_src:]

    @pl.when(pl.program_id(1) == 0)
    def _():
        wab_ref[...] = wa_ref[...].astype(BF16)
        wrb_ref[...] = wr_ref[...].astype(BF16)
    a = jnp.dot(_read_rows(refs[:n_src], n_lat, 1), wab_ref[...], preferred_element_type=F32)
    r = jnp.dot(_read_rows(refs[n_src:2 * n_src], n_lat, 1), wrb_ref[...], preferred_element_type=F32)
    ga = jax.nn.sigmoid(ga_ref[...].astype(F32) + ba_ref[...])
    gr = jax.nn.sigmoid(gr_ref[...].astype(F32) + br_ref[...])
    z_ref[...] = (ga * a + gr * r).astype(z_ref.dtype)


def _merge_gate_call(ya, yr, proj, w_att, w_ret, gate_b, layer, rows, cols):
    d = w_att.shape[2]
    wa_rows, wr_rows = w_att.shape[1], w_ret.shape[1]
    tn = 512
    if isinstance(ya, tuple):
        tm = _tile(math.gcd(ya[0].shape[0], ya[1].shape[0]), 1024, 16)
    else:
        tm = _tile(rows, 1024, 16)
    ga0 = cols["gate_att"] // tn
    gr0 = cols["gate_ret"] // tn
    gb = gate_b.reshape(gate_b.shape[0], 1, 2 * d)
    ya_srcs, ya_specs, n_lat = _row_sources(ya, tm, wa_rows, lambda j, i: 0, 1)
    yr_srcs, yr_specs, _ = _row_sources(yr, tm, wr_rows, lambda j, i: 0, 1)
    return pl.pallas_call(
        functools.partial(_merge_gate_body, n_lat=n_lat),
        grid=(d // tn, rows // tm),
        in_specs=ya_specs + yr_specs + [
            pl.BlockSpec((tm, tn), lambda j, i: (i, ga0 + j)),
            pl.BlockSpec((tm, tn), lambda j, i: (i, gr0 + j)),
            pl.BlockSpec((None, wa_rows, tn), lambda j, i: (layer, 0, j)),
            pl.BlockSpec((None, wr_rows, tn), lambda j, i: (layer, 0, j)),
            pl.BlockSpec((None, 1, tn), lambda j, i: (layer, 0, j)),
            pl.BlockSpec((None, 1, tn), lambda j, i: (layer, 0, d // tn + j)),
        ],
        out_specs=pl.BlockSpec((tm, tn), lambda j, i: (i, j)),
        out_shape=jax.ShapeDtypeStruct((rows, d), BF16),
        scratch_shapes=[pltpu.VMEM((wa_rows, tn), BF16), pltpu.VMEM((wr_rows, tn), BF16)],
        compiler_params=_params(("arbitrary", "arbitrary")),
        name="merge_gate",
    )(*ya_srcs, *yr_srcs, proj, proj, w_att, w_ret, gb, gb)


def _resid_matmul_body(z_ref, w_ref, *refs, n_lat):
    n_src = 1 if n_lat is None else 2
    g_ref, o_ref, wb_ref = refs[n_src:]

    @pl.when(pl.program_id(1) == 0)
    def _():
        wb_ref[...] = w_ref[...].astype(BF16)
    y = jnp.dot(z_ref[...], wb_ref[...], preferred_element_type=F32)
    o_ref[...] = _read_rows(refs[:n_src], n_lat, 1) + g_ref[...] * y


def _resid_matmul_call(z, w, layer, xa, modl, gate_idx, rows, seq, n_batch):
    k, d = w.shape[1], w.shape[2]
    tn = 512
    tm = 512
    assert rows % tm == 0 and seq % tm == 0
    tpb = seq // tm

    def gate_map(j, i):
        return (jnp.minimum(i // tpb, n_batch) * 6 + gate_idx, 0, j)

    srcs, src_specs, n_lat = _row_sources(xa, tm, tn, lambda j, i: j, 1)
    return pl.pallas_call(
        functools.partial(_resid_matmul_body, n_lat=n_lat),
        grid=(d // tn, rows // tm),
        in_specs=[
            pl.BlockSpec((tm, k), lambda j, i: (i, 0)),
            pl.BlockSpec((None, k, tn), lambda j, i: (layer, 0, j)),
        ] + src_specs + [pl.BlockSpec((None, 1, tn), gate_map)],
        out_specs=pl.BlockSpec((tm, tn), lambda j, i: (i, j)),
        out_shape=jax.ShapeDtypeStruct((rows, d), F32),
        scratch_shapes=[pltpu.VMEM((k, tn), BF16)],
        compiler_params=_params(("arbitrary", "arbitrary")),
        name="resid_matmul",
    )(z, w, *srcs, modl)


EXPERT_TILE = 256
COMBINE_TILE = 128
DISPATCH_TILE = 256
COMBINE_CHUNK = 256


def _routing(top_idx, n_exp, tm, n_tiles):
    n_assign = top_idx.size
    e_flat = top_idx.reshape(n_assign)
    onehot = (e_flat[:, None] == jnp.arange(n_exp, dtype=jnp.int32)[None, :]).astype(jnp.int32)
    running = jnp.cumsum(onehot, axis=0)
    counts = running[-1]
    group = (counts + tm - 1) // tm * tm
    group_end = jnp.cumsum(group)
    slot = jnp.sum(onehot * (running - 1 + (group_end - group)[None, :]), axis=1).astype(jnp.int32)
    n_used = (group_end[-1] // tm).astype(jnp.int32).reshape(1)
    tile_start = jnp.arange(n_tiles, dtype=jnp.int32) * tm
    tile_expert = jnp.sum((group_end[None, :] <= tile_start[:, None]).astype(jnp.int32), axis=1)
    tile_expert = jnp.minimum(tile_expert, n_exp - 1).astype(jnp.int32)
    last_tile = jnp.stack([jnp.maximum(group_end - tm, 0), (group > 0).astype(jnp.int32)], axis=1)
    ids = jnp.arange(n_exp, dtype=jnp.int32)
    later = (ids[None, :] > ids[:, None]) & (group[None, :] > 0)
    next_of = jnp.min(jnp.where(later, ids[None, :], n_exp), axis=1)
    next_of = jnp.where(next_of == n_exp, -1, next_of)
    next_expert = jnp.sum((tile_expert[:, None] == ids[None, :]) * next_of[None, :], axis=1).astype(jnp.int32)
    return tile_expert, n_used, slot, last_tile.reshape(2 * n_exp).astype(jnp.int32), next_expert


def _expert_changed(te_ref, t):
    return jnp.logical_or(t == 0, te_ref[t] != te_ref[jnp.maximum(t - 1, 0)])


def _dispatch_body(slot_ref, last_ref, nu_ref, h_ref, xs_hbm, zero_ref, sem, zsem, *, tr, tm, n_exp, n_tiles):
    t = pl.program_id(0)

    @pl.when(t == 0)
    def _():
        zero_ref[...] = jnp.zeros(zero_ref.shape, zero_ref.dtype)

        def zero_copy(start):
            return pltpu.make_async_copy(zero_ref, xs_hbm.at[pl.ds(pl.multiple_of(start, tm), tm), :], zsem)

        def start_one(e, carry):
            @pl.when(last_ref[2 * e + 1] == 1)
            def _():
                zero_copy(last_ref[2 * e]).start()
            return carry

        def wait_one(e, carry):
            @pl.when(last_ref[2 * e + 1] == 1)
            def _():
                zero_copy(last_ref[2 * e]).wait()
            return carry

        def start_tail(i, carry):
            zero_copy(i * tm).start()
            return carry

        def wait_tail(i, carry):
            zero_copy(i * tm).wait()
            return carry

        lax.fori_loop(0, n_exp, start_one, 0)
        lax.fori_loop(nu_ref[0], n_tiles, start_tail, 0)
        lax.fori_loop(0, n_exp, wait_one, 0)
        lax.fori_loop(nu_ref[0], n_tiles, wait_tail, 0)

    base = t * (tr * TOP_K)

    def body(r, carry):
        for k in range(TOP_K):
            row = slot_ref[base + r * TOP_K + k]
            pltpu.async_copy(h_ref.at[pl.ds(r, 1), :], xs_hbm.at[pl.ds(row, 1), :], sem, priority=k % 2)
        return carry
    lax.fori_loop(0, tr, body, 0, unroll=2)

    for _ in range(TOP_K):
        pltpu.make_async_copy(h_ref, xs_hbm.at[pl.ds(0, tr), :], sem).wait()


def _dispatch_call(h2p, slot, last_tile, n_used, rows, n_tiles, n_exp):
    w = h2p.shape[1]
    tr = DISPATCH_TILE
    tm = EXPERT_TILE
    assert rows % tr == 0
    grid_spec = pltpu.PrefetchScalarGridSpec(
        num_scalar_prefetch=3,
        grid=(rows // tr,),
        in_specs=[pl.BlockSpec((tr, w), lambda t, sl, lt, nu: (t, 0))],
        out_specs=pl.BlockSpec(memory_space=pl.ANY),
        scratch_shapes=[pltpu.VMEM((tm, w), jnp.int32), pltpu.SemaphoreType.DMA(()), pltpu.SemaphoreType.DMA(())],
    )
    return pl.pallas_call(
        functools.partial(_dispatch_body, tr=tr, tm=tm, n_exp=n_exp, n_tiles=n_tiles),
        grid_spec=grid_spec,
        out_shape=jax.ShapeDtypeStruct((n_tiles * tm, w), jnp.int32),
        compiler_params=_params(("arbitrary",)),
        name="expert_dispatch",
    )(slot, last_tile, n_used, h2p)


def _expert_body(te_ref, nu_ref, nx_ref, x_ref, wgu_hbm, bgu_ref, sel_ref, wd_hbm, bd_ref, y_ref,
                 sgu_ref, sd_ref, wgu_ref, wd_ref, sem, *, layer):
    t = pl.program_id(0)
    n_used = nu_ref[0]

    def weight_copies(e):
        return (pltpu.make_async_copy(wgu_hbm.at[layer, e], sgu_ref, sem.at[0]),
                pltpu.make_async_copy(wd_hbm.at[layer, e], sd_ref, sem.at[1]))

    def fetch(e):
        for copy in weight_copies(e):
            copy.start()

    @pl.when(t == 0)
    def _():
        fetch(te_ref[0])

    @pl.when(t < n_used)
    def _():
        @pl.when(_expert_changed(te_ref, t))
        def _():
            for copy in weight_copies(te_ref[t]):
                copy.wait()
            wgu_ref[...] = sgu_ref[...].astype(BF16)
            wd_ref[...] = sd_ref[...].astype(BF16)

            @pl.when(nx_ref[t] >= 0)
            def _():
                fetch(nx_ref[t])

        x_lo, x_hi = _unpack_halves(x_ref[...])
        half = x_lo.shape[1]
        gu = (jnp.dot(x_lo.astype(BF16), wgu_ref[:half, :], preferred_element_type=F32)
              + jnp.dot(x_hi.astype(BF16), wgu_ref[half:, :], preferred_element_type=F32)) + bgu_ref[...]
        n2 = gu.shape[1]
        even = lax.broadcasted_iota(jnp.int32, gu.shape, 1) % 2 == 0
        gate = jnp.minimum(gu, SWIGLU_LIMIT)
        up = jnp.clip(pltpu.roll(gu, n2 - 1, 1), -SWIGLU_LIMIT, SWIGLU_LIMIT)
        act = jnp.where(even, (up + 1.0) * (gate * jax.nn.sigmoid(SWIGLU_ALPHA * gate)), 0.0)
        a = jnp.dot(act.astype(BF16), sel_ref[...], preferred_element_type=F32).astype(BF16)
        y = jnp.dot(a, wd_ref[...], preferred_element_type=F32) + bd_ref[...]
        y_ref[...] = _pack_halves(y)

    @pl.when(t >= n_used)
    def _():
        y_ref[...] = jnp.zeros(y_ref.shape, y_ref.dtype)


def _used_tile(t, nu):
    return jnp.minimum(t, nu[0] - 1)


def _expert_call(xs, w_gu, b_gu, w_down, b_down, layer, tile_expert, n_used, next_expert, n_tiles):
    n_exp, d, ff2 = w_gu.shape[1:]
    ff = ff2 // 2
    tm = EXPERT_TILE
    sel = (jnp.arange(ff2, dtype=jnp.int32)[:, None] == 2 * jnp.arange(ff, dtype=jnp.int32)[None, :]).astype(BF16)
    grid_spec = pltpu.PrefetchScalarGridSpec(
        num_scalar_prefetch=3,
        grid=(n_tiles,),
        in_specs=[
            pl.BlockSpec((tm, d // 2), lambda t, te, nu, nx: (_used_tile(t, nu), 0)),
            pl.BlockSpec(memory_space=pl.ANY),
            pl.BlockSpec((None, None, 1, ff2), lambda t, te, nu, nx: (layer, te[t], 0, 0)),
            pl.BlockSpec((ff2, ff), lambda t, te, nu, nx: (0, 0)),
            pl.BlockSpec(memory_space=pl.ANY),
            pl.BlockSpec((None, None, 1, d), lambda t, te, nu, nx: (layer, te[t], 0, 0)),
        ],
        out_specs=pl.BlockSpec((tm, d // 2), lambda t, te, nu, nx: (t, 0)),
        scratch_shapes=[pltpu.VMEM((d, ff2), F32), pltpu.VMEM((ff, d), F32),
                        pltpu.VMEM((d, ff2), BF16), pltpu.VMEM((ff, d), BF16),
                        pltpu.SemaphoreType.DMA((2,))],
    )
    return pl.pallas_call(
        functools.partial(_expert_body, layer=layer),
        grid_spec=grid_spec,
        out_shape=jax.ShapeDtypeStruct((n_tiles * tm, d // 2), jnp.int32),
        compiler_params=_params(("arbitrary",), EXPERT_VMEM_LIMIT),
        name="expert_mlp",
    )(tile_expert, n_used, next_expert, xs, w_gu, b_gu.reshape(b_gu.shape[0], n_exp, 1, ff2), sel,
      w_down, b_down.reshape(b_down.shape[0], n_exp, 1, d))


def _combine_body(pos_ref, y_hbm, w_ref, x_ref, g_ref, *refs, tc, post):
    if post == "next":
        ng_ref, nsh_ref, nsc_ref, o_ref, h_ref, ybuf, sem = refs
    elif post == "final":
        ng_ref, o_ref, ybuf, sem = refs
    else:
        o_ref, ybuf, sem = refs
    t = pl.program_id(0)
    slot = lax.rem(t, 2)

    def gather(tile, s):
        base = tile * (tc * TOP_K)

        def body(r, carry):
            for k in range(TOP_K):
                row = pos_ref[base + r * TOP_K + k]
                pltpu.async_copy(y_hbm.at[pl.ds(row, 1), :], ybuf.at[s, k, pl.ds(r, 1), :], sem.at[s],
                                 priority=k % 2)
            return carry
        lax.fori_loop(0, tc, body, 0, unroll=2)

    @pl.when(t == 0)
    def _():
        gather(0, 0)

    @pl.when(t + 1 < pl.num_programs(0))
    def _():
        gather(t + 1, 1 - slot)

    for k in range(TOP_K):
        pltpu.make_async_copy(y_hbm.at[pl.ds(0, tc), :], ybuf.at[slot, k], sem.at[slot]).wait()
    half = ybuf.shape[-1]
    ch = COMBINE_CHUNK
    w = w_ref[...]
    wk = [jnp.broadcast_to(w[:, k:k + 1], (tc, LANES)) for k in range(TOP_K)]
    sq = jnp.zeros((tc, 1), F32)
    for c0 in range(0, half, ch):
        acc_lo = acc_hi = None
        for k in range(TOP_K):
            lo, hi = _unpack_halves(ybuf[slot, k, :, c0:c0 + ch])
            wt = jnp.tile(wk[k], (1, ch // LANES))
            acc_lo = wt * lo if k == 0 else acc_lo + wt * lo
            acc_hi = wt * hi if k == 0 else acc_hi + wt * hi
        for acc, c in ((acc_lo, c0), (acc_hi, half + c0)):
            new = x_ref[:, c:c + ch] + g_ref[:, c:c + ch] * acc
            o_ref[:, c:c + ch] = new
            if post is not None:
                sq = sq + jnp.sum(new * new, axis=-1, keepdims=True)
    if post is None:
        return
    inv = lax.rsqrt(sq * (1.0 / (2 * half)) + NORM_EPS)
    for c in range(0, 2 * half, ch):
        norm = o_ref[:, c:c + ch] * inv * ng_ref[:, c:c + ch]
        if post == "final":
            o_ref[:, c:c + ch] = norm
        else:
            h_ref[:, c:c + ch] = (norm * (1.0 + nsc_ref[:, c:c + ch]) + nsh_ref[:, c:c + ch]).astype(h_ref.dtype)


def _combine_call(y_sorted, slot, top_w, xa, modl, gate_idx, rows, seq, n_batch, post=None, post_args=()):
    d = xa.shape[1]
    tc = COMBINE_TILE
    assert rows % tc == 0 and seq % tc == 0
    tpb = seq // tc

    def seg_row(which):
        return lambda t, pos: (jnp.minimum(t // tpb, n_batch) * 6 + which, 0, 0)

    row_spec = pl.BlockSpec((tc, d), lambda t, pos: (t, 0))
    in_specs = [
        pl.BlockSpec(memory_space=pl.ANY),
        pl.BlockSpec((tc, LANES), lambda t, pos: (t, 0)),
        row_spec,
        pl.BlockSpec((None, 1, d), seg_row(gate_idx)),
    ]
    args = [slot, y_sorted, top_w, xa, modl]
    out_specs = row_spec
    out_shape = jax.ShapeDtypeStruct((rows, d), F32)
    if post is not None:
        in_specs.append(pl.BlockSpec((1, d), lambda t, pos: (0, 0)))
        args.append(post_args[0].reshape(1, d))
    if post == "next":
        in_specs += [pl.BlockSpec((None, 1, d), seg_row(0)), pl.BlockSpec((None, 1, d), seg_row(1))]
        args += [post_args[1], post_args[1]]
        out_specs = [row_spec, row_spec]
        out_shape = [out_shape, jax.ShapeDtypeStruct((rows, d), BF16)]
    grid_spec = pltpu.PrefetchScalarGridSpec(
        num_scalar_prefetch=1,
        grid=(rows // tc,),
        in_specs=in_specs,
        out_specs=out_specs,
        scratch_shapes=[pltpu.VMEM((2, TOP_K, tc, d // 2), jnp.int32), pltpu.SemaphoreType.DMA((2,))],
    )
    return pl.pallas_call(
        functools.partial(_combine_body, tc=tc, post=post),
        grid_spec=grid_spec,
        out_shape=out_shape,
        compiler_params=_params(("arbitrary",)),
        name="expert_combine",
    )(*args)


def _axial_angles(seq, head_dim):
    rows = seq // GRID_W
    row = jnp.repeat(jnp.arange(rows, dtype=F32), GRID_W)
    col = jnp.tile(jnp.arange(GRID_W, dtype=F32), rows)
    n_freq = head_dim // 4
    inv_freq = ROPE_BASE ** (-jnp.arange(n_freq, dtype=F32) / n_freq)
    return jnp.concatenate([row[:, None] * inv_freq, col[:, None] * inv_freq], axis=-1)


def _rope_tables(seq, ret_head_dim):
    ang_a = _axial_angles(seq, ATT_HEAD_DIM)
    reps = LANES // ATT_HEAD_DIM
    cos_a = jnp.tile(jnp.concatenate([jnp.cos(ang_a), jnp.cos(ang_a)], axis=-1), (1, reps))
    sin_a = jnp.tile(jnp.concatenate([-jnp.sin(ang_a), jnp.sin(ang_a)], axis=-1), (1, reps))
    ang_r = _axial_angles(seq, ret_head_dim)
    return (cos_a, sin_a), (jnp.cos(ang_r), jnp.sin(ang_r))


def kernel(x, c, ctx, c_ctx, ada_w, ada_b, norm_mix_g, norm_ffn_g, w_in, att_sink, ret_decay_fwd,
           ret_decay_bwd, w_branch_att, w_branch_ret, branch_gate_b, w_out, router_w, router_b,
           exp_w_gu, exp_b_gu, exp_w_down, exp_b_down, final_norm_g):
    n_batch, seq, d = x.shape
    n_ctx = ctx.shape[1]
    depth = ada_w.shape[0]
    n_exp, ff = exp_w_down.shape[1], exp_w_down.shape[2]
    att_width = w_branch_att.shape[1]
    ret_width = w_branch_ret.shape[1]
    kv_width = att_width // ATT_GROUP
    sizes = (("k_att", kv_width), ("v_att", kv_width), ("k_ret", ret_width), ("v_ret", ret_width),
             ("q_att", att_width), ("q_ret", ret_width), ("g_ret", ret_width),
             ("gate_att", d), ("gate_ret", d))
    cols, off = {}, 0
    kv_cols = 2 * kv_width + 2 * ret_width
    for name, width in sizes:
        cols[name] = off if off < kv_cols else off - kv_cols
        off += width
    in_cols = off
    assert in_cols == w_in.shape[2]

    lat_rows = n_batch * seq
    all_rows = lat_rows + n_batch * n_ctx
    xa = (x.reshape(lat_rows, d), ctx.reshape(n_batch * n_ctx, d))

    cond = jnp.zeros((16, d), F32).at[:n_batch].set(c).at[n_batch].set(c_ctx)
    mod = _mod_call(cond, ada_w, ada_b)
    rope_att, rope_ret = _rope_tables(seq, ret_width // RET_HEADS)

    modls = [mod[layer, :n_batch + 1].reshape((n_batch + 1) * 6, 1, d) for layer in range(depth)]
    h = _norm_call(xa, norm_mix_g[0], modls[0], all_rows, seq, n_batch, 0, 1)
    out = None
    for layer in range(depth):
        last = layer == depth - 1
        rows = lat_rows if last else all_rows
        modl = modls[layer]
        log_gamma = jnp.stack([jax.nn.log_sigmoid(ret_decay_fwd[layer].astype(F32)),
                               jax.nn.log_sigmoid(ret_decay_bwd[layer].astype(F32))])

        proj_kv = _inproj_call(h, w_in, layer, all_rows, 0, kv_cols)
        proj_q = _inproj_call(h, w_in, layer, rows, kv_cols, in_cols)

        ya = _attn_call(proj_kv, proj_q, att_sink[layer], rope_att, cols, n_batch, seq, n_ctx, att_width, True)
        yr = _ret_call(proj_kv, proj_q, log_gamma, rope_ret, cols, n_batch, seq, n_ctx, ret_width, True)
        if not last:
            ya_c = _attn_call(proj_kv, proj_q, att_sink[layer], None, cols, n_batch, seq, n_ctx, att_width, False)
            yr_c = _ret_call(proj_kv, proj_q, log_gamma, None, cols, n_batch, seq, n_ctx, ret_width, False)
            ya = (ya, ya_c)
            yr = (yr, yr_c)

        z = _merge_gate_call(ya, yr, proj_q, w_branch_att, w_branch_ret, branch_gate_b, layer, rows, cols)
        xa = _resid_matmul_call(z, w_out, layer, xa, modl, 2, rows, seq, n_batch)

        h2, top_idx, top_w = _norm_call(xa, norm_ffn_g[layer], modl, rows, seq, n_batch, 3, 4,
                                        router=(router_w[layer], router_b[layer]))
        n_tiles = -(-rows * TOP_K // EXPERT_TILE) + n_exp
        tile_expert, n_used, slot, last_tile, next_expert = _routing(
            top_idx[:, :TOP_K], n_exp, EXPERT_TILE, n_tiles)
        xs = _dispatch_call(h2, slot, last_tile, n_used, rows, n_tiles, n_exp)
        y_sorted = _expert_call(xs, exp_w_gu, exp_b_gu, exp_w_down, exp_b_down, layer,
                                tile_expert, n_used, next_expert, n_tiles)
        if last:
            out = _combine_call(y_sorted, slot, top_w, xa, modl, 5, rows, seq, n_batch,
                                post="final", post_args=(final_norm_g,))
        else:
            xa, h = _combine_call(y_sorted, slot, top_w, xa, modl, 5, rows, seq, n_batch,
                                  post="next", post_args=(norm_mix_g[layer + 1], modls[layer + 1]))
    return out.reshape(n_batch, seq, d)
```

```python
import functools
import math

import jax
import jax.numpy as jnp
from jax import lax
from jax.experimental import pallas as pl
from jax.experimental.pallas import tpu as pltpu

F32 = jnp.float32
BF16 = jnp.bfloat16

GRID_W = 64
ATT_HEAD_DIM = 64
ATT_GROUP = 4
ATT_WINDOW = 128
ATT_BLOCK = 128
ATT_PAIRS_PER_STEP = 4
RET_HEADS = 8
RET_CHUNK = 128
RET_UNROLL_BWD = 4
RET_UNROLL_FWD = 8
TOP_K = 4
SWIGLU_ALPHA = 1.702
SWIGLU_LIMIT = 7.0
ROPE_BASE = 10000.0
NORM_EPS = 1e-6
GN_EPS = 1e-5
NEG_INF = -1e30
LOG2_E = 1.4426950408889634

LANES = 128
VMEM_LIMIT = 56 * 1024 * 1024
EXPERT_VMEM_LIMIT = 60 * 1024 * 1024


def _params(sem, vmem=VMEM_LIMIT):
    return pltpu.CompilerParams(dimension_semantics=sem, vmem_limit_bytes=vmem)


def _tile(n, cap, mult):
    for t in range(min(cap, n), 0, -1):
        if n % t == 0 and t % mult == 0:
            return t
    raise ValueError(f"no tile for {n} (cap {cap}, multiple of {mult})")


def _silu(v):
    return v * jax.nn.sigmoid(v)


def _mod_body(c_ref, w_ref, b_ref, o_ref):
    s = _silu(c_ref[...]).astype(BF16)
    o_ref[...] = jnp.dot(s, w_ref[...].astype(BF16), preferred_element_type=F32) + b_ref[...]


def _mod_call(cond, ada_w, ada_b):
    depth, d, n = ada_w.shape
    rows = cond.shape[0]
    tn = _tile(n, 512, LANES)
    return pl.pallas_call(
        _mod_body,
        grid=(depth, n // tn),
        in_specs=[
            pl.BlockSpec((rows, d), lambda l, j: (0, 0)),
            pl.BlockSpec((None, d, tn), lambda l, j: (l, 0, j)),
            pl.BlockSpec((None, 1, tn), lambda l, j: (l, 0, j)),
        ],
        out_specs=pl.BlockSpec((None, rows, tn), lambda l, j: (l, 0, j)),
        out_shape=jax.ShapeDtypeStruct((depth, rows, n), F32),
        compiler_params=_params(("arbitrary", "arbitrary")),
        name="adaln_mod",
    )(cond, ada_w, ada_b.reshape(depth, 1, n))


def _rms(x, g):
    ms = jnp.mean(x * x, axis=-1, keepdims=True)
    return x * lax.rsqrt(ms + NORM_EPS) * g


def _row_sources(src, tr, block_cols, col_of, row_axis):
    if not isinstance(src, tuple):
        return [src], [pl.BlockSpec((tr, block_cols), lambda *g: (g[row_axis], col_of(*g)))], None
    lat, cx = src
    assert lat.shape[0] % tr == 0 and cx.shape[0] % tr == 0
    n_lat = lat.shape[0] // tr
    specs = [pl.BlockSpec((tr, block_cols), lambda *g: (jnp.minimum(g[row_axis], n_lat - 1), col_of(*g))),
             pl.BlockSpec((tr, block_cols), lambda *g: (jnp.maximum(g[row_axis] - n_lat, 0), col_of(*g)))]
    return [lat, cx], specs, n_lat


def _read_rows(refs, n_lat, row_axis):
    if n_lat is None:
        return refs[0][...]
    return jnp.where(pl.program_id(row_axis) < n_lat, refs[0][...], refs[1][...])


def _norm_mod_body(*refs, n_lat):
    n_src = 1 if n_lat is None else 2
    g_ref, sh_ref, sc_ref, h_ref = refs[n_src:]
    h = _rms(_read_rows(refs[:n_src], n_lat, 0), g_ref[...]) * (1.0 + sc_ref[...]) + sh_ref[...]
    h_ref[...] = h.astype(h_ref.dtype)


def _pack_halves(v):
    w = v.shape[1] // 2
    bits = lax.bitcast_convert_type(v.astype(BF16).astype(F32), jnp.int32)
    return lax.shift_right_logical(bits[:, :w], 16) | (bits[:, w:] & jnp.int32(-65536))


def _unpack_halves(p):
    lo = lax.bitcast_convert_type(lax.shift_left(p, 16), F32)
    hi = lax.bitcast_convert_type(p & jnp.int32(-65536), F32)
    return lo, hi


def _split_bf16(v):
    hi = v.astype(BF16)
    lo = (v - hi.astype(F32)).astype(BF16)
    return hi, lo


def _norm_router_body(x_ref, g_ref, sh_ref, sc_ref, rw_ref, rb_ref, h_ref, idx_ref, wt_ref):
    h = _rms(x_ref[...], g_ref[...]) * (1.0 + sc_ref[...]) + sh_ref[...]
    h_ref[...] = _pack_halves(h)
    h_hi, h_lo = _split_bf16(h)
    w_hi, w_lo = _split_bf16(rw_ref[...])
    logits = (jnp.dot(h_hi, w_hi, preferred_element_type=F32)
              + jnp.dot(h_hi, w_lo, preferred_element_type=F32)
              + jnp.dot(h_lo, w_hi, preferred_element_type=F32)) + rb_ref[...]
    n_exp = logits.shape[-1]
    lane = lax.broadcasted_iota(jnp.int32, logits.shape, 1).astype(F32)
    work = logits
    out_lane = lax.broadcasted_iota(jnp.int32, idx_ref.shape, 1)
    idx = jnp.zeros(idx_ref.shape, F32)
    ex = jnp.zeros(wt_ref.shape, F32)
    top = None
    total = None
    for k in range(TOP_K):
        m = jnp.max(work, axis=-1, keepdims=True)
        first = jnp.min(jnp.where(work == m, lane, float(n_exp)), axis=-1, keepdims=True)
        work = jnp.where(lane == first, -jnp.inf, work)
        if k == 0:
            top = m
        e = jnp.exp(m - top)
        total = e if k == 0 else total + e
        idx = jnp.where(out_lane == k, first, idx)
        ex = jnp.where(out_lane == k, e, ex)
    idx_ref[...] = idx.astype(jnp.int32)
    wt_ref[...] = ex / total


def _seg_map(tiles_per_batch, n_batch, which):
    def index(i):
        return (jnp.minimum(i // tiles_per_batch, n_batch) * 6 + which, 0, 0)
    return index


def _norm_call(xa, g, modl, rows, seq, n_batch, shift_idx, scale_idx, router=None):
    d = g.shape[0]
    tr = 256
    assert rows % tr == 0 and seq % tr == 0
    tpb = seq // tr
    row_spec = pl.BlockSpec((tr, d), lambda i: (i, 0))
    srcs, src_specs, n_lat = _row_sources(xa, tr, d, lambda i: 0, 0)
    in_specs = src_specs + [
        pl.BlockSpec((1, d), lambda i: (0, 0)),
        pl.BlockSpec((None, 1, d), _seg_map(tpb, n_batch, shift_idx)),
        pl.BlockSpec((None, 1, d), _seg_map(tpb, n_batch, scale_idx)),
    ]
    args = srcs + [g.reshape(1, d), modl, modl]
    if router is None:
        return pl.pallas_call(
            functools.partial(_norm_mod_body, n_lat=n_lat), grid=(rows // tr,), in_specs=in_specs,
            out_specs=row_spec, out_shape=jax.ShapeDtypeStruct((rows, d), BF16),
            compiler_params=_params(("arbitrary",)), name="norm_mod",
        )(*args)
    assert n_lat is None
    rw, rb = router
    n_exp = rw.shape[1]
    in_specs += [pl.BlockSpec((d, n_exp), lambda i: (0, 0)), pl.BlockSpec((1, n_exp), lambda i: (0, 0))]
    args += [rw, rb.reshape(1, n_exp)]
    return pl.pallas_call(
        _norm_router_body, grid=(rows // tr,), in_specs=in_specs,
        out_specs=[pl.BlockSpec((tr, d // 2), lambda i: (i, 0)),
                   pl.BlockSpec((tr, LANES), lambda i: (i, 0)), pl.BlockSpec((tr, LANES), lambda i: (i, 0))],
        out_shape=[jax.ShapeDtypeStruct((rows, d // 2), jnp.int32), jax.ShapeDtypeStruct((rows, LANES), jnp.int32),
                   jax.ShapeDtypeStruct((rows, LANES), F32)],
        compiler_params=_params(("arbitrary",)), name="norm_router",
    )(*args)


def _inproj_body(a_ref, w_ref, o_ref, wb_ref):
    @pl.when(pl.program_id(1) == 0)
    def _():
        wb_ref[...] = w_ref[...].astype(BF16)
    o_ref[...] = jnp.dot(a_ref[...], wb_ref[...], preferred_element_type=F32).astype(o_ref.dtype)


def _inproj_call(h, w_in, layer, rows, col_lo, col_hi):
    d = h.shape[1]
    tn = 1024
    assert col_lo % tn == 0 and col_hi % tn == 0
    tm = _tile(rows, 576, 16)
    j0 = col_lo // tn
    return pl.pallas_call(
        _inproj_body,
        grid=((col_hi - col_lo) // tn, rows // tm),
        in_specs=[
            pl.BlockSpec((tm, d), lambda j, i: (i, 0)),
            pl.BlockSpec((None, d, tn), lambda j, i: (layer, 0, j0 + j)),
        ],
        out_specs=pl.BlockSpec((tm, tn), lambda j, i: (i, j)),
        out_shape=jax.ShapeDtypeStruct((rows, col_hi - col_lo), BF16),
        scratch_shapes=[pltpu.VMEM((d, tn), BF16)],
        compiler_params=_params(("arbitrary", "arbitrary")),
        name="in_proj",
    )(h, w_in)


def _attn_body(*refs, local, scale):
    if local:
        (sink_ref, q_ref, kp_ref, kc_ref, kn_ref, vp_ref, vc_ref, vn_ref, kx_ref, vx_ref,
         cp_ref, cc_ref, cn_ref, sp_ref, sc_ref, sn_ref, bias_ref, o_ref) = refs
    else:
        sink_ref, q_ref, kx_ref, vx_ref, o_ref = refs
    blk = ATT_BLOCK
    half = ATT_HEAD_DIM // 2
    pair_w = 2 * ATT_HEAD_DIM
    n_pairs = kx_ref.shape[1] // pair_w
    slabs_per_pair = ATT_GROUP

    lane_q = lax.broadcasted_iota(jnp.int32, (blk, LANES), 1)
    first_half = (lane_q % ATT_HEAD_DIM) < half

    def rope(v, c_ref, s_ref):
        rot = jnp.where(first_half, pltpu.roll(v, LANES - half, 1), pltpu.roll(v, half, 1))
        return v * c_ref[...] + rot * s_ref[...]

    def placed(v):
        low = lax.broadcasted_iota(jnp.int32, v.shape, 1) < ATT_HEAD_DIM
        swapped = pltpu.roll(v, ATT_HEAD_DIM, 1)
        return [[jnp.where(low, v, 0.0).astype(BF16), jnp.where(low, 0.0, swapped).astype(BF16)],
                [jnp.where(low, swapped, 0.0).astype(BF16), jnp.where(low, 0.0, v).astype(BF16)]]

    top_rows = lax.broadcasted_iota(jnp.int32, (2 * blk, 1), 0) < blk
    work = [(pp, j, hh) for pp in range(n_pairs) for j in range(2) for hh in range(2)]

    v_vars, scores = [], []
    for pp in range(n_pairs):
        lanes = slice(pp * pair_w, (pp + 1) * pair_w)
        if local:
            keys = jnp.concatenate([
                rope(kp_ref[:, lanes].astype(F32), cp_ref, sp_ref),
                rope(kc_ref[:, lanes].astype(F32), cc_ref, sc_ref),
                rope(kn_ref[:, lanes].astype(F32), cn_ref, sn_ref),
                kx_ref[:, lanes].astype(F32)], axis=0)
            vals = jnp.concatenate([vp_ref[:, lanes], vc_ref[:, lanes], vn_ref[:, lanes], vx_ref[:, lanes]],
                                   axis=0).astype(F32)
        else:
            keys = kx_ref[:, lanes].astype(F32)
            vals = vx_ref[:, lanes].astype(F32)
        k_var = placed(keys)
        v_vars.append(placed(vals))
        q_slabs = []
        for c in range(slabs_per_pair):
            c0 = (pp * slabs_per_pair + c) * LANES
            qc = q_ref[:, c0:c0 + LANES].astype(F32)
            if local:
                qc = rope(qc, cc_ref, sc_ref)
            q_slabs.append((qc * (scale * LOG2_E)).astype(BF16))
        for j in range(2):
            lhs = jnp.concatenate([q_slabs[2 * j], q_slabs[2 * j + 1]], axis=0)
            for hh in range(2):
                s = lax.dot_general(lhs, k_var[j][hh], (((1,), (1,)), ((), ())),
                                    preferred_element_type=F32)
                if local:
                    s = s + bias_ref[...]
                scores.append(s)

    probs = []
    for (pp, j, hh), s in zip(work, scores):
        head0 = (pl.program_id(1) * n_pairs + pp) * (2 * ATT_GROUP) + ATT_GROUP * j + hh
        sink = jnp.where(top_rows, sink_ref[head0], sink_ref[head0 + 2]) * LOG2_E
        m = jnp.maximum(jnp.max(s, axis=-1, keepdims=True), sink)
        e = jnp.exp2(s - m)
        denom = jnp.sum(e, axis=-1, keepdims=True) + jnp.exp2(sink - m)
        probs.append((e.astype(BF16), 1.0 / denom))

    out = [jnp.zeros((blk, LANES), F32) for _ in range(n_pairs * slabs_per_pair)]
    for (pp, j, hh), (e, inv) in zip(work, probs):
        o = jnp.dot(e, v_vars[pp][j][hh], preferred_element_type=F32) * inv
        c = pp * slabs_per_pair + 2 * j
        out[c] = out[c] + o[:blk]
        out[c + 1] = out[c + 1] + o[blk:]
    for c, val in enumerate(out):
        o_ref[:, c * LANES:(c + 1) * LANES] = val.astype(o_ref.dtype)


def _attn_call(proj_kv, proj_q, sink, rope_tabs, cols, n_batch, seq, n_ctx, att_width, local):
    blk = ATT_BLOCK
    nb = seq // blk
    pair_w = 2 * ATT_HEAD_DIM * ATT_PAIRS_PER_STEP
    q_w = pair_w * ATT_GROUP
    assert att_width % q_w == 0
    n_pair = att_width // q_w
    k0 = cols["k_att"] // pair_w
    v0 = cols["v_att"] // pair_w
    q0 = cols["q_att"] // q_w
    ctx_row0 = (n_batch * seq) // n_ctx
    kx_spec = pl.BlockSpec((n_ctx, pair_w), lambda b, p, n: (ctx_row0 + b, k0 + p))
    vx_spec = pl.BlockSpec((n_ctx, pair_w), lambda b, p, n: (ctx_row0 + b, v0 + p))
    sink_spec = pl.BlockSpec(memory_space=pltpu.SMEM)
    scale = ATT_HEAD_DIM ** -0.5
    if local:
        cos_t, sin_t = rope_tabs

        def prev(n):
            return jnp.maximum(n - 1, 0)

        def nxt(n):
            return jnp.minimum(n + 1, nb - 1)

        def kv(c0, f):
            return pl.BlockSpec((blk, pair_w), lambda b, p, n: (b * nb + f(n), c0 + p))

        def tab(f):
            return pl.BlockSpec((blk, LANES), lambda b, p, n: (f(n), 0))

        ident = lambda n: n
        assert nb >= 2
        r = jnp.arange(2 * blk, dtype=jnp.int32)[:, None] % blk
        c = jnp.arange(3 * blk + n_ctx, dtype=jnp.int32)[None, :]
        band = (jnp.abs(c - blk - r) <= ATT_WINDOW) | (c >= 3 * blk)
        edge = jnp.stack([band & (c >= blk), band, band & ((c < 2 * blk) | (c >= 3 * blk))])
        bias = jnp.where(edge, 0.0, NEG_INF).astype(F32)
        bias_spec = pl.BlockSpec((None, 2 * blk, 3 * blk + n_ctx),
                                 lambda b, p, n: ((n > 0).astype(jnp.int32) + (n == nb - 1).astype(jnp.int32), 0, 0))
        in_specs = [sink_spec,
                    pl.BlockSpec((blk, q_w), lambda b, p, n: (b * nb + n, q0 + p)),
                    kv(k0, prev), kv(k0, ident), kv(k0, nxt),
                    kv(v0, prev), kv(v0, ident), kv(v0, nxt),
                    kx_spec, vx_spec,
                    tab(prev), tab(ident), tab(nxt), tab(prev), tab(ident), tab(nxt), bias_spec]
        args = [sink, proj_q] + [proj_kv] * 6 + [proj_kv, proj_kv] + [cos_t] * 3 + [sin_t] * 3 + [bias]
        grid = (n_batch, n_pair, nb)
        rows = n_batch * seq
        out_spec = pl.BlockSpec((blk, q_w), lambda b, p, n: (b * nb + n, p))
    else:
        nqb = n_ctx // blk
        qrow0 = (n_batch * seq) // blk
        in_specs = [sink_spec,
                    pl.BlockSpec((blk, q_w), lambda b, p, n: (qrow0 + b * nqb + n, q0 + p)),
                    kx_spec, vx_spec]
        args = [sink, proj_q, proj_kv, proj_kv]
        grid = (n_batch, n_pair, nqb)
        rows = n_batch * n_ctx
        out_spec = pl.BlockSpec((blk, q_w), lambda b, p, n: (b * nqb + n, p))
    return pl.pallas_call(
        functools.partial(_attn_body, local=local, scale=scale),
        grid=grid, in_specs=in_specs, out_specs=out_spec,
        out_shape=jax.ShapeDtypeStruct((rows, att_width), BF16),
        compiler_params=_params(("arbitrary",) * 3),
        name="window_attn" if local else "ctx_attn",
    )(*args)


def _dot_tn(a, b):
    return lax.dot_general(a, b, (((0,), (0,)), ((), ())), preferred_element_type=F32)


def _ret_body(*refs, latent, n_chunks, n_ctx, k_scale):
    if latent:
        (lg_ref, q_ref, k_ref, v_ref, g_ref, kx_ref, vx_ref, cos_ref, sin_ref,
         o_ref, qs_ref, ks_ref, cb_ref, st_ref, sf_ref) = refs
    else:
        lg_ref, q_ref, k_ref, v_ref, g_ref, o_ref, qs_ref, ks_ref, cb_ref, st_ref, sf_ref = refs
    head = pl.program_id(1)
    lg_f = lg_ref[0, head]
    lg_b = lg_ref[1, head]
    c = RET_CHUNK
    hd = q_ref.shape[1]
    hh = hd // 2
    idx = lax.broadcasted_iota(jnp.int32, (c, 1), 0).astype(F32)
    qdec_f = jnp.exp(lg_f * (idx + 1.0))
    kdec_f = jnp.exp(lg_f * (c - 1.0 - idx))
    qdec_b = jnp.exp(lg_b * (c - idx))
    kdec_b = jnp.exp(lg_b * idx)
    one = jnp.ones((1, 1), F32)
    cdec_f = jnp.exp(lg_f * c * one)
    cdec_b = jnp.exp(lg_b * c * one)
    rel = (lax.broadcasted_iota(jnp.int32, (c, c), 0) - lax.broadcasted_iota(jnp.int32, (c, c), 1)).astype(F32)
    dmat = (jnp.where(rel >= 0, jnp.exp(lg_f * jnp.maximum(rel, 0.0)), 0.0)
            + jnp.where(rel <= 0, jnp.exp(lg_b * jnp.maximum(-rel, 0.0)), 0.0))

    def rope(v, rows):
        v1 = v[:, :hh]
        v2 = v[:, hh:]
        cs = cos_ref[rows, :]
        sn = sin_ref[rows, :]
        return jnp.concatenate([v1 * cs - v2 * sn, v2 * cs + v1 * sn], axis=1)

    if latent:
        t = lax.broadcasted_iota(jnp.int32, (n_ctx, 1), 0).astype(F32)
        kx = kx_ref[...].astype(F32) * k_scale
        sf_ref[...] = _dot_tn((kx * jnp.exp(lg_f * (n_ctx - 1.0 - t))).astype(BF16), vx_ref[...])
        st_ref[...] = _dot_tn((kx * jnp.exp(lg_b * t)).astype(BF16), vx_ref[...])
    else:
        sf_ref[...] = jnp.zeros((hd, hd), F32)
        st_ref[...] = jnp.zeros((hd, hd), F32)


    def bwd(step, carry):
        rows = pl.ds(pl.multiple_of((n_chunks - 1 - step) * c, c), c)
        q = q_ref[rows, :].astype(F32)
        k = k_ref[rows, :].astype(F32)
        if latent:
            q = rope(q, rows)
            k = rope(k, rows)
        k = k * k_scale
        qs_ref[rows, :] = q.astype(BF16)
        ks_ref[rows, :] = k.astype(BF16)
        state = st_ref[...]
        cb_ref[rows, :] = jnp.dot((q * qdec_b).astype(BF16), state.astype(BF16), preferred_element_type=F32)
        st_ref[...] = cdec_b * state + _dot_tn((k * kdec_b).astype(BF16), v_ref[rows, :])
        return carry

    lax.fori_loop(0, n_chunks, bwd, 0, unroll=min(RET_UNROLL_BWD, n_chunks))

    st_ref[...] = sf_ref[...]

    def fwd(step, carry):
        rows = pl.ds(pl.multiple_of(step * c, c), c)
        qb = qs_ref[rows, :]
        kb = ks_ref[rows, :]
        v = v_ref[rows, :]
        scores = lax.dot_general(qb, kb, (((1,), (1,)), ((), ())), preferred_element_type=F32) * dmat
        o = jnp.dot(scores.astype(BF16), v, preferred_element_type=F32)
        state = st_ref[...]
        o = o + jnp.dot((qb.astype(F32) * qdec_f).astype(BF16), state.astype(BF16), preferred_element_type=F32)
        o = o + cb_ref[rows, :]
        st_ref[...] = cdec_f * state + _dot_tn((kb.astype(F32) * kdec_f).astype(BF16), v)
        mu = jnp.mean(o, axis=-1, keepdims=True)
        dev = o - mu
        var = jnp.mean(dev * dev, axis=-1, keepdims=True)
        y = dev * lax.rsqrt(var + GN_EPS)
        o_ref[rows, :] = (_silu(g_ref[rows, :].astype(F32)) * y).astype(o_ref.dtype)
        return carry

    lax.fori_loop(0, n_chunks, fwd, 0, unroll=min(RET_UNROLL_FWD, n_chunks))


def _ret_call(proj_kv, proj_q, log_gamma, rope_tabs, cols, n_batch, seq, n_ctx, ret_width, latent):
    hd = ret_width // RET_HEADS
    t_len = seq if latent else n_ctx
    assert t_len % RET_CHUNK == 0
    k0, v0, q0, g0 = (cols[name] // hd for name in ("k_ret", "v_ret", "q_ret", "g_ret"))
    ctx_row0 = (n_batch * seq) // n_ctx
    row0 = 0 if latent else ctx_row0

    def seq_spec(c0):
        return pl.BlockSpec((t_len, hd), lambda b, h: (row0 + b, c0 + h))

    in_specs = [pl.BlockSpec(memory_space=pltpu.SMEM),
                seq_spec(q0), seq_spec(k0), seq_spec(v0), seq_spec(g0)]
    args = [log_gamma, proj_q, proj_kv, proj_kv, proj_q]
    if latent:
        cos_t, sin_t = rope_tabs
        in_specs += [pl.BlockSpec((n_ctx, hd), lambda b, h: (ctx_row0 + b, k0 + h)),
                     pl.BlockSpec((n_ctx, hd), lambda b, h: (ctx_row0 + b, v0 + h)),
                     pl.BlockSpec((t_len, hd // 2), lambda b, h: (0, 0)),
                     pl.BlockSpec((t_len, hd // 2), lambda b, h: (0, 0))]
        args += [proj_kv, proj_kv, cos_t, sin_t]
    return pl.pallas_call(
        functools.partial(_ret_body, latent=latent, n_chunks=t_len // RET_CHUNK, n_ctx=n_ctx,
                          k_scale=hd ** -0.5),
        grid=(n_batch, RET_HEADS), in_specs=in_specs,
        out_specs=pl.BlockSpec((t_len, hd), lambda b, h: (b, h)),
        out_shape=jax.ShapeDtypeStruct((n_batch * t_len, ret_width), BF16),
        scratch_shapes=[pltpu.VMEM((t_len, hd), BF16), pltpu.VMEM((t_len, hd), BF16),
                        pltpu.VMEM((t_len, hd), F32), pltpu.VMEM((hd, hd), F32), pltpu.VMEM((hd, hd), F32)],
        compiler_params=_params(("arbitrary", "arbitrary")),
        name="retention" if latent else "ctx_retention",
    )(*args)


def _merge_gate_body(*refs, n_lat):
    n_src = 1 if n_lat is None else 2
    ga_ref, gr_ref, wa_ref, wr_ref, ba_ref, br_ref, z_ref, wab_ref, wrb_ref = refs[2 * n_src:]

    @pl.when(pl.program_id(1) == 0)
    def _():
        wab_ref[...] = wa_ref[...].astype(BF16)
        wrb_ref[...] = wr_ref[...].astype(BF16)
    a = jnp.dot(_read_rows(refs[:n_src], n_lat, 1), wab_ref[...], preferred_element_type=F32)
    r = jnp.dot(_read_rows(refs[n_src:2 * n_src], n_lat, 1), wrb_ref[...], preferred_element_type=F32)
    ga = jax.nn.sigmoid(ga_ref[...].astype(F32) + ba_ref[...])
    gr = jax.nn.sigmoid(gr_ref[...].astype(F32) + br_ref[...])
    z_ref[...] = (ga * a + gr * r).astype(z_ref.dtype)


def _merge_gate_call(ya, yr, proj, w_att, w_ret, gate_b, layer, rows, cols):
    d = w_att.shape[2]
    wa_rows, wr_rows = w_att.shape[1], w_ret.shape[1]
    tn = 512
    if isinstance(ya, tuple):
        tm = _tile(math.gcd(ya[0].shape[0], ya[1].shape[0]), 1024, 16)
    else:
        tm = _tile(rows, 1024, 16)
    ga0 = cols["gate_att"] // tn
    gr0 = cols["gate_ret"] // tn
    gb = gate_b.reshape(gate_b.shape[0], 1, 2 * d)
    ya_srcs, ya_specs, n_lat = _row_sources(ya, tm, wa_rows, lambda j, i: 0, 1)
    yr_srcs, yr_specs, _ = _row_sources(yr, tm, wr_rows, lambda j, i: 0, 1)
    return pl.pallas_call(
        functools.partial(_merge_gate_body, n_lat=n_lat),
        grid=(d // tn, rows // tm),
        in_specs=ya_specs + yr_specs + [
            pl.BlockSpec((tm, tn), lambda j, i: (i, ga0 + j)),
            pl.BlockSpec((tm, tn), lambda j, i: (i, gr0 + j)),
            pl.BlockSpec((None, wa_rows, tn), lambda j, i: (layer, 0, j)),
            pl.BlockSpec((None, wr_rows, tn), lambda j, i: (layer, 0, j)),
            pl.BlockSpec((None, 1, tn), lambda j, i: (layer, 0, j)),
            pl.BlockSpec((None, 1, tn), lambda j, i: (layer, 0, d // tn + j)),
        ],
        out_specs=pl.BlockSpec((tm, tn), lambda j, i: (i, j)),
        out_shape=jax.ShapeDtypeStruct((rows, d), BF16),
        scratch_shapes=[pltpu.VMEM((wa_rows, tn), BF16), pltpu.VMEM((wr_rows, tn), BF16)],
        compiler_params=_params(("arbitrary", "arbitrary")),
        name="merge_gate",
    )(*ya_srcs, *yr_srcs, proj, proj, w_att, w_ret, gb, gb)


def _resid_matmul_body(z_ref, w_ref, *refs, n_lat):
    n_src = 1 if n_lat is None else 2
    g_ref, o_ref, wb_ref = refs[n_src:]

    @pl.when(pl.program_id(1) == 0)
    def _():
        wb_ref[...] = w_ref[...].astype(BF16)
    y = jnp.dot(z_ref[...], wb_ref[...], preferred_element_type=F32)
    o_ref[...] = _read_rows(refs[:n_src], n_lat, 1) + g_ref[...] * y


def _resid_matmul_call(z, w, layer, xa, modl, gate_idx, rows, seq, n_batch):
    k, d = w.shape[1], w.shape[2]
    tn = 512
    tm = 512
    assert rows % tm == 0 and seq % tm == 0
    tpb = seq // tm

    def gate_map(j, i):
        return (jnp.minimum(i // tpb, n_batch) * 6 + gate_idx, 0, j)

    srcs, src_specs, n_lat = _row_sources(xa, tm, tn, lambda j, i: j, 1)
    return pl.pallas_call(
        functools.partial(_resid_matmul_body, n_lat=n_lat),
        grid=(d // tn, rows // tm),
        in_specs=[
            pl.BlockSpec((tm, k), lambda j, i: (i, 0)),
            pl.BlockSpec((None, k, tn), lambda j, i: (layer, 0, j)),
        ] + src_specs + [pl.BlockSpec((None, 1, tn), gate_map)],
        out_specs=pl.BlockSpec((tm, tn), lambda j, i: (i, j)),
        out_shape=jax.ShapeDtypeStruct((rows, d), F32),
        scratch_shapes=[pltpu.VMEM((k, tn), BF16)],
        compiler_params=_params(("arbitrary", "arbitrary")),
        name="resid_matmul",
    )(z, w, *srcs, modl)


EXPERT_TILE = 256
COMBINE_TILE = 128
DISPATCH_TILE = 256
COMBINE_CHUNK = 256


def _routing(top_idx, n_exp, tm, n_tiles):
    n_assign = top_idx.size
    e_flat = top_idx.reshape(n_assign)
    onehot = (e_flat[:, None] == jnp.arange(n_exp, dtype=jnp.int32)[None, :]).astype(jnp.int32)
    running = jnp.cumsum(onehot, axis=0)
    counts = running[-1]
    group = (counts + tm - 1) // tm * tm
    group_end = jnp.cumsum(group)
    slot = jnp.sum(onehot * (running - 1 + (group_end - group)[None, :]), axis=1).astype(jnp.int32)
    n_used = (group_end[-1] // tm).astype(jnp.int32).reshape(1)
    tile_start = jnp.arange(n_tiles, dtype=jnp.int32) * tm
    tile_expert = jnp.sum((group_end[None, :] <= tile_start[:, None]).astype(jnp.int32), axis=1)
    tile_expert = jnp.minimum(tile_expert, n_exp - 1).astype(jnp.int32)
    last_tile = jnp.stack([jnp.maximum(group_end - tm, 0), (group > 0).astype(jnp.int32)], axis=1)
    ids = jnp.arange(n_exp, dtype=jnp.int32)
    later = (ids[None, :] > ids[:, None]) & (group[None, :] > 0)
    next_of = jnp.min(jnp.where(later, ids[None, :], n_exp), axis=1)
    next_of = jnp.where(next_of == n_exp, -1, next_of)
    next_expert = jnp.sum((tile_expert[:, None] == ids[None, :]) * next_of[None, :], axis=1).astype(jnp.int32)
    return tile_expert, n_used, slot, last_tile.reshape(2 * n_exp).astype(jnp.int32), next_expert


def _expert_changed(te_ref, t):
    return jnp.logical_or(t == 0, te_ref[t] != te_ref[jnp.maximum(t - 1, 0)])


def _dispatch_body(slot_ref, last_ref, nu_ref, h_ref, xs_hbm, zero_ref, sem, zsem, *, tr, tm, n_exp, n_tiles):
    t = pl.program_id(0)

    @pl.when(t == 0)
    def _():
        zero_ref[...] = jnp.zeros(zero_ref.shape, zero_ref.dtype)

        def zero_copy(start):
            return pltpu.make_async_copy(zero_ref, xs_hbm.at[pl.ds(pl.multiple_of(start, tm), tm), :], zsem)

        def start_one(e, carry):
            @pl.when(last_ref[2 * e + 1] == 1)
            def _():
                zero_copy(last_ref[2 * e]).start()
            return carry

        def wait_one(e, carry):
            @pl.when(last_ref[2 * e + 1] == 1)
            def _():
                zero_copy(last_ref[2 * e]).wait()
            return carry

        def start_tail(i, carry):
            zero_copy(i * tm).start()
            return carry

        def wait_tail(i, carry):
            zero_copy(i * tm).wait()
            return carry

        lax.fori_loop(0, n_exp, start_one, 0)
        lax.fori_loop(nu_ref[0], n_tiles, start_tail, 0)
        lax.fori_loop(0, n_exp, wait_one, 0)
        lax.fori_loop(nu_ref[0], n_tiles, wait_tail, 0)

    base = t * (tr * TOP_K)

    def body(r, carry):
        for k in range(TOP_K):
            row = slot_ref[base + r * TOP_K + k]
            pltpu.async_copy(h_ref.at[pl.ds(r, 1), :], xs_hbm.at[pl.ds(row, 1), :], sem, priority=k % 2)
        return carry
    lax.fori_loop(0, tr, body, 0, unroll=2)

    for _ in range(TOP_K):
        pltpu.make_async_copy(h_ref, xs_hbm.at[pl.ds(0, tr), :], sem).wait()


def _dispatch_call(h2p, slot, last_tile, n_used, rows, n_tiles, n_exp):
    w = h2p.shape[1]
    tr = DISPATCH_TILE
    tm = EXPERT_TILE
    assert rows % tr == 0
    grid_spec = pltpu.PrefetchScalarGridSpec(
        num_scalar_prefetch=3,
        grid=(rows // tr,),
        in_specs=[pl.BlockSpec((tr, w), lambda t, sl, lt, nu: (t, 0))],
        out_specs=pl.BlockSpec(memory_space=pl.ANY),
        scratch_shapes=[pltpu.VMEM((tm, w), jnp.int32), pltpu.SemaphoreType.DMA(()), pltpu.SemaphoreType.DMA(())],
    )
    return pl.pallas_call(
        functools.partial(_dispatch_body, tr=tr, tm=tm, n_exp=n_exp, n_tiles=n_tiles),
        grid_spec=grid_spec,
        out_shape=jax.ShapeDtypeStruct((n_tiles * tm, w), jnp.int32),
        compiler_params=_params(("arbitrary",)),
        name="expert_dispatch",
    )(slot, last_tile, n_used, h2p)


def _expert_body(te_ref, nu_ref, nx_ref, x_ref, wgu_hbm, bgu_ref, sel_ref, wd_hbm, bd_ref, y_ref,
                 sgu_ref, sd_ref, wgu_ref, wd_ref, sem, *, layer):
    t = pl.program_id(0)
    n_used = nu_ref[0]

    def weight_copies(e):
        return (pltpu.make_async_copy(wgu_hbm.at[layer, e], sgu_ref, sem.at[0]),
                pltpu.make_async_copy(wd_hbm.at[layer, e], sd_ref, sem.at[1]))

    def fetch(e):
        for copy in weight_copies(e):
            copy.start()

    @pl.when(t == 0)
    def _():
        fetch(te_ref[0])

    @pl.when(t < n_used)
    def _():
        @pl.when(_expert_changed(te_ref, t))
        def _():
            for copy in weight_copies(te_ref[t]):
                copy.wait()
            wgu_ref[...] = sgu_ref[...].astype(BF16)
            wd_ref[...] = sd_ref[...].astype(BF16)

            @pl.when(nx_ref[t] >= 0)
            def _():
                fetch(nx_ref[t])

        x_lo, x_hi = _unpack_halves(x_ref[...])
        half = x_lo.shape[1]
        gu = (jnp.dot(x_lo.astype(BF16), wgu_ref[:half, :], preferred_element_type=F32)
              + jnp.dot(x_hi.astype(BF16), wgu_ref[half:, :], preferred_element_type=F32)) + bgu_ref[...]
        n2 = gu.shape[1]
        even = lax.broadcasted_iota(jnp.int32, gu.shape, 1) % 2 == 0
        gate = jnp.minimum(gu, SWIGLU_LIMIT)
        up = jnp.clip(pltpu.roll(gu, n2 - 1, 1), -SWIGLU_LIMIT, SWIGLU_LIMIT)
        act = jnp.where(even, (up + 1.0) * (gate * jax.nn.sigmoid(SWIGLU_ALPHA * gate)), 0.0)
        a = jnp.dot(act.astype(BF16), sel_ref[...], preferred_element_type=F32).astype(BF16)
        y = jnp.dot(a, wd_ref[...], preferred_element_type=F32) + bd_ref[...]
        y_ref[...] = _pack_halves(y)

    @pl.when(t >= n_used)
    def _():
        y_ref[...] = jnp.zeros(y_ref.shape, y_ref.dtype)


def _used_tile(t, nu):
    return jnp.minimum(t, nu[0] - 1)


def _expert_call(xs, w_gu, b_gu, w_down, b_down, layer, tile_expert, n_used, next_expert, n_tiles):
    n_exp, d, ff2 = w_gu.shape[1:]
    ff = ff2 // 2
    tm = EXPERT_TILE
    sel = (jnp.arange(ff2, dtype=jnp.int32)[:, None] == 2 * jnp.arange(ff, dtype=jnp.int32)[None, :]).astype(BF16)
    grid_spec = pltpu.PrefetchScalarGridSpec(
        num_scalar_prefetch=3,
        grid=(n_tiles,),
        in_specs=[
            pl.BlockSpec((tm, d // 2), lambda t, te, nu, nx: (_used_tile(t, nu), 0)),
            pl.BlockSpec(memory_space=pl.ANY),
            pl.BlockSpec((None, None, 1, ff2), lambda t, te, nu, nx: (layer, te[t], 0, 0)),
            pl.BlockSpec((ff2, ff), lambda t, te, nu, nx: (0, 0)),
            pl.BlockSpec(memory_space=pl.ANY),
            pl.BlockSpec((None, None, 1, d), lambda t, te, nu, nx: (layer, te[t], 0, 0)),
        ],
        out_specs=pl.BlockSpec((tm, d // 2), lambda t, te, nu, nx: (t, 0)),
        scratch_shapes=[pltpu.VMEM((d, ff2), F32), pltpu.VMEM((ff, d), F32),
                        pltpu.VMEM((d, ff2), BF16), pltpu.VMEM((ff, d), BF16),
                        pltpu.SemaphoreType.DMA((2,))],
    )
    return pl.pallas_call(
        functools.partial(_expert_body, layer=layer),
        grid_spec=grid_spec,
        out_shape=jax.ShapeDtypeStruct((n_tiles * tm, d // 2), jnp.int32),
        compiler_params=_params(("arbitrary",), EXPERT_VMEM_LIMIT),
        name="expert_mlp",
    )(tile_expert, n_used, next_expert, xs, w_gu, b_gu.reshape(b_gu.shape[0], n_exp, 1, ff2), sel,
      w_down, b_down.reshape(b_down.shape[0], n_exp, 1, d))


def _combine_body(pos_ref, y_hbm, w_ref, x_ref, g_ref, *refs, tc, post):
    if post == "next":
        ng_ref, nsh_ref, nsc_ref, o_ref, h_ref, ybuf, sem = refs
    elif post == "final":
        ng_ref, o_ref, ybuf, sem = refs
    else:
        o_ref, ybuf, sem = refs
    t = pl.program_id(0)
    last = pl.num_programs(0) - 1
    slot = lax.rem(t, 2)
    other = 1 - slot

    def row_copies(base, s, r):
        for k in range(TOP_K):
            row = pos_ref[base + r * TOP_K + k]
            pltpu.async_copy(y_hbm.at[pl.ds(row, 1), :], ybuf.at[s, k, pl.ds(r, 1), :], sem.at[s], priority=k % 2)

    def drain(s):
        for k in range(TOP_K):
            pltpu.make_async_copy(y_hbm.at[pl.ds(0, tc), :], ybuf.at[s, k], sem.at[s]).wait()

    @pl.when(t == 0)
    def _():
        def body(r, carry):
            row_copies(0, 0, r)
            return carry
        lax.fori_loop(0, tc, body, 0, unroll=2)

    drain(slot)

    next_base = jnp.minimum(t + 1, last) * (tc * TOP_K)
    half = ybuf.shape[-1]
    ch = COMBINE_CHUNK
    n_first = half // ch
    n_second = 0 if post is None else 2 * half // ch
    rows_first = (tc if post is None else tc // 2) // n_first
    rows_second = 0 if post is None else (tc - rows_first * n_first) // n_second
    assert rows_first * n_first + rows_second * n_second == tc
    issued = [0]

    def issue_rows(count):
        for r in range(issued[0], issued[0] + count):
            row_copies(next_base, other, r)
        issued[0] += count

    w = w_ref[...]
    wk = [jnp.broadcast_to(w[:, k:k + 1], (tc, LANES)) for k in range(TOP_K)]
    sq = jnp.zeros((tc, 1), F32)
    for c0 in range(0, half, ch):
        issue_rows(rows_first)
        acc_lo = acc_hi = None
        for k in range(TOP_K):
            lo, hi = _unpack_halves(ybuf[slot, k, :, c0:c0 + ch])
            wt = jnp.tile(wk[k], (1, ch // LANES))
            acc_lo = wt * lo if k == 0 else acc_lo + wt * lo
            acc_hi = wt * hi if k == 0 else acc_hi + wt * hi
        for acc, c in ((acc_lo, c0), (acc_hi, half + c0)):
            new = x_ref[:, c:c + ch] + g_ref[:, c:c + ch] * acc
            o_ref[:, c:c + ch] = new
            if post is not None:
                sq = sq + jnp.sum(new * new, axis=-1, keepdims=True)
    if post is not None:
        inv = lax.rsqrt(sq * (1.0 / (2 * half)) + NORM_EPS)
        for c in range(0, 2 * half, ch):
            issue_rows(rows_second)
            norm = o_ref[:, c:c + ch] * inv * ng_ref[:, c:c + ch]
            if post == "final":
                o_ref[:, c:c + ch] = norm
            else:
                h_ref[:, c:c + ch] = (norm * (1.0 + nsc_ref[:, c:c + ch])
                                      + nsh_ref[:, c:c + ch]).astype(h_ref.dtype)

    @pl.when(t == last)
    def _():
        drain(other)


def _combine_call(y_sorted, slot, top_w, xa, modl, gate_idx, rows, seq, n_batch, post=None, post_args=()):
    d = xa.shape[1]
    tc = COMBINE_TILE
    assert rows % tc == 0 and seq % tc == 0
    tpb = seq // tc

    def seg_row(which):
        return lambda t, pos: (jnp.minimum(t // tpb, n_batch) * 6 + which, 0, 0)

    row_spec = pl.BlockSpec((tc, d), lambda t, pos: (t, 0))
    in_specs = [
        pl.BlockSpec(memory_space=pl.ANY),
        pl.BlockSpec((tc, LANES), lambda t, pos: (t, 0)),
        row_spec,
        pl.BlockSpec((None, 1, d), seg_row(gate_idx)),
    ]
    args = [slot, y_sorted, top_w, xa, modl]
    out_specs = row_spec
    out_shape = jax.ShapeDtypeStruct((rows, d), F32)
    if post is not None:
        in_specs.append(pl.BlockSpec((1, d), lambda t, pos: (0, 0)))
        args.append(post_args[0].reshape(1, d))
    if post == "next":
        in_specs += [pl.BlockSpec((None, 1, d), seg_row(0)), pl.BlockSpec((None, 1, d), seg_row(1))]
        args += [post_args[1], post_args[1]]
        out_specs = [row_spec, row_spec]
        out_shape = [out_shape, jax.ShapeDtypeStruct((rows, d), BF16)]
    grid_spec = pltpu.PrefetchScalarGridSpec(
        num_scalar_prefetch=1,
        grid=(rows // tc,),
        in_specs=in_specs,
        out_specs=out_specs,
        scratch_shapes=[pltpu.VMEM((2, TOP_K, tc, d // 2), jnp.int32), pltpu.SemaphoreType.DMA((2,))],
    )
    return pl.pallas_call(
        functools.partial(_combine_body, tc=tc, post=post),
        grid_spec=grid_spec,
        out_shape=out_shape,
        compiler_params=_params(("arbitrary",)),
        name="expert_combine",
    )(*args)


def _axial_angles(seq, head_dim):
    rows = seq // GRID_W
    row = jnp.repeat(jnp.arange(rows, dtype=F32), GRID_W)
    col = jnp.tile(jnp.arange(GRID_W, dtype=F32), rows)
    n_freq = head_dim // 4
    inv_freq = ROPE_BASE ** (-jnp.arange(n_freq, dtype=F32) / n_freq)
    return jnp.concatenate([row[:, None] * inv_freq, col[:, None] * inv_freq], axis=-1)


def _rope_tables(seq, ret_head_dim):
    ang_a = _axial_angles(seq, ATT_HEAD_DIM)
    reps = LANES // ATT_HEAD_DIM
    cos_a = jnp.tile(jnp.concatenate([jnp.cos(ang_a), jnp.cos(ang_a)], axis=-1), (1, reps))
    sin_a = jnp.tile(jnp.concatenate([-jnp.sin(ang_a), jnp.sin(ang_a)], axis=-1), (1, reps))
    ang_r = _axial_angles(seq, ret_head_dim)
    return (cos_a, sin_a), (jnp.cos(ang_r), jnp.sin(ang_r))


def kernel(x, c, ctx, c_ctx, ada_w, ada_b, norm_mix_g, norm_ffn_g, w_in, att_sink, ret_decay_fwd,
           ret_decay_bwd, w_branch_att, w_branch_ret, branch_gate_b, w_out, router_w, router_b,
           exp_w_gu, exp_b_gu, exp_w_down, exp_b_down, final_norm_g):
    n_batch, seq, d = x.shape
    n_ctx = ctx.shape[1]
    depth = ada_w.shape[0]
    n_exp, ff = exp_w_down.shape[1], exp_w_down.shape[2]
    att_width = w_branch_att.shape[1]
    ret_width = w_branch_ret.shape[1]
    kv_width = att_width // ATT_GROUP
    sizes = (("k_att", kv_width), ("v_att", kv_width), ("k_ret", ret_width), ("v_ret", ret_width),
             ("q_att", att_width), ("q_ret", ret_width), ("g_ret", ret_width),
             ("gate_att", d), ("gate_ret", d))
    cols, off = {}, 0
    kv_cols = 2 * kv_width + 2 * ret_width
    for name, width in sizes:
        cols[name] = off if off < kv_cols else off - kv_cols
        off += width
    in_cols = off
    assert in_cols == w_in.shape[2]

    lat_rows = n_batch * seq
    all_rows = lat_rows + n_batch * n_ctx
    xa = (x.reshape(lat_rows, d), ctx.reshape(n_batch * n_ctx, d))

    cond = jnp.zeros((16, d), F32).at[:n_batch].set(c).at[n_batch].set(c_ctx)
    mod = _mod_call(cond, ada_w, ada_b)
    rope_att, rope_ret = _rope_tables(seq, ret_width // RET_HEADS)

    modls = [mod[layer, :n_batch + 1].reshape((n_batch + 1) * 6, 1, d) for layer in range(depth)]
    h = _norm_call(xa, norm_mix_g[0], modls[0], all_rows, seq, n_batch, 0, 1)
    out = None
    for layer in range(depth):
        last = layer == depth - 1
        rows = lat_rows if last else all_rows
        modl = modls[layer]
        log_gamma = jnp.stack([jax.nn.log_sigmoid(ret_decay_fwd[layer].astype(F32)),
                               jax.nn.log_sigmoid(ret_decay_bwd[layer].astype(F32))])

        proj_kv = _inproj_call(h, w_in, layer, all_rows, 0, kv_cols)
        proj_q = _inproj_call(h, w_in, layer, rows, kv_cols, in_cols)

        ya = _attn_call(proj_kv, proj_q, att_sink[layer], rope_att, cols, n_batch, seq, n_ctx, att_width, True)
        yr = _ret_call(proj_kv, proj_q, log_gamma, rope_ret, cols, n_batch, seq, n_ctx, ret_width, True)
        if not last:
            ya_c = _attn_call(proj_kv, proj_q, att_sink[layer], None, cols, n_batch, seq, n_ctx, att_width, False)
            yr_c = _ret_call(proj_kv, proj_q, log_gamma, None, cols, n_batch, seq, n_ctx, ret_width, False)
            ya = (ya, ya_c)
            yr = (yr, yr_c)

        z = _merge_gate_call(ya, yr, proj_q, w_branch_att, w_branch_ret, branch_gate_b, layer, rows, cols)
        xa = _resid_matmul_call(z, w_out, layer, xa, modl, 2, rows, seq, n_batch)

        h2, top_idx, top_w = _norm_call(xa, norm_ffn_g[layer], modl, rows, seq, n_batch, 3, 4,
                                        router=(router_w[layer], router_b[layer]))
        n_tiles = -(-rows * TOP_K // EXPERT_TILE) + n_exp
        tile_expert, n_used, slot, last_tile, next_expert = _routing(
            top_idx[:, :TOP_K], n_exp, EXPERT_TILE, n_tiles)
        xs = _dispatch_call(h2, slot, last_tile, n_used, rows, n_tiles, n_exp)
        y_sorted = _expert_call(xs, exp_w_gu, exp_b_gu, exp_w_down, exp_b_down, layer,
                                tile_expert, n_used, next_expert, n_tiles)
        if last:
            out = _combine_call(y_sorted, slot, top_w, xa, modl, 5, rows, seq, n_batch,
                                post="final", post_args=(final_norm_g,))
        else:
            xa, h = _combine_call(y_sorted, slot, top_w, xa, modl, 5, rows, seq, n_batch,
                                  post="next", post_args=(norm_mix_g[layer + 1], modls[layer + 1]))
    return out.reshape(n_batch, seq, d)
```

```python
import functools
import math

import jax
import jax.numpy as jnp
from jax import lax
from jax.experimental import pallas as pl
from jax.experimental.pallas import tpu as pltpu

F32 = jnp.float32
BF16 = jnp.bfloat16

GRID_W = 64
ATT_HEAD_DIM = 64
ATT_GROUP = 4
ATT_WINDOW = 128
ATT_BLOCK = 128
ATT_PAIRS_PER_STEP = 4
RET_HEADS = 8
RET_CHUNK = 128
RET_UNROLL_BWD = 4
RET_UNROLL_FWD = 8
TOP_K = 4
SWIGLU_ALPHA = 1.702
SWIGLU_LIMIT = 7.0
ROPE_BASE = 10000.0
NORM_EPS = 1e-6
GN_EPS = 1e-5
NEG_INF = -1e30
LOG2_E = 1.4426950408889634

LANES = 128
VMEM_LIMIT = 56 * 1024 * 1024
EXPERT_VMEM_LIMIT = 60 * 1024 * 1024


def _params(sem, vmem=VMEM_LIMIT):
    return pltpu.CompilerParams(dimension_semantics=sem, vmem_limit_bytes=vmem)


def _tile(n, cap, mult):
    for t in range(min(cap, n), 0, -1):
        if n % t == 0 and t % mult == 0:
            return t
    raise ValueError(f"no tile for {n} (cap {cap}, multiple of {mult})")


def _silu(v):
    return v * jax.nn.sigmoid(v)


def _mod_body(c_ref, w_ref, b_ref, o_ref):
    s = _silu(c_ref[...]).astype(BF16)
    o_ref[...] = jnp.dot(s, w_ref[...].astype(BF16), preferred_element_type=F32) + b_ref[...]


def _mod_call(cond, ada_w, ada_b):
    depth, d, n = ada_w.shape
    rows = cond.shape[0]
    tn = _tile(n, 512, LANES)
    return pl.pallas_call(
        _mod_body,
        grid=(depth, n // tn),
        in_specs=[
            pl.BlockSpec((rows, d), lambda l, j: (0, 0)),
            pl.BlockSpec((None, d, tn), lambda l, j: (l, 0, j)),
            pl.BlockSpec((None, 1, tn), lambda l, j: (l, 0, j)),
        ],
        out_specs=pl.BlockSpec((None, rows, tn), lambda l, j: (l, 0, j)),
        out_shape=jax.ShapeDtypeStruct((depth, rows, n), F32),
        compiler_params=_params(("arbitrary", "arbitrary")),
        name="adaln_mod",
    )(cond, ada_w, ada_b.reshape(depth, 1, n))


def _rms(x, g):
    ms = jnp.mean(x * x, axis=-1, keepdims=True)
    return x * lax.rsqrt(ms + NORM_EPS) * g


def _row_sources(src, tr, block_cols, col_of, row_axis):
    if not isinstance(src, tuple):
        return [src], [pl.BlockSpec((tr, block_cols), lambda *g: (g[row_axis], col_of(*g)))], None
    lat, cx = src
    assert lat.shape[0] % tr == 0 and cx.shape[0] % tr == 0
    n_lat = lat.shape[0] // tr
    specs = [pl.BlockSpec((tr, block_cols), lambda *g: (jnp.minimum(g[row_axis], n_lat - 1), col_of(*g))),
             pl.BlockSpec((tr, block_cols), lambda *g: (jnp.maximum(g[row_axis] - n_lat, 0), col_of(*g)))]
    return [lat, cx], specs, n_lat


def _read_rows(refs, n_lat, row_axis):
    if n_lat is None:
        return refs[0][...]
    return jnp.where(pl.program_id(row_axis) < n_lat, refs[0][...], refs[1][...])


def _norm_mod_body(*refs, n_lat):
    n_src = 1 if n_lat is None else 2
    g_ref, sh_ref, sc_ref, h_ref = refs[n_src:]
    h = _rms(_read_rows(refs[:n_src], n_lat, 0), g_ref[...]) * (1.0 + sc_ref[...]) + sh_ref[...]
    h_ref[...] = h.astype(h_ref.dtype)


def _pack_halves(v):
    w = v.shape[1] // 2
    bits = lax.bitcast_convert_type(v.astype(BF16).astype(F32), jnp.int32)
    return lax.shift_right_logical(bits[:, :w], 16) | (bits[:, w:] & jnp.int32(-65536))


def _unpack_halves(p):
    lo = lax.bitcast_convert_type(lax.shift_left(p, 16), F32)
    hi = lax.bitcast_convert_type(p & jnp.int32(-65536), F32)
    return lo, hi


def _split_bf16(v):
    hi = v.astype(BF16)
    lo = (v - hi.astype(F32)).astype(BF16)
    return hi, lo


def _norm_router_body(x_ref, g_ref, sh_ref, sc_ref, rw_ref, rb_ref, h_ref, idx_ref, wt_ref):
    h = _rms(x_ref[...], g_ref[...]) * (1.0 + sc_ref[...]) + sh_ref[...]
    h_ref[...] = _pack_halves(h)
    h_hi, h_lo = _split_bf16(h)
    w_hi, w_lo = _split_bf16(rw_ref[...])
    n_exp = w_hi.shape[1]
    both = jnp.dot(h_hi, jnp.concatenate([w_hi, w_lo], axis=1), preferred_element_type=F32)
    logits = (both[:, :n_exp] + both[:, n_exp:]
              + jnp.dot(h_lo, w_hi, preferred_element_type=F32)) + rb_ref[...]
    lane = lax.broadcasted_iota(jnp.int32, logits.shape, 1).astype(F32)
    work = logits
    out_lane = lax.broadcasted_iota(jnp.int32, idx_ref.shape, 1)
    idx = jnp.zeros(idx_ref.shape, F32)
    ex = jnp.zeros(wt_ref.shape, F32)
    top = None
    total = None
    for k in range(TOP_K):
        m = jnp.max(work, axis=-1, keepdims=True)
        first = jnp.min(jnp.where(work == m, lane, float(n_exp)), axis=-1, keepdims=True)
        work = jnp.where(lane == first, -jnp.inf, work)
        if k == 0:
            top = m
        e = jnp.exp(m - top)
        total = e if k == 0 else total + e
        idx = jnp.where(out_lane == k, first, idx)
        ex = jnp.where(out_lane == k, e, ex)
    idx_ref[...] = idx.astype(jnp.int32)
    wt_ref[...] = ex / total


def _seg_map(tiles_per_batch, n_batch, which):
    def index(i):
        return (jnp.minimum(i // tiles_per_batch, n_batch) * 6 + which, 0, 0)
    return index


def _norm_call(xa, g, modl, rows, seq, n_batch, shift_idx, scale_idx, router=None):
    d = g.shape[0]
    tr = 256
    assert rows % tr == 0 and seq % tr == 0
    tpb = seq // tr
    row_spec = pl.BlockSpec((tr, d), lambda i: (i, 0))
    srcs, src_specs, n_lat = _row_sources(xa, tr, d, lambda i: 0, 0)
    in_specs = src_specs + [
        pl.BlockSpec((1, d), lambda i: (0, 0)),
        pl.BlockSpec((None, 1, d), _seg_map(tpb, n_batch, shift_idx)),
        pl.BlockSpec((None, 1, d), _seg_map(tpb, n_batch, scale_idx)),
    ]
    args = srcs + [g.reshape(1, d), modl, modl]
    if router is None:
        return pl.pallas_call(
            functools.partial(_norm_mod_body, n_lat=n_lat), grid=(rows // tr,), in_specs=in_specs,
            out_specs=row_spec, out_shape=jax.ShapeDtypeStruct((rows, d), BF16),
            compiler_params=_params(("arbitrary",)), name="norm_mod",
        )(*args)
    assert n_lat is None
    rw, rb = router
    n_exp = rw.shape[1]
    in_specs += [pl.BlockSpec((d, n_exp), lambda i: (0, 0)), pl.BlockSpec((1, n_exp), lambda i: (0, 0))]
    args += [rw, rb.reshape(1, n_exp)]
    return pl.pallas_call(
        _norm_router_body, grid=(rows // tr,), in_specs=in_specs,
        out_specs=[pl.BlockSpec((tr, d // 2), lambda i: (i, 0)),
                   pl.BlockSpec((tr, LANES), lambda i: (i, 0)), pl.BlockSpec((tr, LANES), lambda i: (i, 0))],
        out_shape=[jax.ShapeDtypeStruct((rows, d // 2), jnp.int32), jax.ShapeDtypeStruct((rows, LANES), jnp.int32),
                   jax.ShapeDtypeStruct((rows, LANES), F32)],
        compiler_params=_params(("arbitrary",)), name="norm_router",
    )(*args)


def _inproj_body(a_ref, w_ref, o_ref, wb_ref):
    @pl.when(pl.program_id(1) == 0)
    def _():
        wb_ref[...] = w_ref[...].astype(BF16)
    o_ref[...] = jnp.dot(a_ref[...], wb_ref[...], preferred_element_type=F32).astype(o_ref.dtype)


def _inproj_call(h, w_in, layer, rows, col_lo, col_hi):
    d = h.shape[1]
    tn = 1024
    assert col_lo % tn == 0 and col_hi % tn == 0
    tm = _tile(rows, 576, 16)
    j0 = col_lo // tn
    return pl.pallas_call(
        _inproj_body,
        grid=((col_hi - col_lo) // tn, rows // tm),
        in_specs=[
            pl.BlockSpec((tm, d), lambda j, i: (i, 0)),
            pl.BlockSpec((None, d, tn), lambda j, i: (layer, 0, j0 + j)),
        ],
        out_specs=pl.BlockSpec((tm, tn), lambda j, i: (i, j)),
        out_shape=jax.ShapeDtypeStruct((rows, col_hi - col_lo), BF16),
        scratch_shapes=[pltpu.VMEM((d, tn), BF16)],
        compiler_params=_params(("arbitrary", "arbitrary")),
        name="in_proj",
    )(h, w_in)


def _attn_body(*refs, local, scale):
    if local:
        (sink_ref, q_ref, kp_ref, kc_ref, kn_ref, vp_ref, vc_ref, vn_ref, kx_ref, vx_ref,
         cp_ref, cc_ref, cn_ref, sp_ref, sc_ref, sn_ref, bias_ref, o_ref) = refs
    else:
        sink_ref, q_ref, kx_ref, vx_ref, o_ref = refs
    blk = ATT_BLOCK
    half = ATT_HEAD_DIM // 2
    pair_w = 2 * ATT_HEAD_DIM
    n_pairs = kx_ref.shape[1] // pair_w
    slabs_per_pair = ATT_GROUP

    lane_q = lax.broadcasted_iota(jnp.int32, (blk, LANES), 1)
    first_half = (lane_q % ATT_HEAD_DIM) < half

    def rope(v, c_ref, s_ref):
        rot = jnp.where(first_half, pltpu.roll(v, LANES - half, 1), pltpu.roll(v, half, 1))
        return v * c_ref[...] + rot * s_ref[...]

    def placed(v):
        low = lax.broadcasted_iota(jnp.int32, v.shape, 1) < ATT_HEAD_DIM
        swapped = pltpu.roll(v, ATT_HEAD_DIM, 1)
        return [[jnp.where(low, v, 0.0).astype(BF16), jnp.where(low, 0.0, swapped).astype(BF16)],
                [jnp.where(low, swapped, 0.0).astype(BF16), jnp.where(low, 0.0, v).astype(BF16)]]

    top_rows = lax.broadcasted_iota(jnp.int32, (2 * blk, 1), 0) < blk
    work = [(pp, j, hh) for pp in range(n_pairs) for j in range(2) for hh in range(2)]

    v_vars, scores = [], []
    for pp in range(n_pairs):
        lanes = slice(pp * pair_w, (pp + 1) * pair_w)
        if local:
            keys = jnp.concatenate([
                rope(kp_ref[:, lanes].astype(F32), cp_ref, sp_ref),
                rope(kc_ref[:, lanes].astype(F32), cc_ref, sc_ref),
                rope(kn_ref[:, lanes].astype(F32), cn_ref, sn_ref),
                kx_ref[:, lanes].astype(F32)], axis=0)
            vals = jnp.concatenate([vp_ref[:, lanes], vc_ref[:, lanes], vn_ref[:, lanes], vx_ref[:, lanes]],
                                   axis=0).astype(F32)
        else:
            keys = kx_ref[:, lanes].astype(F32)
            vals = vx_ref[:, lanes].astype(F32)
        k_var = placed(keys)
        v_vars.append(placed(vals))
        q_slabs = []
        for c in range(slabs_per_pair):
            c0 = (pp * slabs_per_pair + c) * LANES
            qc = q_ref[:, c0:c0 + LANES].astype(F32)
            if local:
                qc = rope(qc, cc_ref, sc_ref)
            q_slabs.append((qc * (scale * LOG2_E)).astype(BF16))
        for j in range(2):
            lhs = jnp.concatenate([q_slabs[2 * j], q_slabs[2 * j + 1]], axis=0)
            for hh in range(2):
                s = lax.dot_general(lhs, k_var[j][hh], (((1,), (1,)), ((), ())),
                                    preferred_element_type=F32)
                if local:
                    s = s + bias_ref[...]
                scores.append(s)

    probs = []
    for (pp, j, hh), s in zip(work, scores):
        head0 = (pl.program_id(1) * n_pairs + pp) * (2 * ATT_GROUP) + ATT_GROUP * j + hh
        sink = jnp.where(top_rows, sink_ref[head0], sink_ref[head0 + 2]) * LOG2_E
        m = jnp.maximum(jnp.max(s, axis=-1, keepdims=True), sink)
        e = jnp.exp2(s - m)
        denom = jnp.sum(e, axis=-1, keepdims=True) + jnp.exp2(sink - m)
        probs.append((e.astype(BF16), 1.0 / denom))

    out = [jnp.zeros((blk, LANES), F32) for _ in range(n_pairs * slabs_per_pair)]
    for (pp, j, hh), (e, inv) in zip(work, probs):
        o = jnp.dot(e, v_vars[pp][j][hh], preferred_element_type=F32) * inv
        c = pp * slabs_per_pair + 2 * j
        out[c] = out[c] + o[:blk]
        out[c + 1] = out[c + 1] + o[blk:]
    for c, val in enumerate(out):
        o_ref[:, c * LANES:(c + 1) * LANES] = val.astype(o_ref.dtype)


def _attn_call(proj_kv, proj_q, sink, rope_tabs, cols, n_batch, seq, n_ctx, att_width, local):
    blk = ATT_BLOCK
    nb = seq // blk
    pair_w = 2 * ATT_HEAD_DIM * ATT_PAIRS_PER_STEP
    q_w = pair_w * ATT_GROUP
    assert att_width % q_w == 0
    n_pair = att_width // q_w
    k0 = cols["k_att"] // pair_w
    v0 = cols["v_att"] // pair_w
    q0 = cols["q_att"] // q_w
    ctx_row0 = (n_batch * seq) // n_ctx
    kx_spec = pl.BlockSpec((n_ctx, pair_w), lambda b, p, n: (ctx_row0 + b, k0 + p))
    vx_spec = pl.BlockSpec((n_ctx, pair_w), lambda b, p, n: (ctx_row0 + b, v0 + p))
    sink_spec = pl.BlockSpec(memory_space=pltpu.SMEM)
    scale = ATT_HEAD_DIM ** -0.5
    if local:
        cos_t, sin_t = rope_tabs

        def prev(n):
            return jnp.maximum(n - 1, 0)

        def nxt(n):
            return jnp.minimum(n + 1, nb - 1)

        def kv(c0, f):
            return pl.BlockSpec((blk, pair_w), lambda b, p, n: (b * nb + f(n), c0 + p))

        def tab(f):
            return pl.BlockSpec((blk, LANES), lambda b, p, n: (f(n), 0))

        ident = lambda n: n
        assert nb >= 2
        r = jnp.arange(2 * blk, dtype=jnp.int32)[:, None] % blk
        c = jnp.arange(3 * blk + n_ctx, dtype=jnp.int32)[None, :]
        band = (jnp.abs(c - blk - r) <= ATT_WINDOW) | (c >= 3 * blk)
        edge = jnp.stack([band & (c >= blk), band, band & ((c < 2 * blk) | (c >= 3 * blk))])
        bias = jnp.where(edge, 0.0, NEG_INF).astype(F32)
        bias_spec = pl.BlockSpec((None, 2 * blk, 3 * blk + n_ctx),
                                 lambda b, p, n: ((n > 0).astype(jnp.int32) + (n == nb - 1).astype(jnp.int32), 0, 0))
        in_specs = [sink_spec,
                    pl.BlockSpec((blk, q_w), lambda b, p, n: (b * nb + n, q0 + p)),
                    kv(k0, prev), kv(k0, ident), kv(k0, nxt),
                    kv(v0, prev), kv(v0, ident), kv(v0, nxt),
                    kx_spec, vx_spec,
                    tab(prev), tab(ident), tab(nxt), tab(prev), tab(ident), tab(nxt), bias_spec]
        args = [sink, proj_q] + [proj_kv] * 6 + [proj_kv, proj_kv] + [cos_t] * 3 + [sin_t] * 3 + [bias]
        grid = (n_batch, n_pair, nb)
        rows = n_batch * seq
        out_spec = pl.BlockSpec((blk, q_w), lambda b, p, n: (b * nb + n, p))
    else:
        nqb = n_ctx // blk
        qrow0 = (n_batch * seq) // blk
        in_specs = [sink_spec,
                    pl.BlockSpec((blk, q_w), lambda b, p, n: (qrow0 + b * nqb + n, q0 + p)),
                    kx_spec, vx_spec]
        args = [sink, proj_q, proj_kv, proj_kv]
        grid = (n_batch, n_pair, nqb)
        rows = n_batch * n_ctx
        out_spec = pl.BlockSpec((blk, q_w), lambda b, p, n: (b * nqb + n, p))
    return pl.pallas_call(
        functools.partial(_attn_body, local=local, scale=scale),
        grid=grid, in_specs=in_specs, out_specs=out_spec,
        out_shape=jax.ShapeDtypeStruct((rows, att_width), BF16),
        compiler_params=_params(("arbitrary",) * 3),
        name="window_attn" if local else "ctx_attn",
    )(*args)


def _dot_tn(a, b):
    return lax.dot_general(a, b, (((0,), (0,)), ((), ())), preferred_element_type=F32)


def _ret_body(*refs, latent, n_chunks, n_ctx, k_scale):
    if latent:
        (lg_ref, q_ref, k_ref, v_ref, g_ref, kx_ref, vx_ref, cos_ref, sin_ref,
         o_ref, qs_ref, ks_ref, cb_ref, st_ref, sf_ref) = refs
    else:
        lg_ref, q_ref, k_ref, v_ref, g_ref, o_ref, qs_ref, ks_ref, cb_ref, st_ref, sf_ref = refs
    head = pl.program_id(1)
    lg_f = lg_ref[0, head]
    lg_b = lg_ref[1, head]
    c = RET_CHUNK
    hd = q_ref.shape[1]
    hh = hd // 2
    idx = lax.broadcasted_iota(jnp.int32, (c, 1), 0).astype(F32)
    qdec_f = jnp.exp(lg_f * (idx + 1.0))
    kdec_f = jnp.exp(lg_f * (c - 1.0 - idx))
    qdec_b = jnp.exp(lg_b * (c - idx))
    kdec_b = jnp.exp(lg_b * idx)
    one = jnp.ones((1, 1), F32)
    cdec_f = jnp.exp(lg_f * c * one)
    cdec_b = jnp.exp(lg_b * c * one)
    rel = (lax.broadcasted_iota(jnp.int32, (c, c), 0) - lax.broadcasted_iota(jnp.int32, (c, c), 1)).astype(F32)
    dmat = (jnp.where(rel >= 0, jnp.exp(lg_f * jnp.maximum(rel, 0.0)), 0.0)
            + jnp.where(rel <= 0, jnp.exp(lg_b * jnp.maximum(-rel, 0.0)), 0.0))

    def rope(v, rows):
        v1 = v[:, :hh]
        v2 = v[:, hh:]
        cs = cos_ref[rows, :]
        sn = sin_ref[rows, :]
        return jnp.concatenate([v1 * cs - v2 * sn, v2 * cs + v1 * sn], axis=1)

    if latent:
        t = lax.broadcasted_iota(jnp.int32, (n_ctx, 1), 0).astype(F32)
        kx = kx_ref[...].astype(F32) * k_scale
        sf_ref[...] = _dot_tn((kx * jnp.exp(lg_f * (n_ctx - 1.0 - t))).astype(BF16), vx_ref[...])
        st_ref[...] = _dot_tn((kx * jnp.exp(lg_b * t)).astype(BF16), vx_ref[...])
    else:
        sf_ref[...] = jnp.zeros((hd, hd), F32)
        st_ref[...] = jnp.zeros((hd, hd), F32)


    def bwd(step, carry):
        rows = pl.ds(pl.multiple_of((n_chunks - 1 - step) * c, c), c)
        q = q_ref[rows, :].astype(F32)
        k = k_ref[rows, :].astype(F32)
        if latent:
            q = rope(q, rows)
            k = rope(k, rows)
        k = k * k_scale
        qs_ref[rows, :] = q.astype(BF16)
        ks_ref[rows, :] = k.astype(BF16)
        state = st_ref[...]
        cb_ref[rows, :] = jnp.dot((q * qdec_b).astype(BF16), state.astype(BF16), preferred_element_type=F32)
        st_ref[...] = cdec_b * state + _dot_tn((k * kdec_b).astype(BF16), v_ref[rows, :])
        return carry

    lax.fori_loop(0, n_chunks, bwd, 0, unroll=min(RET_UNROLL_BWD, n_chunks))

    st_ref[...] = sf_ref[...]

    def fwd(step, carry):
        rows = pl.ds(pl.multiple_of(step * c, c), c)
        qb = qs_ref[rows, :]
        kb = ks_ref[rows, :]
        v = v_ref[rows, :]
        scores = lax.dot_general(qb, kb, (((1,), (1,)), ((), ())), preferred_element_type=F32) * dmat
        o = jnp.dot(scores.astype(BF16), v, preferred_element_type=F32)
        state = st_ref[...]
        o = o + jnp.dot((qb.astype(F32) * qdec_f).astype(BF16), state.astype(BF16), preferred_element_type=F32)
        o = o + cb_ref[rows, :]
        st_ref[...] = cdec_f * state + _dot_tn((kb.astype(F32) * kdec_f).astype(BF16), v)
        mu = jnp.mean(o, axis=-1, keepdims=True)
        dev = o - mu
        var = jnp.mean(dev * dev, axis=-1, keepdims=True)
        y = dev * lax.rsqrt(var + GN_EPS)
        o_ref[rows, :] = (_silu(g_ref[rows, :].astype(F32)) * y).astype(o_ref.dtype)
        return carry

    lax.fori_loop(0, n_chunks, fwd, 0, unroll=min(RET_UNROLL_FWD, n_chunks))


def _ret_call(proj_kv, proj_q, log_gamma, rope_tabs, cols, n_batch, seq, n_ctx, ret_width, latent):
    hd = ret_width // RET_HEADS
    t_len = seq if latent else n_ctx
    assert t_len % RET_CHUNK == 0
    k0, v0, q0, g0 = (cols[name] // hd for name in ("k_ret", "v_ret", "q_ret", "g_ret"))
    ctx_row0 = (n_batch * seq) // n_ctx
    row0 = 0 if latent else ctx_row0

    def seq_spec(c0):
        return pl.BlockSpec((t_len, hd), lambda b, h: (row0 + b, c0 + h))

    in_specs = [pl.BlockSpec(memory_space=pltpu.SMEM),
                seq_spec(q0), seq_spec(k0), seq_spec(v0), seq_spec(g0)]
    args = [log_gamma, proj_q, proj_kv, proj_kv, proj_q]
    if latent:
        cos_t, sin_t = rope_tabs
        in_specs += [pl.BlockSpec((n_ctx, hd), lambda b, h: (ctx_row0 + b, k0 + h)),
                     pl.BlockSpec((n_ctx, hd), lambda b, h: (ctx_row0 + b, v0 + h)),
                     pl.BlockSpec((t_len, hd // 2), lambda b, h: (0, 0)),
                     pl.BlockSpec((t_len, hd // 2), lambda b, h: (0, 0))]
        args += [proj_kv, proj_kv, cos_t, sin_t]
    return pl.pallas_call(
        functools.partial(_ret_body, latent=latent, n_chunks=t_len // RET_CHUNK, n_ctx=n_ctx,
                          k_scale=hd ** -0.5),
        grid=(n_batch, RET_HEADS), in_specs=in_specs,
        out_specs=pl.BlockSpec((t_len, hd), lambda b, h: (b, h)),
        out_shape=jax.ShapeDtypeStruct((n_batch * t_len, ret_width), BF16),
        scratch_shapes=[pltpu.VMEM((t_len, hd), BF16), pltpu.VMEM((t_len, hd), BF16),
                        pltpu.VMEM((t_len, hd), F32), pltpu.VMEM((hd, hd), F32), pltpu.VMEM((hd, hd), F32)],
        compiler_params=_params(("arbitrary", "arbitrary")),
        name="retention" if latent else "ctx_retention",
    )(*args)


def _merge_gate_body(*refs, n_lat):
    n_src = 1 if n_lat is None else 2
    ga_ref, gr_ref, wa_ref, wr_ref, ba_ref, br_ref, z_ref, wab_ref, wrb_ref = refs[2 * n_src:]

    @pl.when(pl.program_id(1) == 0)
    def _():
        wab_ref[...] = wa_ref[...].astype(BF16)
        wrb_ref[...] = wr_ref[...].astype(BF16)
    a = jnp.dot(_read_rows(refs[:n_src], n_lat, 1), wab_ref[...], preferred_element_type=F32)
    r = jnp.dot(_read_rows(refs[n_src:2 * n_src], n_lat, 1), wrb_ref[...], preferred_element_type=F32)
    ga = jax.nn.sigmoid(ga_ref[...].astype(F32) + ba_ref[...])
    gr = jax.nn.sigmoid(gr_ref[...].astype(F32) + br_ref[...])
    z_ref[...] = (ga * a + gr * r).astype(z_ref.dtype)


def _merge_gate_call(ya, yr, proj, w_att, w_ret, gate_b, layer, rows, cols):
    d = w_att.shape[2]
    wa_rows, wr_rows = w_att.shape[1], w_ret.shape[1]
    tn = 512
    if isinstance(ya, tuple):
        tm = _tile(math.gcd(ya[0].shape[0], ya[1].shape[0]), 1024, 16)
    else:
        tm = _tile(rows, 1024, 16)
    ga0 = cols["gate_att"] // tn
    gr0 = cols["gate_ret"] // tn
    gb = gate_b.reshape(gate_b.shape[0], 1, 2 * d)
    ya_srcs, ya_specs, n_lat = _row_sources(ya, tm, wa_rows, lambda j, i: 0, 1)
    yr_srcs, yr_specs, _ = _row_sources(yr, tm, wr_rows, lambda j, i: 0, 1)
    return pl.pallas_call(
        functools.partial(_merge_gate_body, n_lat=n_lat),
        grid=(d // tn, rows // tm),
        in_specs=ya_specs + yr_specs + [
            pl.BlockSpec((tm, tn), lambda j, i: (i, ga0 + j)),
            pl.BlockSpec((tm, tn), lambda j, i: (i, gr0 + j)),
            pl.BlockSpec((None, wa_rows, tn), lambda j, i: (layer, 0, j)),
            pl.BlockSpec((None, wr_rows, tn), lambda j, i: (layer, 0, j)),
            pl.BlockSpec((None, 1, tn), lambda j, i: (layer, 0, j)),
            pl.BlockSpec((None, 1, tn), lambda j, i: (layer, 0, d // tn + j)),
        ],
        out_specs=pl.BlockSpec((tm, tn), lambda j, i: (i, j)),
        out_shape=jax.ShapeDtypeStruct((rows, d), BF16),
        scratch_shapes=[pltpu.VMEM((wa_rows, tn), BF16), pltpu.VMEM((wr_rows, tn), BF16)],
        compiler_params=_params(("arbitrary", "arbitrary")),
        name="merge_gate",
    )(*ya_srcs, *yr_srcs, proj, proj, w_att, w_ret, gb, gb)


def _resid_matmul_body(z_ref, w_ref, *refs, n_lat):
    n_src = 1 if n_lat is None else 2
    g_ref, o_ref, wb_ref = refs[n_src:]

    @pl.when(pl.program_id(1) == 0)
    def _():
        wb_ref[...] = w_ref[...].astype(BF16)
    y = jnp.dot(z_ref[...], wb_ref[...], preferred_element_type=F32)
    o_ref[...] = _read_rows(refs[:n_src], n_lat, 1) + g_ref[...] * y


def _resid_matmul_call(z, w, layer, xa, modl, gate_idx, rows, seq, n_batch):
    k, d = w.shape[1], w.shape[2]
    tn = 512
    tm = 1024 if (rows % 1024 == 0 and seq % 1024 == 0 and not isinstance(xa, tuple)) else 512
    assert rows % tm == 0 and seq % tm == 0
    tpb = seq // tm

    def gate_map(j, i):
        return (jnp.minimum(i // tpb, n_batch) * 6 + gate_idx, 0, j)

    srcs, src_specs, n_lat = _row_sources(xa, tm, tn, lambda j, i: j, 1)
    return pl.pallas_call(
        functools.partial(_resid_matmul_body, n_lat=n_lat),
        grid=(d // tn, rows // tm),
        in_specs=[
            pl.BlockSpec((tm, k), lambda j, i: (i, 0)),
            pl.BlockSpec((None, k, tn), lambda j, i: (layer, 0, j)),
        ] + src_specs + [pl.BlockSpec((None, 1, tn), gate_map)],
        out_specs=pl.BlockSpec((tm, tn), lambda j, i: (i, j)),
        out_shape=jax.ShapeDtypeStruct((rows, d), F32),
        scratch_shapes=[pltpu.VMEM((k, tn), BF16)],
        compiler_params=_params(("arbitrary", "arbitrary")),
        name="resid_matmul",
    )(z, w, *srcs, modl)


EXPERT_TILE = 256
COMBINE_TILE = 128
DISPATCH_TILE = 256
COMBINE_CHUNK = 256


def _routing(top_idx, n_exp, tm, n_tiles):
    n_assign = top_idx.size
    e_flat = top_idx.reshape(n_assign)
    onehot = (e_flat[:, None] == jnp.arange(n_exp, dtype=jnp.int32)[None, :]).astype(jnp.int32)
    running = jnp.cumsum(onehot, axis=0)
    counts = running[-1]
    group = (counts + tm - 1) // tm * tm
    group_end = jnp.cumsum(group)
    slot = jnp.sum(onehot * (running - 1 + (group_end - group)[None, :]), axis=1).astype(jnp.int32)
    n_used = (group_end[-1] // tm).astype(jnp.int32).reshape(1)
    tile_start = jnp.arange(n_tiles, dtype=jnp.int32) * tm
    tile_expert = jnp.sum((group_end[None, :] <= tile_start[:, None]).astype(jnp.int32), axis=1)
    tile_expert = jnp.minimum(tile_expert, n_exp - 1).astype(jnp.int32)
    last_tile = jnp.stack([jnp.maximum(group_end - tm, 0), (group > 0).astype(jnp.int32)], axis=1)
    ids = jnp.arange(n_exp, dtype=jnp.int32)
    later = (ids[None, :] > ids[:, None]) & (group[None, :] > 0)
    next_of = jnp.min(jnp.where(later, ids[None, :], n_exp), axis=1)
    next_of = jnp.where(next_of == n_exp, -1, next_of)
    next_expert = jnp.sum((tile_expert[:, None] == ids[None, :]) * next_of[None, :], axis=1).astype(jnp.int32)
    return tile_expert, n_used, slot, last_tile.reshape(2 * n_exp).astype(jnp.int32), next_expert


def _expert_changed(te_ref, t):
    return jnp.logical_or(t == 0, te_ref[t] != te_ref[jnp.maximum(t - 1, 0)])


def _dispatch_body(slot_ref, last_ref, nu_ref, h_ref, xs_hbm, zero_ref, sem, zsem, *, tr, tm, n_exp, n_tiles):
    t = pl.program_id(0)

    @pl.when(t == 0)
    def _():
        zero_ref[...] = jnp.zeros(zero_ref.shape, zero_ref.dtype)

        def zero_copy(start):
            return pltpu.make_async_copy(zero_ref, xs_hbm.at[pl.ds(pl.multiple_of(start, tm), tm), :], zsem)

        def start_one(e, carry):
            @pl.when(last_ref[2 * e + 1] == 1)
            def _():
                zero_copy(last_ref[2 * e]).start()
            return carry

        def wait_one(e, carry):
            @pl.when(last_ref[2 * e + 1] == 1)
            def _():
                zero_copy(last_ref[2 * e]).wait()
            return carry

        def start_tail(i, carry):
            zero_copy(i * tm).start()
            return carry

        def wait_tail(i, carry):
            zero_copy(i * tm).wait()
            return carry

        lax.fori_loop(0, n_exp, start_one, 0)
        lax.fori_loop(nu_ref[0], n_tiles, start_tail, 0)
        lax.fori_loop(0, n_exp, wait_one, 0)
        lax.fori_loop(nu_ref[0], n_tiles, wait_tail, 0)

    base = t * (tr * TOP_K)

    def body(r, carry):
        for k in range(TOP_K):
            row = slot_ref[base + r * TOP_K + k]
            pltpu.async_copy(h_ref.at[pl.ds(r, 1), :], xs_hbm.at[pl.ds(row, 1), :], sem, priority=k % 2)
        return carry
    lax.fori_loop(0, tr, body, 0, unroll=2)

    for _ in range(TOP_K):
        pltpu.make_async_copy(h_ref, xs_hbm.at[pl.ds(0, tr), :], sem).wait()


def _dispatch_call(h2p, slot, last_tile, n_used, rows, n_tiles, n_exp):
    w = h2p.shape[1]
    tr = DISPATCH_TILE
    tm = EXPERT_TILE
    assert rows % tr == 0
    grid_spec = pltpu.PrefetchScalarGridSpec(
        num_scalar_prefetch=3,
        grid=(rows // tr,),
        in_specs=[pl.BlockSpec((tr, w), lambda t, sl, lt, nu: (t, 0))],
        out_specs=pl.BlockSpec(memory_space=pl.ANY),
        scratch_shapes=[pltpu.VMEM((tm, w), jnp.int32), pltpu.SemaphoreType.DMA(()), pltpu.SemaphoreType.DMA(())],
    )
    return pl.pallas_call(
        functools.partial(_dispatch_body, tr=tr, tm=tm, n_exp=n_exp, n_tiles=n_tiles),
        grid_spec=grid_spec,
        out_shape=jax.ShapeDtypeStruct((n_tiles * tm, w), jnp.int32),
        compiler_params=_params(("arbitrary",)),
        name="expert_dispatch",
    )(slot, last_tile, n_used, h2p)


def _expert_body(te_ref, nu_ref, nx_ref, x_ref, wgu_hbm, bgu_ref, sel_ref, wd_hbm, bd_ref, y_ref,
                 sgu_ref, sd_ref, wgu_ref, wd_ref, sem, *, layer):
    t = pl.program_id(0)
    n_used = nu_ref[0]

    def weight_copies(e):
        return (pltpu.make_async_copy(wgu_hbm.at[layer, e], sgu_ref, sem.at[0]),
                pltpu.make_async_copy(wd_hbm.at[layer, e], sd_ref, sem.at[1]))

    def fetch(e):
        for copy in weight_copies(e):
            copy.start()

    @pl.when(t == 0)
    def _():
        fetch(te_ref[0])

    @pl.when(t < n_used)
    def _():
        @pl.when(_expert_changed(te_ref, t))
        def _():
            for copy in weight_copies(te_ref[t]):
                copy.wait()
            wgu_ref[...] = sgu_ref[...].astype(BF16)
            wd_ref[...] = sd_ref[...].astype(BF16)

            @pl.when(nx_ref[t] >= 0)
            def _():
                fetch(nx_ref[t])

        x_lo, x_hi = _unpack_halves(x_ref[...])
        half = x_lo.shape[1]
        gu = (jnp.dot(x_lo.astype(BF16), wgu_ref[:half, :], preferred_element_type=F32)
              + jnp.dot(x_hi.astype(BF16), wgu_ref[half:, :], preferred_element_type=F32)) + bgu_ref[...]
        n2 = gu.shape[1]
        even = lax.broadcasted_iota(jnp.int32, gu.shape, 1) % 2 == 0
        gate = jnp.minimum(gu, SWIGLU_LIMIT)
        up = jnp.clip(pltpu.roll(gu, n2 - 1, 1), -SWIGLU_LIMIT, SWIGLU_LIMIT)
        act = jnp.where(even, (up + 1.0) * (gate * jax.nn.sigmoid(SWIGLU_ALPHA * gate)), 0.0)
        a = jnp.dot(act.astype(BF16), sel_ref[...], preferred_element_type=F32).astype(BF16)
        y = jnp.dot(a, wd_ref[...], preferred_element_type=F32) + bd_ref[...]
        y_ref[...] = _pack_halves(y)

    @pl.when(t >= n_used)
    def _():
        y_ref[...] = jnp.zeros(y_ref.shape, y_ref.dtype)


def _used_tile(t, nu):
    return jnp.minimum(t, nu[0] - 1)


def _expert_call(xs, w_gu, b_gu, w_down, b_down, layer, tile_expert, n_used, next_expert, n_tiles):
    n_exp, d, ff2 = w_gu.shape[1:]
    ff = ff2 // 2
    tm = EXPERT_TILE
    sel = (jnp.arange(ff2, dtype=jnp.int32)[:, None] == 2 * jnp.arange(ff, dtype=jnp.int32)[None, :]).astype(BF16)
    grid_spec = pltpu.PrefetchScalarGridSpec(
        num_scalar_prefetch=3,
        grid=(n_tiles,),
        in_specs=[
            pl.BlockSpec((tm, d // 2), lambda t, te, nu, nx: (_used_tile(t, nu), 0)),
            pl.BlockSpec(memory_space=pl.ANY),
            pl.BlockSpec((None, None, 1, ff2), lambda t, te, nu, nx: (layer, te[t], 0, 0)),
            pl.BlockSpec((ff2, ff), lambda t, te, nu, nx: (0, 0)),
            pl.BlockSpec(memory_space=pl.ANY),
            pl.BlockSpec((None, None, 1, d), lambda t, te, nu, nx: (layer, te[t], 0, 0)),
        ],
        out_specs=pl.BlockSpec((tm, d // 2), lambda t, te, nu, nx: (t, 0)),
        scratch_shapes=[pltpu.VMEM((d, ff2), F32), pltpu.VMEM((ff, d), F32),
                        pltpu.VMEM((d, ff2), BF16), pltpu.VMEM((ff, d), BF16),
                        pltpu.SemaphoreType.DMA((2,))],
    )
    return pl.pallas_call(
        functools.partial(_expert_body, layer=layer),
        grid_spec=grid_spec,
        out_shape=jax.ShapeDtypeStruct((n_tiles * tm, d // 2), jnp.int32),
        compiler_params=_params(("arbitrary",), EXPERT_VMEM_LIMIT),
        name="expert_mlp",
    )(tile_expert, n_used, next_expert, xs, w_gu, b_gu.reshape(b_gu.shape[0], n_exp, 1, ff2), sel,
      w_down, b_down.reshape(b_down.shape[0], n_exp, 1, d))


def _combine_body(pos_ref, y_hbm, w_ref, x_ref, g_ref, *refs, tc, post):
    if post == "next":
        ng_ref, nsh_ref, nsc_ref, o_ref, h_ref, ybuf, sem = refs
    elif post == "final":
        ng_ref, o_ref, ybuf, sem = refs
    else:
        o_ref, ybuf, sem = refs
    t = pl.program_id(0)
    last = pl.num_programs(0) - 1
    slot = lax.rem(t, 2)
    other = 1 - slot

    def row_copies(base, s, r):
        for k in range(TOP_K):
            row = pos_ref[base + r * TOP_K + k]
            pltpu.async_copy(y_hbm.at[pl.ds(row, 1), :], ybuf.at[s, k, pl.ds(r, 1), :], sem.at[s], priority=k % 2)

    def drain(s):
        for k in range(TOP_K):
            pltpu.make_async_copy(y_hbm.at[pl.ds(0, tc), :], ybuf.at[s, k], sem.at[s]).wait()

    @pl.when(t == 0)
    def _():
        def body(r, carry):
            row_copies(0, 0, r)
            return carry
        lax.fori_loop(0, tc, body, 0, unroll=2)

    drain(slot)

    next_base = jnp.minimum(t + 1, last) * (tc * TOP_K)
    half = ybuf.shape[-1]
    ch = COMBINE_CHUNK
    n_first = half // ch
    n_second = 0 if post is None else 2 * half // ch
    rows_first = (tc if post is None else tc // 2) // n_first
    rows_second = 0 if post is None else (tc - rows_first * n_first) // n_second
    assert rows_first * n_first + rows_second * n_second == tc
    issued = [0]

    def issue_rows(count):
        for r in range(issued[0], issued[0] + count):
            row_copies(next_base, other, r)
        issued[0] += count

    w = w_ref[...]
    wk = [jnp.broadcast_to(w[:, k:k + 1], (tc, LANES)) for k in range(TOP_K)]
    sq = jnp.zeros((tc, 1), F32)
    for c0 in range(0, half, ch):
        issue_rows(rows_first)
        acc_lo = acc_hi = None
        for k in range(TOP_K):
            lo, hi = _unpack_halves(ybuf[slot, k, :, c0:c0 + ch])
            wt = jnp.tile(wk[k], (1, ch // LANES))
            acc_lo = wt * lo if k == 0 else acc_lo + wt * lo
            acc_hi = wt * hi if k == 0 else acc_hi + wt * hi
        for acc, c in ((acc_lo, c0), (acc_hi, half + c0)):
            new = x_ref[:, c:c + ch] + g_ref[:, c:c + ch] * acc
            o_ref[:, c:c + ch] = new
            if post is not None:
                sq = sq + jnp.sum(new * new, axis=-1, keepdims=True)
    if post is not None:
        inv = lax.rsqrt(sq * (1.0 / (2 * half)) + NORM_EPS)
        for c in range(0, 2 * half, ch):
            issue_rows(rows_second)
            norm = o_ref[:, c:c + ch] * inv * ng_ref[:, c:c + ch]
            if post == "final":
                o_ref[:, c:c + ch] = norm
            else:
                h_ref[:, c:c + ch] = (norm * (1.0 + nsc_ref[:, c:c + ch])
                                      + nsh_ref[:, c:c + ch]).astype(h_ref.dtype)

    @pl.when(t == last)
    def _():
        drain(other)


def _combine_call(y_sorted, slot, top_w, xa, modl, gate_idx, rows, seq, n_batch, post=None, post_args=()):
    d = xa.shape[1]
    tc = COMBINE_TILE
    assert rows % tc == 0 and seq % tc == 0
    tpb = seq // tc

    def seg_row(which):
        return lambda t, pos: (jnp.minimum(t // tpb, n_batch) * 6 + which, 0, 0)

    row_spec = pl.BlockSpec((tc, d), lambda t, pos: (t, 0))
    in_specs = [
        pl.BlockSpec(memory_space=pl.ANY),
        pl.BlockSpec((tc, LANES), lambda t, pos: (t, 0)),
        row_spec,
        pl.BlockSpec((None, 1, d), seg_row(gate_idx)),
    ]
    args = [slot, y_sorted, top_w, xa, modl]
    out_specs = row_spec
    out_shape = jax.ShapeDtypeStruct((rows, d), F32)
    if post is not None:
        in_specs.append(pl.BlockSpec((1, d), lambda t, pos: (0, 0)))
        args.append(post_args[0].reshape(1, d))
    if post == "next":
        in_specs += [pl.BlockSpec((None, 1, d), seg_row(0)), pl.BlockSpec((None, 1, d), seg_row(1))]
        args += [post_args[1], post_args[1]]
        out_specs = [row_spec, row_spec]
        out_shape = [out_shape, jax.ShapeDtypeStruct((rows, d), BF16)]
    grid_spec = pltpu.PrefetchScalarGridSpec(
        num_scalar_prefetch=1,
        grid=(rows // tc,),
        in_specs=in_specs,
        out_specs=out_specs,
        scratch_shapes=[pltpu.VMEM((2, TOP_K, tc, d // 2), jnp.int32), pltpu.SemaphoreType.DMA((2,))],
    )
    return pl.pallas_call(
        functools.partial(_combine_body, tc=tc, post=post),
        grid_spec=grid_spec,
        out_shape=out_shape,
        compiler_params=_params(("arbitrary",)),
        name="expert_combine",
    )(*args)


def _axial_angles(seq, head_dim):
    rows = seq // GRID_W
    row = jnp.repeat(jnp.arange(rows, dtype=F32), GRID_W)
    col = jnp.tile(jnp.arange(GRID_W, dtype=F32), rows)
    n_freq = head_dim // 4
    inv_freq = ROPE_BASE ** (-jnp.arange(n_freq, dtype=F32) / n_freq)
    return jnp.concatenate([row[:, None] * inv_freq, col[:, None] * inv_freq], axis=-1)


def _rope_tables(seq, ret_head_dim):
    ang_a = _axial_angles(seq, ATT_HEAD_DIM)
    reps = LANES // ATT_HEAD_DIM
    cos_a = jnp.tile(jnp.concatenate([jnp.cos(ang_a), jnp.cos(ang_a)], axis=-1), (1, reps))
    sin_a = jnp.tile(jnp.concatenate([-jnp.sin(ang_a), jnp.sin(ang_a)], axis=-1), (1, reps))
    ang_r = _axial_angles(seq, ret_head_dim)
    return (cos_a, sin_a), (jnp.cos(ang_r), jnp.sin(ang_r))


def kernel(x, c, ctx, c_ctx, ada_w, ada_b, norm_mix_g, norm_ffn_g, w_in, att_sink, ret_decay_fwd,
           ret_decay_bwd, w_branch_att, w_branch_ret, branch_gate_b, w_out, router_w, router_b,
           exp_w_gu, exp_b_gu, exp_w_down, exp_b_down, final_norm_g):
    n_batch, seq, d = x.shape
    n_ctx = ctx.shape[1]
    depth = ada_w.shape[0]
    n_exp, ff = exp_w_down.shape[1], exp_w_down.shape[2]
    att_width = w_branch_att.shape[1]
    ret_width = w_branch_ret.shape[1]
    kv_width = att_width // ATT_GROUP
    sizes = (("k_att", kv_width), ("v_att", kv_width), ("k_ret", ret_width), ("v_ret", ret_width),
             ("q_att", att_width), ("q_ret", ret_width), ("g_ret", ret_width),
             ("gate_att", d), ("gate_ret", d))
    cols, off = {}, 0
    kv_cols = 2 * kv_width + 2 * ret_width
    for name, width in sizes:
        cols[name] = off if off < kv_cols else off - kv_cols
        off += width
    in_cols = off
    assert in_cols == w_in.shape[2]

    lat_rows = n_batch * seq
    all_rows = lat_rows + n_batch * n_ctx
    xa = (x.reshape(lat_rows, d), ctx.reshape(n_batch * n_ctx, d))

    cond = jnp.zeros((16, d), F32).at[:n_batch].set(c).at[n_batch].set(c_ctx)
    mod = _mod_call(cond, ada_w, ada_b)
    rope_att, rope_ret = _rope_tables(seq, ret_width // RET_HEADS)

    modls = [mod[layer, :n_batch + 1].reshape((n_batch + 1) * 6, 1, d) for layer in range(depth)]
    h = _norm_call(xa, norm_mix_g[0], modls[0], all_rows, seq, n_batch, 0, 1)
    out = None
    for layer in range(depth):
        last = layer == depth - 1
        rows = lat_rows if last else all_rows
        modl = modls[layer]
        log_gamma = jnp.stack([jax.nn.log_sigmoid(ret_decay_fwd[layer].astype(F32)),
                               jax.nn.log_sigmoid(ret_decay_bwd[layer].astype(F32))])

        proj_kv = _inproj_call(h, w_in, layer, all_rows, 0, kv_cols)
        proj_q = _inproj_call(h, w_in, layer, rows, kv_cols, in_cols)

        ya = _attn_call(proj_kv, proj_q, att_sink[layer], rope_att, cols, n_batch, seq, n_ctx, att_width, True)
        yr = _ret_call(proj_kv, proj_q, log_gamma, rope_ret, cols, n_batch, seq, n_ctx, ret_width, True)
        if not last:
            ya_c = _attn_call(proj_kv, proj_q, att_sink[layer], None, cols, n_batch, seq, n_ctx, att_width, False)
            yr_c = _ret_call(proj_kv, proj_q, log_gamma, None, cols, n_batch, seq, n_ctx, ret_width, False)
            ya = (ya, ya_c)
            yr = (yr, yr_c)

        z = _merge_gate_call(ya, yr, proj_q, w_branch_att, w_branch_ret, branch_gate_b, layer, rows, cols)
        xa = _resid_matmul_call(z, w_out, layer, xa, modl, 2, rows, seq, n_batch)

        h2, top_idx, top_w = _norm_call(xa, norm_ffn_g[layer], modl, rows, seq, n_batch, 3, 4,
                                        router=(router_w[layer], router_b[layer]))
        n_tiles = -(-rows * TOP_K // EXPERT_TILE) + n_exp
        tile_expert, n_used, slot, last_tile, next_expert = _routing(
            top_idx[:, :TOP_K], n_exp, EXPERT_TILE, n_tiles)
        xs = _dispatch_call(h2, slot, last_tile, n_used, rows, n_tiles, n_exp)
        y_sorted = _expert_call(xs, exp_w_gu, exp_b_gu, exp_w_down, exp_b_down, layer,
                                tile_expert, n_used, next_expert, n_tiles)
        if last:
            out = _combine_call(y_sorted, slot, top_w, xa, modl, 5, rows, seq, n_batch,
                                post="final", post_args=(final_norm_g,))
        else:
            xa, h = _combine_call(y_sorted, slot, top_w, xa, modl, 5, rows, seq, n_batch,
                                  post="next", post_args=(norm_mix_g[layer + 1], modls[layer + 1]))
    return out.reshape(n_batch, seq, d)
```

```python
import functools
import math

import jax
import jax.numpy as jnp
from jax import lax
from jax.experimental import pallas as pl
from jax.experimental.pallas import tpu as pltpu

F32 = jnp.float32
BF16 = jnp.bfloat16

GRID_W = 64
ATT_HEAD_DIM = 64
ATT_GROUP = 4
ATT_WINDOW = 128
ATT_BLOCK = 128
ATT_PAIRS_PER_STEP = 4
RET_HEADS = 8
RET_CHUNK = 128
RET_UNROLL_BWD = 4
RET_UNROLL_FWD = 8
TOP_K = 4
SWIGLU_ALPHA = 1.702
SWIGLU_LIMIT = 7.0
ROPE_BASE = 10000.0
NORM_EPS = 1e-6
GN_EPS = 1e-5
NEG_INF = -1e30
LOG2_E = 1.4426950408889634

LANES = 128
VMEM_LIMIT = 56 * 1024 * 1024
EXPERT_VMEM_LIMIT = 60 * 1024 * 1024


def _params(sem, vmem=VMEM_LIMIT):
    return pltpu.CompilerParams(dimension_semantics=sem, vmem_limit_bytes=vmem)


def _tile(n, cap, mult):
    for t in range(min(cap, n), 0, -1):
        if n % t == 0 and t % mult == 0:
            return t
    raise ValueError(f"no tile for {n} (cap {cap}, multiple of {mult})")


def _silu(v):
    return v * jax.nn.sigmoid(v)


def _mod_body(c_ref, w_ref, b_ref, o_ref):
    s = _silu(c_ref[...]).astype(BF16)
    o_ref[...] = jnp.dot(s, w_ref[...].astype(BF16), preferred_element_type=F32) + b_ref[...]


def _mod_call(cond, ada_w, ada_b):
    depth, d, n = ada_w.shape
    rows = cond.shape[0]
    tn = _tile(n, 512, LANES)
    return pl.pallas_call(
        _mod_body,
        grid=(depth, n // tn),
        in_specs=[
            pl.BlockSpec((rows, d), lambda l, j: (0, 0)),
            pl.BlockSpec((None, d, tn), lambda l, j: (l, 0, j)),
            pl.BlockSpec((None, 1, tn), lambda l, j: (l, 0, j)),
        ],
        out_specs=pl.BlockSpec((None, rows, tn), lambda l, j: (l, 0, j)),
        out_shape=jax.ShapeDtypeStruct((depth, rows, n), F32),
        compiler_params=_params(("arbitrary", "arbitrary")),
        name="adaln_mod",
    )(cond, ada_w, ada_b.reshape(depth, 1, n))


def _rms(x, g):
    ms = jnp.mean(x * x, axis=-1, keepdims=True)
    return x * lax.rsqrt(ms + NORM_EPS) * g


def _row_sources(src, tr, block_cols, col_of, row_axis):
    if not isinstance(src, tuple):
        return [src], [pl.BlockSpec((tr, block_cols), lambda *g: (g[row_axis], col_of(*g)))], None
    lat, cx = src
    assert lat.shape[0] % tr == 0 and cx.shape[0] % tr == 0
    n_lat = lat.shape[0] // tr
    specs = [pl.BlockSpec((tr, block_cols), lambda *g: (jnp.minimum(g[row_axis], n_lat - 1), col_of(*g))),
             pl.BlockSpec((tr, block_cols), lambda *g: (jnp.maximum(g[row_axis] - n_lat, 0), col_of(*g)))]
    return [lat, cx], specs, n_lat


def _read_rows(refs, n_lat, row_axis):
    if n_lat is None:
        return refs[0][...]
    return jnp.where(pl.program_id(row_axis) < n_lat, refs[0][...], refs[1][...])


def _norm_mod_body(*refs, n_lat):
    n_src = 1 if n_lat is None else 2
    g_ref, sh_ref, sc_ref, h_ref = refs[n_src:]
    h = _rms(_read_rows(refs[:n_src], n_lat, 0), g_ref[...]) * (1.0 + sc_ref[...]) + sh_ref[...]
    h_ref[...] = h.astype(h_ref.dtype)


def _split_bf16(v):
    hi = v.astype(BF16)
    lo = (v - hi.astype(F32)).astype(BF16)
    return hi, lo


def _norm_router_body(x_ref, g_ref, sh_ref, sc_ref, rw_ref, rb_ref, h_ref, idx_ref, wt_ref):
    h = _rms(x_ref[...], g_ref[...]) * (1.0 + sc_ref[...]) + sh_ref[...]
    h_ref[...] = h.astype(BF16).astype(F32)
    h_hi, h_lo = _split_bf16(h)
    w_hi, w_lo = _split_bf16(rw_ref[...])
    n_exp = w_hi.shape[1]
    both = jnp.dot(h_hi, jnp.concatenate([w_hi, w_lo], axis=1), preferred_element_type=F32)
    logits = (both[:, :n_exp] + both[:, n_exp:]
              + jnp.dot(h_lo, w_hi, preferred_element_type=F32)) + rb_ref[...]
    lane = lax.broadcasted_iota(jnp.int32, logits.shape, 1).astype(F32)
    work = logits
    out_lane = lax.broadcasted_iota(jnp.int32, idx_ref.shape, 1)
    idx = jnp.zeros(idx_ref.shape, F32)
    ex = jnp.zeros(wt_ref.shape, F32)
    top = None
    total = None
    for k in range(TOP_K):
        m = jnp.max(work, axis=-1, keepdims=True)
        first = jnp.min(jnp.where(work == m, lane, float(n_exp)), axis=-1, keepdims=True)
        work = jnp.where(lane == first, -jnp.inf, work)
        if k == 0:
            top = m
        e = jnp.exp(m - top)
        total = e if k == 0 else total + e
        idx = jnp.where(out_lane == k, first, idx)
        ex = jnp.where(out_lane == k, e, ex)
    idx_ref[...] = idx.astype(jnp.int32)
    wt_ref[...] = ex / total


def _seg_map(tiles_per_batch, n_batch, which):
    def index(i):
        return (jnp.minimum(i // tiles_per_batch, n_batch) * 6 + which, 0, 0)
    return index


def _norm_call(xa, g, modl, rows, seq, n_batch, shift_idx, scale_idx, router=None):
    d = g.shape[0]
    tr = 256
    assert rows % tr == 0 and seq % tr == 0
    tpb = seq // tr
    row_spec = pl.BlockSpec((tr, d), lambda i: (i, 0))
    srcs, src_specs, n_lat = _row_sources(xa, tr, d, lambda i: 0, 0)
    in_specs = src_specs + [
        pl.BlockSpec((1, d), lambda i: (0, 0)),
        pl.BlockSpec((None, 1, d), _seg_map(tpb, n_batch, shift_idx)),
        pl.BlockSpec((None, 1, d), _seg_map(tpb, n_batch, scale_idx)),
    ]
    args = srcs + [g.reshape(1, d), modl, modl]
    if router is None:
        return pl.pallas_call(
            functools.partial(_norm_mod_body, n_lat=n_lat), grid=(rows // tr,), in_specs=in_specs,
            out_specs=row_spec, out_shape=jax.ShapeDtypeStruct((rows, d), BF16),
            compiler_params=_params(("arbitrary",)), name="norm_mod",
        )(*args)
    assert n_lat is None
    rw, rb = router
    n_exp = rw.shape[1]
    in_specs += [pl.BlockSpec((d, n_exp), lambda i: (0, 0)), pl.BlockSpec((1, n_exp), lambda i: (0, 0))]
    args += [rw, rb.reshape(1, n_exp)]
    return pl.pallas_call(
        _norm_router_body, grid=(rows // tr,), in_specs=in_specs,
        out_specs=[pl.BlockSpec((tr, d), lambda i: (i, 0)),
                   pl.BlockSpec((tr, LANES), lambda i: (i, 0)), pl.BlockSpec((tr, LANES), lambda i: (i, 0))],
        out_shape=[jax.ShapeDtypeStruct((rows, d), F32), jax.ShapeDtypeStruct((rows, LANES), jnp.int32),
                   jax.ShapeDtypeStruct((rows, LANES), F32)],
        compiler_params=_params(("arbitrary",)), name="norm_router",
    )(*args)


def _inproj_body(a_ref, w_ref, o_ref, wb_ref):
    @pl.when(pl.program_id(1) == 0)
    def _():
        wb_ref[...] = w_ref[...].astype(BF16)
    o_ref[...] = jnp.dot(a_ref[...], wb_ref[...], preferred_element_type=F32).astype(o_ref.dtype)


def _inproj_call(h, w_in, layer, rows, col_lo, col_hi):
    d = h.shape[1]
    tn = 1024
    assert col_lo % tn == 0 and col_hi % tn == 0
    tm = _tile(rows, 576, 16)
    j0 = col_lo // tn
    return pl.pallas_call(
        _inproj_body,
        grid=((col_hi - col_lo) // tn, rows // tm),
        in_specs=[
            pl.BlockSpec((tm, d), lambda j, i: (i, 0)),
            pl.BlockSpec((None, d, tn), lambda j, i: (layer, 0, j0 + j)),
        ],
        out_specs=pl.BlockSpec((tm, tn), lambda j, i: (i, j)),
        out_shape=jax.ShapeDtypeStruct((rows, col_hi - col_lo), BF16),
        scratch_shapes=[pltpu.VMEM((d, tn), BF16)],
        compiler_params=_params(("arbitrary", "arbitrary")),
        name="in_proj",
    )(h, w_in)


def _attn_body(*refs, local, scale):
    if local:
        (sink_ref, q_ref, kp_ref, kc_ref, kn_ref, vp_ref, vc_ref, vn_ref, kx_ref, vx_ref,
         cp_ref, cc_ref, cn_ref, sp_ref, sc_ref, sn_ref, bias_ref, o_ref) = refs
    else:
        sink_ref, q_ref, kx_ref, vx_ref, o_ref = refs
    blk = ATT_BLOCK
    half = ATT_HEAD_DIM // 2
    pair_w = 2 * ATT_HEAD_DIM
    n_pairs = kx_ref.shape[1] // pair_w
    slabs_per_pair = ATT_GROUP

    lane_q = lax.broadcasted_iota(jnp.int32, (blk, LANES), 1)
    first_half = (lane_q % ATT_HEAD_DIM) < half

    def rope(v, c_ref, s_ref):
        rot = jnp.where(first_half, pltpu.roll(v, LANES - half, 1), pltpu.roll(v, half, 1))
        return v * c_ref[...] + rot * s_ref[...]

    def placed(v):
        low = lax.broadcasted_iota(jnp.int32, v.shape, 1) < ATT_HEAD_DIM
        swapped = pltpu.roll(v, ATT_HEAD_DIM, 1)
        return [[jnp.where(low, v, 0.0).astype(BF16), jnp.where(low, 0.0, swapped).astype(BF16)],
                [jnp.where(low, swapped, 0.0).astype(BF16), jnp.where(low, 0.0, v).astype(BF16)]]

    top_rows = lax.broadcasted_iota(jnp.int32, (2 * blk, 1), 0) < blk
    work = [(pp, j, hh) for pp in range(n_pairs) for j in range(2) for hh in range(2)]

    v_vars, scores = [], []
    for pp in range(n_pairs):
        lanes = slice(pp * pair_w, (pp + 1) * pair_w)
        if local:
            keys = jnp.concatenate([
                rope(kp_ref[:, lanes].astype(F32), cp_ref, sp_ref),
                rope(kc_ref[:, lanes].astype(F32), cc_ref, sc_ref),
                rope(kn_ref[:, lanes].astype(F32), cn_ref, sn_ref),
                kx_ref[:, lanes].astype(F32)], axis=0)
            vals = jnp.concatenate([vp_ref[:, lanes], vc_ref[:, lanes], vn_ref[:, lanes], vx_ref[:, lanes]],
                                   axis=0).astype(F32)
        else:
            keys = kx_ref[:, lanes].astype(F32)
            vals = vx_ref[:, lanes].astype(F32)
        k_var = placed(keys)
        v_vars.append(placed(vals))
        q_slabs = []
        for c in range(slabs_per_pair):
            c0 = (pp * slabs_per_pair + c) * LANES
            qc = q_ref[:, c0:c0 + LANES].astype(F32)
            if local:
                qc = rope(qc, cc_ref, sc_ref)
            q_slabs.append((qc * (scale * LOG2_E)).astype(BF16))
        for j in range(2):
            lhs = jnp.concatenate([q_slabs[2 * j], q_slabs[2 * j + 1]], axis=0)
            for hh in range(2):
                s = lax.dot_general(lhs, k_var[j][hh], (((1,), (1,)), ((), ())),
                                    preferred_element_type=F32)
                if local:
                    s = s + bias_ref[...]
                scores.append(s)

    probs = []
    for (pp, j, hh), s in zip(work, scores):
        head0 = (pl.program_id(1) * n_pairs + pp) * (2 * ATT_GROUP) + ATT_GROUP * j + hh
        sink = jnp.where(top_rows, sink_ref[head0], sink_ref[head0 + 2]) * LOG2_E
        m = jnp.maximum(jnp.max(s, axis=-1, keepdims=True), sink)
        e = jnp.exp2(s - m)
        denom = jnp.sum(e, axis=-1, keepdims=True) + jnp.exp2(sink - m)
        probs.append((e.astype(BF16), 1.0 / denom))

    out = [jnp.zeros((blk, LANES), F32) for _ in range(n_pairs * slabs_per_pair)]
    for (pp, j, hh), (e, inv) in zip(work, probs):
        o = jnp.dot(e, v_vars[pp][j][hh], preferred_element_type=F32) * inv
        c = pp * slabs_per_pair + 2 * j
        out[c] = out[c] + o[:blk]
        out[c + 1] = out[c + 1] + o[blk:]
    for c, val in enumerate(out):
        o_ref[:, c * LANES:(c + 1) * LANES] = val.astype(o_ref.dtype)


def _attn_call(proj_kv, proj_q, sink, rope_tabs, cols, n_batch, seq, n_ctx, att_width, local):
    blk = ATT_BLOCK
    nb = seq // blk
    pair_w = 2 * ATT_HEAD_DIM * ATT_PAIRS_PER_STEP
    q_w = pair_w * ATT_GROUP
    assert att_width % q_w == 0
    n_pair = att_width // q_w
    k0 = cols["k_att"] // pair_w
    v0 = cols["v_att"] // pair_w
    q0 = cols["q_att"] // q_w
    ctx_row0 = (n_batch * seq) // n_ctx
    kx_spec = pl.BlockSpec((n_ctx, pair_w), lambda b, p, n: (ctx_row0 + b, k0 + p))
    vx_spec = pl.BlockSpec((n_ctx, pair_w), lambda b, p, n: (ctx_row0 + b, v0 + p))
    sink_spec = pl.BlockSpec(memory_space=pltpu.SMEM)
    scale = ATT_HEAD_DIM ** -0.5
    if local:
        cos_t, sin_t = rope_tabs

        def prev(n):
            return jnp.maximum(n - 1, 0)

        def nxt(n):
            return jnp.minimum(n + 1, nb - 1)

        def kv(c0, f):
            return pl.BlockSpec((blk, pair_w), lambda b, p, n: (b * nb + f(n), c0 + p))

        def tab(f):
            return pl.BlockSpec((blk, LANES), lambda b, p, n: (f(n), 0))

        ident = lambda n: n
        assert nb >= 2
        r = jnp.arange(2 * blk, dtype=jnp.int32)[:, None] % blk
        c = jnp.arange(3 * blk + n_ctx, dtype=jnp.int32)[None, :]
        band = (jnp.abs(c - blk - r) <= ATT_WINDOW) | (c >= 3 * blk)
        edge = jnp.stack([band & (c >= blk), band, band & ((c < 2 * blk) | (c >= 3 * blk))])
        bias = jnp.where(edge, 0.0, NEG_INF).astype(F32)
        bias_spec = pl.BlockSpec((None, 2 * blk, 3 * blk + n_ctx),
                                 lambda b, p, n: ((n > 0).astype(jnp.int32) + (n == nb - 1).astype(jnp.int32), 0, 0))
        in_specs = [sink_spec,
                    pl.BlockSpec((blk, q_w), lambda b, p, n: (b * nb + n, q0 + p)),
                    kv(k0, prev), kv(k0, ident), kv(k0, nxt),
                    kv(v0, prev), kv(v0, ident), kv(v0, nxt),
                    kx_spec, vx_spec,
                    tab(prev), tab(ident), tab(nxt), tab(prev), tab(ident), tab(nxt), bias_spec]
        args = [sink, proj_q] + [proj_kv] * 6 + [proj_kv, proj_kv] + [cos_t] * 3 + [sin_t] * 3 + [bias]
        grid = (n_batch, n_pair, nb)
        rows = n_batch * seq
        out_spec = pl.BlockSpec((blk, q_w), lambda b, p, n: (b * nb + n, p))
    else:
        nqb = n_ctx // blk
        qrow0 = (n_batch * seq) // blk
        in_specs = [sink_spec,
                    pl.BlockSpec((blk, q_w), lambda b, p, n: (qrow0 + b * nqb + n, q0 + p)),
                    kx_spec, vx_spec]
        args = [sink, proj_q, proj_kv, proj_kv]
        grid = (n_batch, n_pair, nqb)
        rows = n_batch * n_ctx
        out_spec = pl.BlockSpec((blk, q_w), lambda b, p, n: (b * nqb + n, p))
    return pl.pallas_call(
        functools.partial(_attn_body, local=local, scale=scale),
        grid=grid, in_specs=in_specs, out_specs=out_spec,
        out_shape=jax.ShapeDtypeStruct((rows, att_width), BF16),
        compiler_params=_params(("arbitrary",) * 3),
        name="window_attn" if local else "ctx_attn",
    )(*args)


def _dot_tn(a, b):
    return lax.dot_general(a, b, (((0,), (0,)), ((), ())), preferred_element_type=F32)


def _ret_body(*refs, latent, n_chunks, n_ctx, k_scale):
    if latent:
        (lg_ref, q_ref, k_ref, v_ref, g_ref, kx_ref, vx_ref, cos_ref, sin_ref,
         o_ref, qs_ref, ks_ref, cb_ref, st_ref, sf_ref) = refs
    else:
        lg_ref, q_ref, k_ref, v_ref, g_ref, o_ref, qs_ref, ks_ref, cb_ref, st_ref, sf_ref = refs
    head = pl.program_id(1)
    lg_f = lg_ref[0, head]
    lg_b = lg_ref[1, head]
    c = RET_CHUNK
    hd = q_ref.shape[1]
    hh = hd // 2
    idx = lax.broadcasted_iota(jnp.int32, (c, 1), 0).astype(F32)
    qdec_f = jnp.exp(lg_f * (idx + 1.0))
    kdec_f = jnp.exp(lg_f * (c - 1.0 - idx))
    qdec_b = jnp.exp(lg_b * (c - idx))
    kdec_b = jnp.exp(lg_b * idx)
    one = jnp.ones((1, 1), F32)
    cdec_f = jnp.exp(lg_f * c * one)
    cdec_b = jnp.exp(lg_b * c * one)
    rel = (lax.broadcasted_iota(jnp.int32, (c, c), 0) - lax.broadcasted_iota(jnp.int32, (c, c), 1)).astype(F32)
    dmat = (jnp.where(rel >= 0, jnp.exp(lg_f * jnp.maximum(rel, 0.0)), 0.0)
            + jnp.where(rel <= 0, jnp.exp(lg_b * jnp.maximum(-rel, 0.0)), 0.0))

    def rope(v, rows):
        v1 = v[:, :hh]
        v2 = v[:, hh:]
        cs = cos_ref[rows, :]
        sn = sin_ref[rows, :]
        return jnp.concatenate([v1 * cs - v2 * sn, v2 * cs + v1 * sn], axis=1)

    if latent:
        t = lax.broadcasted_iota(jnp.int32, (n_ctx, 1), 0).astype(F32)
        kx = kx_ref[...].astype(F32) * k_scale
        sf_ref[...] = _dot_tn((kx * jnp.exp(lg_f * (n_ctx - 1.0 - t))).astype(BF16), vx_ref[...])
        st_ref[...] = _dot_tn((kx * jnp.exp(lg_b * t)).astype(BF16), vx_ref[...])
    else:
        sf_ref[...] = jnp.zeros((hd, hd), F32)
        st_ref[...] = jnp.zeros((hd, hd), F32)


    def bwd(step, carry):
        rows = pl.ds(pl.multiple_of((n_chunks - 1 - step) * c, c), c)
        q = q_ref[rows, :].astype(F32)
        k = k_ref[rows, :].astype(F32)
        if latent:
            q = rope(q, rows)
            k = rope(k, rows)
        k = k * k_scale
        qs_ref[rows, :] = q.astype(BF16)
        ks_ref[rows, :] = k.astype(BF16)
        state = st_ref[...]
        cb_ref[rows, :] = jnp.dot((q * qdec_b).astype(BF16), state.astype(BF16), preferred_element_type=F32)
        st_ref[...] = cdec_b * state + _dot_tn((k * kdec_b).astype(BF16), v_ref[rows, :])
        return carry

    lax.fori_loop(0, n_chunks, bwd, 0, unroll=min(RET_UNROLL_BWD, n_chunks))

    st_ref[...] = sf_ref[...]

    def fwd(step, carry):
        rows = pl.ds(pl.multiple_of(step * c, c), c)
        qb = qs_ref[rows, :]
        kb = ks_ref[rows, :]
        v = v_ref[rows, :]
        scores = lax.dot_general(qb, kb, (((1,), (1,)), ((), ())), preferred_element_type=F32) * dmat
        o = jnp.dot(scores.astype(BF16), v, preferred_element_type=F32)
        state = st_ref[...]
        o = o + jnp.dot((qb.astype(F32) * qdec_f).astype(BF16), state.astype(BF16), preferred_element_type=F32)
        o = o + cb_ref[rows, :]
        st_ref[...] = cdec_f * state + _dot_tn((kb.astype(F32) * kdec_f).astype(BF16), v)
        mu = jnp.mean(o, axis=-1, keepdims=True)
        dev = o - mu
        var = jnp.mean(dev * dev, axis=-1, keepdims=True)
        y = dev * lax.rsqrt(var + GN_EPS)
        o_ref[rows, :] = (_silu(g_ref[rows, :].astype(F32)) * y).astype(o_ref.dtype)
        return carry

    lax.fori_loop(0, n_chunks, fwd, 0, unroll=min(RET_UNROLL_FWD, n_chunks))


def _ret_call(proj_kv, proj_q, log_gamma, rope_tabs, cols, n_batch, seq, n_ctx, ret_width, latent):
    hd = ret_width // RET_HEADS
    t_len = seq if latent else n_ctx
    assert t_len % RET_CHUNK == 0
    k0, v0, q0, g0 = (cols[name] // hd for name in ("k_ret", "v_ret", "q_ret", "g_ret"))
    ctx_row0 = (n_batch * seq) // n_ctx
    row0 = 0 if latent else ctx_row0

    def seq_spec(c0):
        return pl.BlockSpec((t_len, hd), lambda b, h: (row0 + b, c0 + h))

    in_specs = [pl.BlockSpec(memory_space=pltpu.SMEM),
                seq_spec(q0), seq_spec(k0), seq_spec(v0), seq_spec(g0)]
    args = [log_gamma, proj_q, proj_kv, proj_kv, proj_q]
    if latent:
        cos_t, sin_t = rope_tabs
        in_specs += [pl.BlockSpec((n_ctx, hd), lambda b, h: (ctx_row0 + b, k0 + h)),
                     pl.BlockSpec((n_ctx, hd), lambda b, h: (ctx_row0 + b, v0 + h)),
                     pl.BlockSpec((t_len, hd // 2), lambda b, h: (0, 0)),
                     pl.BlockSpec((t_len, hd // 2), lambda b, h: (0, 0))]
        args += [proj_kv, proj_kv, cos_t, sin_t]
    return pl.pallas_call(
        functools.partial(_ret_body, latent=latent, n_chunks=t_len // RET_CHUNK, n_ctx=n_ctx,
                          k_scale=hd ** -0.5),
        grid=(n_batch, RET_HEADS), in_specs=in_specs,
        out_specs=pl.BlockSpec((t_len, hd), lambda b, h: (b, h)),
        out_shape=jax.ShapeDtypeStruct((n_batch * t_len, ret_width), BF16),
        scratch_shapes=[pltpu.VMEM((t_len, hd), BF16), pltpu.VMEM((t_len, hd), BF16),
                        pltpu.VMEM((t_len, hd), F32), pltpu.VMEM((hd, hd), F32), pltpu.VMEM((hd, hd), F32)],
        compiler_params=_params(("arbitrary", "arbitrary")),
        name="retention" if latent else "ctx_retention",
    )(*args)


def _merge_gate_body(*refs, n_lat):
    n_src = 1 if n_lat is None else 2
    ga_ref, gr_ref, wa_ref, wr_ref, ba_ref, br_ref, z_ref, wab_ref, wrb_ref = refs[2 * n_src:]

    @pl.when(pl.program_id(1) == 0)
    def _():
        wab_ref[...] = wa_ref[...].astype(BF16)
        wrb_ref[...] = wr_ref[...].astype(BF16)
    a = jnp.dot(_read_rows(refs[:n_src], n_lat, 1), wab_ref[...], preferred_element_type=F32)
    r = jnp.dot(_read_rows(refs[n_src:2 * n_src], n_lat, 1), wrb_ref[...], preferred_element_type=F32)
    ga = jax.nn.sigmoid(ga_ref[...].astype(F32) + ba_ref[...])
    gr = jax.nn.sigmoid(gr_ref[...].astype(F32) + br_ref[...])
    z_ref[...] = (ga * a + gr * r).astype(z_ref.dtype)


def _merge_gate_call(ya, yr, proj, w_att, w_ret, gate_b, layer, rows, cols):
    d = w_att.shape[2]
    wa_rows, wr_rows = w_att.shape[1], w_ret.shape[1]
    tn = 512
    if isinstance(ya, tuple):
        tm = _tile(math.gcd(ya[0].shape[0], ya[1].shape[0]), 1024, 16)
    else:
        tm = _tile(rows, 1024, 16)
    ga0 = cols["gate_att"] // tn
    gr0 = cols["gate_ret"] // tn
    gb = gate_b.reshape(gate_b.shape[0], 1, 2 * d)
    ya_srcs, ya_specs, n_lat = _row_sources(ya, tm, wa_rows, lambda j, i: 0, 1)
    yr_srcs, yr_specs, _ = _row_sources(yr, tm, wr_rows, lambda j, i: 0, 1)
    return pl.pallas_call(
        functools.partial(_merge_gate_body, n_lat=n_lat),
        grid=(d // tn, rows // tm),
        in_specs=ya_specs + yr_specs + [
            pl.BlockSpec((tm, tn), lambda j, i: (i, ga0 + j)),
            pl.BlockSpec((tm, tn), lambda j, i: (i, gr0 + j)),
            pl.BlockSpec((None, wa_rows, tn), lambda j, i: (layer, 0, j)),
            pl.BlockSpec((None, wr_rows, tn), lambda j, i: (layer, 0, j)),
            pl.BlockSpec((None, 1, tn), lambda j, i: (layer, 0, j)),
            pl.BlockSpec((None, 1, tn), lambda j, i: (layer, 0, d // tn + j)),
        ],
        out_specs=pl.BlockSpec((tm, tn), lambda j, i: (i, j)),
        out_shape=jax.ShapeDtypeStruct((rows, d), BF16),
        scratch_shapes=[pltpu.VMEM((wa_rows, tn), BF16), pltpu.VMEM((wr_rows, tn), BF16)],
        compiler_params=_params(("arbitrary", "arbitrary")),
        name="merge_gate",
    )(*ya_srcs, *yr_srcs, proj, proj, w_att, w_ret, gb, gb)


def _resid_matmul_body(z_ref, w_ref, *refs, n_lat):
    n_src = 1 if n_lat is None else 2
    g_ref, o_ref, wb_ref = refs[n_src:]

    @pl.when(pl.program_id(1) == 0)
    def _():
        wb_ref[...] = w_ref[...].astype(BF16)
    y = jnp.dot(z_ref[...], wb_ref[...], preferred_element_type=F32)
    o_ref[...] = _read_rows(refs[:n_src], n_lat, 1) + g_ref[...] * y


def _resid_matmul_call(z, w, layer, xa, modl, gate_idx, rows, seq, n_batch):
    k, d = w.shape[1], w.shape[2]
    tn = 512
    tm = 1024 if (rows % 1024 == 0 and seq % 1024 == 0 and not isinstance(xa, tuple)) else 512
    assert rows % tm == 0 and seq % tm == 0
    tpb = seq // tm

    def gate_map(j, i):
        return (jnp.minimum(i // tpb, n_batch) * 6 + gate_idx, 0, j)

    srcs, src_specs, n_lat = _row_sources(xa, tm, tn, lambda j, i: j, 1)
    return pl.pallas_call(
        functools.partial(_resid_matmul_body, n_lat=n_lat),
        grid=(d // tn, rows // tm),
        in_specs=[
            pl.BlockSpec((tm, k), lambda j, i: (i, 0)),
            pl.BlockSpec((None, k, tn), lambda j, i: (layer, 0, j)),
        ] + src_specs + [pl.BlockSpec((None, 1, tn), gate_map)],
        out_specs=pl.BlockSpec((tm, tn), lambda j, i: (i, j)),
        out_shape=jax.ShapeDtypeStruct((rows, d), F32),
        scratch_shapes=[pltpu.VMEM((k, tn), BF16)],
        compiler_params=_params(("arbitrary", "arbitrary")),
        name="resid_matmul",
    )(z, w, *srcs, modl)


EXPERT_TILE = 128
COMBINE_TILE = 128
DISPATCH_TILE_CAP = 1152
COMBINE_CHUNK = 256


def _routing(top_idx, n_exp, tm, n_tiles):
    n_assign = top_idx.size
    e_flat = top_idx.reshape(n_assign)
    onehot = (e_flat[:, None] == jnp.arange(n_exp, dtype=jnp.int32)[None, :]).astype(jnp.int32)
    running = jnp.cumsum(onehot, axis=0)
    counts = running[-1]
    group = (counts + tm - 1) // tm * tm
    group_end = jnp.cumsum(group)
    slot = jnp.sum(onehot * (running - 1 + (group_end - group)[None, :]), axis=1).astype(jnp.int32)
    n_used = (group_end[-1] // tm).astype(jnp.int32).reshape(1)
    tile_start = jnp.arange(n_tiles, dtype=jnp.int32) * tm
    tile_expert = jnp.sum((group_end[None, :] <= tile_start[:, None]).astype(jnp.int32), axis=1)
    tile_expert = jnp.minimum(tile_expert, n_exp - 1).astype(jnp.int32)
    last_tile = jnp.stack([jnp.maximum(group_end - tm, 0), (group > 0).astype(jnp.int32)], axis=1)
    ids = jnp.arange(n_exp, dtype=jnp.int32)
    later = (ids[None, :] > ids[:, None]) & (group[None, :] > 0)
    next_of = jnp.min(jnp.where(later, ids[None, :], n_exp), axis=1)
    next_of = jnp.where(next_of == n_exp, -1, next_of)
    next_expert = jnp.sum((tile_expert[:, None] == ids[None, :]) * next_of[None, :], axis=1).astype(jnp.int32)
    return tile_expert, n_used, slot, last_tile.reshape(2 * n_exp).astype(jnp.int32), next_expert


def _expert_changed(te_ref, t):
    return jnp.logical_or(t == 0, te_ref[t] != te_ref[jnp.maximum(t - 1, 0)])


def _dispatch_body(slot_ref, last_ref, nu_ref, h_ref, xs_hbm, zero_ref, sem, zsem, *, tr, tm, n_exp, n_tiles):
    t = pl.program_id(0)

    @pl.when(t == 0)
    def _():
        zero_ref[...] = jnp.zeros(zero_ref.shape, zero_ref.dtype)

        def zero_copy(start):
            return pltpu.make_async_copy(zero_ref, xs_hbm.at[pl.ds(pl.multiple_of(start, tm), tm), :], zsem)

        def start_one(e, carry):
            @pl.when(last_ref[2 * e + 1] == 1)
            def _():
                zero_copy(last_ref[2 * e]).start()
            return carry

        def wait_one(e, carry):
            @pl.when(last_ref[2 * e + 1] == 1)
            def _():
                zero_copy(last_ref[2 * e]).wait()
            return carry

        def start_tail(i, carry):
            zero_copy(i * tm).start()
            return carry

        def wait_tail(i, carry):
            zero_copy(i * tm).wait()
            return carry

        lax.fori_loop(0, n_exp, start_one, 0)
        lax.fori_loop(nu_ref[0], n_tiles, start_tail, 0)
        lax.fori_loop(0, n_exp, wait_one, 0)
        lax.fori_loop(nu_ref[0], n_tiles, wait_tail, 0)

    base = t * (tr * TOP_K)

    def body(r, carry):
        for k in range(TOP_K):
            row = slot_ref[base + r * TOP_K + k]
            pltpu.async_copy(h_ref.at[pl.ds(r, 1), :], xs_hbm.at[pl.ds(row, 1), :], sem, priority=k % 2)
        return carry
    lax.fori_loop(0, tr, body, 0, unroll=2)

    for _ in range(TOP_K):
        pltpu.make_async_copy(h_ref, xs_hbm.at[pl.ds(0, tr), :], sem).wait()


def _dispatch_call(h2p, slot, last_tile, n_used, rows, n_tiles, n_exp):
    w = h2p.shape[1]
    tr = _tile(rows, DISPATCH_TILE_CAP, 8)
    tm = EXPERT_TILE
    grid_spec = pltpu.PrefetchScalarGridSpec(
        num_scalar_prefetch=3,
        grid=(rows // tr,),
        in_specs=[pl.BlockSpec((tr, w), lambda t, sl, lt, nu: (t, 0))],
        out_specs=pl.BlockSpec(memory_space=pl.ANY),
        scratch_shapes=[pltpu.VMEM((tm, w), F32), pltpu.SemaphoreType.DMA(()), pltpu.SemaphoreType.DMA(())],
    )
    return pl.pallas_call(
        functools.partial(_dispatch_body, tr=tr, tm=tm, n_exp=n_exp, n_tiles=n_tiles),
        grid_spec=grid_spec,
        out_shape=jax.ShapeDtypeStruct((n_tiles * tm, w), F32),
        compiler_params=_params(("arbitrary",)),
        name="expert_dispatch",
    )(slot, last_tile, n_used, h2p)


def _expert_body(te_ref, nu_ref, nx_ref, x_ref, wgu_hbm, bgu_ref, sel_ref, wd_hbm, bd_ref, y_ref,
                 sgu_ref, sd_ref, wgu_ref, wd_ref, sem, *, layer):
    t = pl.program_id(0)
    n_used = nu_ref[0]

    def weight_copies(e):
        return (pltpu.make_async_copy(wgu_hbm.at[layer, e], sgu_ref, sem.at[0]),
                pltpu.make_async_copy(wd_hbm.at[layer, e], sd_ref, sem.at[1]))

    def fetch(e):
        for copy in weight_copies(e):
            copy.start()

    @pl.when(t == 0)
    def _():
        fetch(te_ref[0])

    @pl.when(t < n_used)
    def _():
        @pl.when(_expert_changed(te_ref, t))
        def _():
            for copy in weight_copies(te_ref[t]):
                copy.wait()
            wgu_ref[...] = sgu_ref[...].astype(BF16)
            wd_ref[...] = sd_ref[...].astype(BF16)

            @pl.when(nx_ref[t] >= 0)
            def _():
                fetch(nx_ref[t])

        gu = jnp.dot(x_ref[...].astype(BF16), wgu_ref[...], preferred_element_type=F32) + bgu_ref[...]
        n2 = gu.shape[1]
        even = lax.broadcasted_iota(jnp.int32, gu.shape, 1) % 2 == 0
        gate = jnp.minimum(gu, SWIGLU_LIMIT)
        up = jnp.clip(pltpu.roll(gu, n2 - 1, 1), -SWIGLU_LIMIT, SWIGLU_LIMIT)
        act = jnp.where(even, (up + 1.0) * (gate * jax.nn.sigmoid(SWIGLU_ALPHA * gate)), 0.0)
        a = jnp.dot(act.astype(BF16), sel_ref[...], preferred_element_type=F32).astype(BF16)
        y_ref[...] = jnp.dot(a, wd_ref[...], preferred_element_type=F32) + bd_ref[...]

    @pl.when(t >= n_used)
    def _():
        y_ref[...] = jnp.zeros(y_ref.shape, y_ref.dtype)


def _used_tile(t, nu):
    return jnp.minimum(t, nu[0] - 1)


def _expert_call(xs, w_gu, b_gu, w_down, b_down, layer, tile_expert, n_used, next_expert, n_tiles):
    n_exp, d, ff2 = w_gu.shape[1:]
    ff = ff2 // 2
    tm = EXPERT_TILE
    sel = (jnp.arange(ff2, dtype=jnp.int32)[:, None] == 2 * jnp.arange(ff, dtype=jnp.int32)[None, :]).astype(BF16)
    grid_spec = pltpu.PrefetchScalarGridSpec(
        num_scalar_prefetch=3,
        grid=(n_tiles,),
        in_specs=[
            pl.BlockSpec((tm, d), lambda t, te, nu, nx: (_used_tile(t, nu), 0)),
            pl.BlockSpec(memory_space=pl.ANY),
            pl.BlockSpec((None, None, 1, ff2), lambda t, te, nu, nx: (layer, te[t], 0, 0)),
            pl.BlockSpec((ff2, ff), lambda t, te, nu, nx: (0, 0)),
            pl.BlockSpec(memory_space=pl.ANY),
            pl.BlockSpec((None, None, 1, d), lambda t, te, nu, nx: (layer, te[t], 0, 0)),
        ],
        out_specs=pl.BlockSpec((tm, d), lambda t, te, nu, nx: (t, 0)),
        scratch_shapes=[pltpu.VMEM((d, ff2), F32), pltpu.VMEM((ff, d), F32),
                        pltpu.VMEM((d, ff2), BF16), pltpu.VMEM((ff, d), BF16),
                        pltpu.SemaphoreType.DMA((2,))],
    )
    return pl.pallas_call(
        functools.partial(_expert_body, layer=layer),
        grid_spec=grid_spec,
        out_shape=jax.ShapeDtypeStruct((n_tiles * tm, d), F32),
        compiler_params=_params(("arbitrary",), EXPERT_VMEM_LIMIT),
        name="expert_mlp",
    )(tile_expert, n_used, next_expert, xs, w_gu, b_gu.reshape(b_gu.shape[0], n_exp, 1, ff2), sel,
      w_down, b_down.reshape(b_down.shape[0], n_exp, 1, d))


def _combine_body(pos_ref, y_hbm, w_ref, x_ref, g_ref, *refs, tc, post):
    if post == "next":
        ng_ref, nsh_ref, nsc_ref, o_ref, h_ref, ybuf, sem = refs
    elif post == "final":
        ng_ref, o_ref, ybuf, sem = refs
    else:
        o_ref, ybuf, sem = refs
    t = pl.program_id(0)
    last = pl.num_programs(0) - 1
    slot = lax.rem(t, 2)
    other = 1 - slot

    def row_copies(base, s, r):
        for k in range(TOP_K):
            row = pos_ref[base + r * TOP_K + k]
            pltpu.async_copy(y_hbm.at[pl.ds(row, 1), :], ybuf.at[s, k, pl.ds(r, 1), :], sem.at[s], priority=k % 2)

    def drain(s):
        for k in range(TOP_K):
            pltpu.make_async_copy(y_hbm.at[pl.ds(0, tc), :], ybuf.at[s, k], sem.at[s]).wait()

    @pl.when(t == 0)
    def _():
        def body(r, carry):
            row_copies(0, 0, r)
            return carry
        lax.fori_loop(0, tc, body, 0, unroll=2)

    drain(slot)

    next_base = jnp.minimum(t + 1, last) * (tc * TOP_K)
    half = ybuf.shape[-1] // 2
    ch = COMBINE_CHUNK
    n_first = half // ch
    n_second = 0 if post is None else 2 * half // ch
    rows_first = (tc if post is None else tc // 2) // n_first
    rows_second = 0 if post is None else (tc - rows_first * n_first) // n_second
    assert rows_first * n_first + rows_second * n_second == tc
    issued = [0]

    def issue_rows(count):
        for r in range(issued[0], issued[0] + count):
            row_copies(next_base, other, r)
        issued[0] += count

    w = w_ref[...]
    wk = [jnp.broadcast_to(w[:, k:k + 1], (tc, LANES)) for k in range(TOP_K)]
    sq = jnp.zeros((tc, 1), F32)
    for c0 in range(0, half, ch):
        issue_rows(rows_first)
        acc_lo = acc_hi = None
        for k in range(TOP_K):
            lo = ybuf[slot, k, :, c0:c0 + ch]
            hi = ybuf[slot, k, :, half + c0:half + c0 + ch]
            wt = jnp.tile(wk[k], (1, ch // LANES))
            acc_lo = wt * lo if k == 0 else acc_lo + wt * lo
            acc_hi = wt * hi if k == 0 else acc_hi + wt * hi
        for acc, c in ((acc_lo, c0), (acc_hi, half + c0)):
            new = x_ref[:, c:c + ch] + g_ref[:, c:c + ch] * acc
            o_ref[:, c:c + ch] = new
            if post is not None:
                sq = sq + jnp.sum(new * new, axis=-1, keepdims=True)
    if post is not None:
        inv = lax.rsqrt(sq * (1.0 / (2 * half)) + NORM_EPS)
        for c in range(0, 2 * half, ch):
            issue_rows(rows_second)
            norm = o_ref[:, c:c + ch] * inv * ng_ref[:, c:c + ch]
            if post == "final":
                o_ref[:, c:c + ch] = norm
            else:
                h_ref[:, c:c + ch] = (norm * (1.0 + nsc_ref[:, c:c + ch])
                                      + nsh_ref[:, c:c + ch]).astype(h_ref.dtype)

    @pl.when(t == last)
    def _():
        drain(other)


def _combine_call(y_sorted, slot, top_w, xa, modl, gate_idx, rows, seq, n_batch, post=None, post_args=()):
    d = xa.shape[1]
    tc = COMBINE_TILE
    assert rows % tc == 0 and seq % tc == 0
    tpb = seq // tc

    def seg_row(which):
        return lambda t, pos: (jnp.minimum(t // tpb, n_batch) * 6 + which, 0, 0)

    row_spec = pl.BlockSpec((tc, d), lambda t, pos: (t, 0))
    in_specs = [
        pl.BlockSpec(memory_space=pl.ANY),
        pl.BlockSpec((tc, LANES), lambda t, pos: (t, 0)),
        row_spec,
        pl.BlockSpec((None, 1, d), seg_row(gate_idx)),
    ]
    args = [slot, y_sorted, top_w, xa, modl]
    out_specs = row_spec
    out_shape = jax.ShapeDtypeStruct((rows, d), F32)
    if post is not None:
        in_specs.append(pl.BlockSpec((1, d), lambda t, pos: (0, 0)))
        args.append(post_args[0].reshape(1, d))
    if post == "next":
        in_specs += [pl.BlockSpec((None, 1, d), seg_row(0)), pl.BlockSpec((None, 1, d), seg_row(1))]
        args += [post_args[1], post_args[1]]
        out_specs = [row_spec, row_spec]
        out_shape = [out_shape, jax.ShapeDtypeStruct((rows, d), BF16)]
    grid_spec = pltpu.PrefetchScalarGridSpec(
        num_scalar_prefetch=1,
        grid=(rows // tc,),
        in_specs=in_specs,
        out_specs=out_specs,
        scratch_shapes=[pltpu.VMEM((2, TOP_K, tc, d), F32), pltpu.SemaphoreType.DMA((2,))],
    )
    return pl.pallas_call(
        functools.partial(_combine_body, tc=tc, post=post),
        grid_spec=grid_spec,
        out_shape=out_shape,
        compiler_params=_params(("arbitrary",)),
        name="expert_combine",
    )(*args)


def _axial_angles(seq, head_dim):
    rows = seq // GRID_W
    row = jnp.repeat(jnp.arange(rows, dtype=F32), GRID_W)
    col = jnp.tile(jnp.arange(GRID_W, dtype=F32), rows)
    n_freq = head_dim // 4
    inv_freq = ROPE_BASE ** (-jnp.arange(n_freq, dtype=F32) / n_freq)
    return jnp.concatenate([row[:, None] * inv_freq, col[:, None] * inv_freq], axis=-1)


def _rope_tables(seq, ret_head_dim):
    ang_a = _axial_angles(seq, ATT_HEAD_DIM)
    reps = LANES // ATT_HEAD_DIM
    cos_a = jnp.tile(jnp.concatenate([jnp.cos(ang_a), jnp.cos(ang_a)], axis=-1), (1, reps))
    sin_a = jnp.tile(jnp.concatenate([-jnp.sin(ang_a), jnp.sin(ang_a)], axis=-1), (1, reps))
    ang_r = _axial_angles(seq, ret_head_dim)
    return (cos_a, sin_a), (jnp.cos(ang_r), jnp.sin(ang_r))


def kernel(x, c, ctx, c_ctx, ada_w, ada_b, norm_mix_g, norm_ffn_g, w_in, att_sink, ret_decay_fwd,
           ret_decay_bwd, w_branch_att, w_branch_ret, branch_gate_b, w_out, router_w, router_b,
           exp_w_gu, exp_b_gu, exp_w_down, exp_b_down, final_norm_g):
    n_batch, seq, d = x.shape
    n_ctx = ctx.shape[1]
    depth = ada_w.shape[0]
    n_exp = exp_w_down.shape[1]
    att_width = w_branch_att.shape[1]
    ret_width = w_branch_ret.shape[1]
    kv_width = att_width // ATT_GROUP
    sizes = (("k_att", kv_width), ("v_att", kv_width), ("k_ret", ret_width), ("v_ret", ret_width),
             ("q_att", att_width), ("q_ret", ret_width), ("g_ret", ret_width),
             ("gate_att", d), ("gate_ret", d))
    cols, off = {}, 0
    kv_cols = 2 * kv_width + 2 * ret_width
    for name, width in sizes:
        cols[name] = off if off < kv_cols else off - kv_cols
        off += width
    in_cols = off
    assert in_cols == w_in.shape[2]

    lat_rows = n_batch * seq
    all_rows = lat_rows + n_batch * n_ctx
    xa = (x.reshape(lat_rows, d), ctx.reshape(n_batch * n_ctx, d))

    cond = jnp.zeros((16, d), F32).at[:n_batch].set(c).at[n_batch].set(c_ctx)
    mod = _mod_call(cond, ada_w, ada_b)
    rope_att, rope_ret = _rope_tables(seq, ret_width // RET_HEADS)

    modls = [mod[layer, :n_batch + 1].reshape((n_batch + 1) * 6, 1, d) for layer in range(depth)]
    h = _norm_call(xa, norm_mix_g[0], modls[0], all_rows, seq, n_batch, 0, 1)
    out = None
    for layer in range(depth):
        last = layer == depth - 1
        rows = lat_rows if last else all_rows
        modl = modls[layer]
        log_gamma = jnp.stack([jax.nn.log_sigmoid(ret_decay_fwd[layer].astype(F32)),
                               jax.nn.log_sigmoid(ret_decay_bwd[layer].astype(F32))])

        proj_kv = _inproj_call(h, w_in, layer, all_rows, 0, kv_cols)
        proj_q = _inproj_call(h, w_in, layer, rows, kv_cols, in_cols)

        ya = _attn_call(proj_kv, proj_q, att_sink[layer], rope_att, cols, n_batch, seq, n_ctx, att_width, True)
        yr = _ret_call(proj_kv, proj_q, log_gamma, rope_ret, cols, n_batch, seq, n_ctx, ret_width, True)
        if not last:
            ya_c = _attn_call(proj_kv, proj_q, att_sink[layer], None, cols, n_batch, seq, n_ctx, att_width, False)
            yr_c = _ret_call(proj_kv, proj_q, log_gamma, None, cols, n_batch, seq, n_ctx, ret_width, False)
            ya = (ya, ya_c)
            yr = (yr, yr_c)

        z = _merge_gate_call(ya, yr, proj_q, w_branch_att, w_branch_ret, branch_gate_b, layer, rows, cols)
        xa = _resid_matmul_call(z, w_out, layer, xa, modl, 2, rows, seq, n_batch)

        h2, top_idx, top_w = _norm_call(xa, norm_ffn_g[layer], modl, rows, seq, n_batch, 3, 4,
                                        router=(router_w[layer], router_b[layer]))
        n_tiles = -(-rows * TOP_K // EXPERT_TILE) + n_exp
        tile_expert, n_used, slot, last_tile, next_expert = _routing(
            top_idx[:, :TOP_K], n_exp, EXPERT_TILE, n_tiles)
        xs = _dispatch_call(h2, slot, last_tile, n_used, rows, n_tiles, n_exp)
        y_sorted = _expert_call(xs, exp_w_gu, exp_b_gu, exp_w_down, exp_b_down, layer,
                                tile_expert, n_used, next_expert, n_tiles)
        if last:
            out = _combine_call(y_sorted, slot, top_w, xa, modl, 5, rows, seq, n_batch,
                                post="final", post_args=(final_norm_g,))
        else:
            xa, h = _combine_call(y_sorted, slot, top_w, xa, modl, 5, rows, seq, n_batch,
                                  post="next", post_args=(norm_mix_g[layer + 1], modls[layer + 1]))
    return out.reshape(n_batch, seq, d)
```
